```python
import math
import jax, jax.numpy as jnp
from jax import lax
import numpy as np

D_MODEL = 1024
BATCH = 2
SEQ = 8192
DEPTH = 1
DEC_BATCH = 128
DEC_SEQ = 8
PAST_LEN = 8192
PAGE_SIZE = 128

MIX_WIDTH = D_MODEL
HEAD_DIM = 128
GDN_HEADS = 4
GDN_DK = HEAD_DIM
GDN_DV = HEAD_DIM
GDN_QK = GDN_HEADS * GDN_DK
GDN_WIDTH = GDN_HEADS * GDN_DV
GDN_CONV_W = 4
GDN_CONV_CH = 2 * GDN_QK + GDN_WIDTH
GDN_CHUNK = 64
MOBA_HEADS = 4
MOBA_WIDTH = MOBA_HEADS * HEAD_DIM
MOBA_BLOCK = 256
MOBA_TOPK = 3
MOBA_QBLOCK = 64
ROPE_THETA = 10000.0
NORM_EPS = 1e-6
IN_SPLITS = (GDN_CONV_CH, GDN_WIDTH, GDN_HEADS, GDN_HEADS, MOBA_WIDTH, MOBA_WIDTH, MOBA_WIDTH, MOBA_WIDTH)
IN_COLS = sum(IN_SPLITS)
IN_OFFSETS = tuple(int(o) for o in np.cumsum(IN_SPLITS)[:-1])

kernel_name = 'hymba_gdn_moba_step'


def rms_norm(x, w):
    xf = x.astype(jnp.float32)
    y = xf * lax.rsqrt(jnp.mean(xf * xf, axis=-1, keepdims=True) + NORM_EPS)
    return (y * w.astype(jnp.float32)).astype(x.dtype)


def l2_normalize(x):
    return x * lax.rsqrt(jnp.sum(x * x, axis=-1, keepdims=True) + NORM_EPS)


def rope(x, pos):
    half = HEAD_DIM // 2
    inv_freq = ROPE_THETA ** (-jnp.arange(half, dtype=jnp.float32) / half)
    ang = pos.astype(jnp.float32)[:, None] * inv_freq[None, :]
    cos = jnp.cos(ang)[:, None, :]
    sin = jnp.sin(ang)[:, None, :]
    xf = x.astype(jnp.float32)
    x1, x2 = xf[..., :half], xf[..., half:]
    return jnp.concatenate([x1 * cos - x2 * sin, x2 * cos + x1 * sin], axis=-1).astype(x.dtype)


def in_project(x, norm_w, w_in):
    h = rms_norm(x, norm_w)
    return jnp.split(h @ w_in, IN_OFFSETS, axis=-1)


def gdn_inputs(conv_in, conv_w, b, a, a_log, dt_bias):
    c = lax.conv_general_dilated(conv_in, conv_w[:, None, :].astype(conv_in.dtype), (1,), 'VALID',
                                 dimension_numbers=('NWC', 'WIO', 'NWC'),
                                 feature_group_count=GDN_CONV_CH)
    c = jax.nn.silu(c.astype(jnp.float32))
    B, T = c.shape[0], c.shape[1]
    q, k, v = jnp.split(c, (GDN_QK, 2 * GDN_QK), axis=-1)
    q = l2_normalize(q.reshape(B, T, GDN_HEADS, GDN_DK)) * (GDN_DK ** -0.5)
    k = l2_normalize(k.reshape(B, T, GDN_HEADS, GDN_DK))
    v = v.reshape(B, T, GDN_HEADS, GDN_DV)
    beta = jax.nn.sigmoid(b.astype(jnp.float32))
    g = -jnp.exp(a_log.astype(jnp.float32)) * jax.nn.softplus(
        a.astype(jnp.float32) + dt_bias.astype(jnp.float32))
    return q, k, v, beta, g


def gdn_chunked(q, k, v, beta, g):
    B, T, H = q.shape[0], q.shape[1], q.shape[2]
    C = GDN_CHUNK
    N = T // C

    def to_chunks(t):
        return jnp.moveaxis(t.reshape((B, N, C) + t.shape[2:]), 3, 1)

    q, k, v, beta, g = (to_chunks(t) for t in (q, k, v, beta, g))
    gc = jnp.cumsum(g, axis=-1)
    tril = jnp.tril(jnp.ones((C, C), dtype=bool))
    strict = jnp.tril(jnp.ones((C, C), dtype=bool), -1)
    gamma = jnp.exp(jnp.where(tril, gc[..., :, None] - gc[..., None, :], -jnp.inf))
    k_beta = k * beta[..., None]
    lower = jnp.where(strict, jnp.einsum('bhnid,bhnjd->bhnij', k_beta, k) * gamma, 0.0)
    a_mat = lower + jnp.eye(C, dtype=jnp.float32)
    rhs = jnp.concatenate([v * beta[..., None], k_beta * jnp.exp(gc)[..., None]], axis=-1)
    sol = lax.linalg.triangular_solve(a_mat, rhs, left_side=True, lower=True, unit_diagonal=True)
    u, w = sol[..., :GDN_DV], sol[..., GDN_DV:]
    intra = jnp.where(tril, jnp.einsum('bhnid,bhnjd->bhnij', q, k) * gamma, 0.0)
    q_dec = q * jnp.exp(gc)[..., None]
    k_dec = k * jnp.exp(gc[..., -1:] - gc)[..., None]
    g_last = jnp.exp(gc[..., -1])

    def step(S, inp):
        u_n, w_n, qd_n, kd_n, a_n, gl_n = inp
        v_new = u_n - jnp.einsum('bhck,bhkv->bhcv', w_n, S)
        o = jnp.einsum('bhck,bhkv->bhcv', qd_n, S) + jnp.einsum('bhij,bhjv->bhiv', a_n, v_new)
        S = S * gl_n[..., None, None] + jnp.einsum('bhck,bhcv->bhkv', kd_n, v_new)
        return S, o

    xs = tuple(jnp.moveaxis(t, 2, 0) for t in (u, w, q_dec, k_dec, intra, g_last))
    S0 = jnp.zeros((B, H, GDN_DK, GDN_DV), jnp.float32)
    S, o = lax.scan(step, S0, xs)
    o = jnp.moveaxis(o, 0, 2).transpose(0, 2, 3, 1, 4).reshape(B, T, H, GDN_DV)
    return o, S


def gdn_recurrent(q, k, v, beta, g, S0):
    def step(S, inp):
        q_t, k_t, v_t, b_t, g_t = inp
        S = S * jnp.exp(g_t)[..., None, None]
        kv = jnp.einsum('bhk,bhkv->bhv', k_t, S)
        S = S + jnp.einsum('bhk,bhv->bhkv', k_t, (v_t - kv) * b_t[..., None])
        return S, jnp.einsum('bhk,bhkv->bhv', q_t, S)

    xs = tuple(jnp.moveaxis(t, 1, 0) for t in (q, k, v, beta, g))
    S, o = lax.scan(step, S0.astype(jnp.float32), xs)
    return jnp.moveaxis(o, 0, 1), S


def moba_prompt(q, k, v):
    B, S, H, D = q.shape
    nb = -(-S // MOBA_BLOCK)
    pad = nb * MOBA_BLOCK - S
    qh = q.transpose(0, 2, 1, 3)
    kh = jnp.pad(k.transpose(0, 2, 1, 3), ((0, 0), (0, 0), (0, pad), (0, 0)))
    vh = jnp.pad(v.transpose(0, 2, 1, 3), ((0, 0), (0, 0), (0, pad), (0, 0)))
    kblk = kh.reshape(B, H, nb, MOBA_BLOCK, D)
    vblk = vh.reshape(B, H, nb, MOBA_BLOCK, D)
    kmean = jnp.mean(kblk.astype(jnp.float32), axis=3)
    topk = min(MOBA_TOPK, nb)
    nq = S // MOBA_QBLOCK
    q_blocks = jnp.moveaxis(qh.reshape(B, H, nq, MOBA_QBLOCK, D), 2, 0)
    bi = jnp.arange(B)[:, None, None, None]
    hi = jnp.arange(H)[None, :, None, None]
    scale = D ** -0.5

    def one_block(args):
        j, qb = args
        t0 = j * MOBA_QBLOCK
        pos_q = t0 + jnp.arange(MOBA_QBLOCK)
        cur = t0 // MOBA_BLOCK
        gate = jnp.einsum('bhqd,bhnd->bhqn', qb.astype(jnp.float32), kmean)
        gate = jnp.where(jnp.arange(nb) < cur, gate, -jnp.inf)
        _, sel = lax.top_k(gate, topk)
        sel_ok = jnp.arange(topk) < cur
        kg = kblk[bi, hi, sel]
        vg = vblk[bi, hi, sel]
        s_sel = jnp.einsum('bhqd,bhqrkd->bhqrk', qb, kg, preferred_element_type=jnp.float32) * scale
        s_sel = jnp.where(sel_ok[:, None], s_sel, -jnp.inf).reshape(B, H, MOBA_QBLOCK, topk * MOBA_BLOCK)
        k_own = lax.dynamic_slice_in_dim(kh, cur * MOBA_BLOCK, MOBA_BLOCK, axis=2)
        v_own = lax.dynamic_slice_in_dim(vh, cur * MOBA_BLOCK, MOBA_BLOCK, axis=2)
        s_own = jnp.einsum('bhqd,bhkd->bhqk', qb, k_own, preferred_element_type=jnp.float32) * scale
        pos_k = cur * MOBA_BLOCK + jnp.arange(MOBA_BLOCK)
        s_own = jnp.where(pos_k[None, :] <= pos_q[:, None], s_own, -jnp.inf)
        p = jax.nn.softmax(jnp.concatenate([s_sel, s_own], axis=-1), axis=-1).astype(vh.dtype)
        p_sel = p[..., :topk * MOBA_BLOCK].reshape(B, H, MOBA_QBLOCK, topk, MOBA_BLOCK)
        p_own = p[..., topk * MOBA_BLOCK:]
        return (jnp.einsum('bhqrk,bhqrkd->bhqd', p_sel, vg)
                + jnp.einsum('bhqk,bhkd->bhqd', p_own, v_own))

    o = lax.map(one_block, (jnp.arange(nq), q_blocks))
    return jnp.moveaxis(o, 0, 2).reshape(B, H, S, D).transpose(0, 2, 1, 3)


def moba_sample(q, k, v, cache_k, cache_v, page_table, layer):
    DB, T, H, D = q.shape
    ppb = MOBA_BLOCK // PAGE_SIZE
    n_past_blocks = PAST_LEN // MOBA_BLOCK
    n_own_pages = PAST_LEN // PAGE_SIZE - n_past_blocks * ppb
    n_own_past = n_own_pages * PAGE_SIZE
    scale = D ** -0.5
    qh = q.transpose(0, 2, 1, 3)
    own_pages = page_table[:, n_past_blocks * ppb:]
    k_own = jnp.concatenate([cache_k[layer, own_pages].reshape(DB, n_own_past, H, D).astype(k.dtype), k], axis=1)
    v_own = jnp.concatenate([cache_v[layer, own_pages].reshape(DB, n_own_past, H, D).astype(v.dtype), v], axis=1)
    jk = jnp.arange(n_own_past + T)
    it = jnp.arange(T)
    own_mask = (jk[None, :] < n_own_past) | (jk[None, :] - n_own_past <= it[:, None])
    s_own = jnp.einsum('bhtd,bkhd->bhtk', qh, k_own, preferred_element_type=jnp.float32) * scale
    s_own = jnp.where(own_mask, s_own, -jnp.inf)
    if n_past_blocks > 0:
        topk = min(MOBA_TOPK, n_past_blocks)
        past_pages = page_table[:, :n_past_blocks * ppb]
        kmean = jnp.mean(cache_k[layer, past_pages].astype(jnp.float32).reshape(
            DB, n_past_blocks, MOBA_BLOCK, H, D), axis=2)
        gate = jnp.einsum('bhtd,bnhd->bhtn', qh.astype(jnp.float32), kmean)
        _, sel = lax.top_k(gate, topk)
        logical = sel[..., None] * ppb + jnp.arange(ppb)
        phys = page_table[jnp.arange(DB)[:, None, None, None, None], logical]
        head_idx = jnp.arange(H)[None, :, None, None, None, None]
        slot = jnp.arange(PAGE_SIZE)
        kg = cache_k[layer, phys[..., None], slot, head_idx].reshape(DB, H, T, topk * MOBA_BLOCK, D).astype(k.dtype)
        vg = cache_v[layer, phys[..., None], slot, head_idx].reshape(DB, H, T, topk * MOBA_BLOCK, D).astype(v.dtype)
        s_sel = jnp.einsum('bhtd,bhtkd->bhtk', qh, kg, preferred_element_type=jnp.float32) * scale
        p = jax.nn.softmax(jnp.concatenate([s_sel, s_own], axis=-1), axis=-1).astype(v.dtype)
        n_sel = topk * MOBA_BLOCK
        o = (jnp.einsum('bhtk,bhtkd->bhtd', p[..., :n_sel], vg)
             + jnp.einsum('bhtk,bkhd->bhtd', p[..., n_sel:], v_own))
    else:
        p = jax.nn.softmax(s_own, axis=-1).astype(v.dtype)
        o = jnp.einsum('bhtk,bkhd->bhtd', p, v_own)
    return o.transpose(0, 2, 1, 3)


def branch_merge(o_gdn, z_a, o_moba, z_b, gdn_norm_w, w_out, dtype):
    B, T = o_gdn.shape[0], o_gdn.shape[1]
    ga = rms_norm(o_gdn, gdn_norm_w).reshape(B, T, GDN_WIDTH).astype(dtype) * jax.nn.silu(z_a)
    gb = o_moba.reshape(B, T, MOBA_WIDTH).astype(dtype) * jax.nn.silu(z_b)
    return jnp.concatenate([ga, gb], axis=-1) @ w_out


def setup_inputs(seed: int = 0) -> dict:
    key = jax.random.key(seed)
    ks = jax.random.split(key, 16)
    f32 = jnp.float32
    n_pages = PAST_LEN // PAGE_SIZE
    n_pool = (DEC_BATCH * n_pages * 5) // 4
    x_prompt = jax.random.normal(ks[0], (BATCH, SEQ, D_MODEL), f32)
    x_sample = jax.random.normal(ks[1], (DEC_BATCH, DEC_SEQ, D_MODEL), f32)
    cache_k = jax.random.normal(ks[2], (DEPTH, n_pool, PAGE_SIZE, MOBA_HEADS, HEAD_DIM), f32)
    cache_v = jax.random.normal(ks[3], (DEPTH, n_pool, PAGE_SIZE, MOBA_HEADS, HEAD_DIM), f32)
    state_gdn = 0.5 * jax.random.normal(ks[4], (DEPTH, DEC_BATCH, GDN_HEADS, GDN_DK, GDN_DV), f32)
    state_conv = jax.random.normal(ks[5], (DEPTH, DEC_BATCH, GDN_CONV_W - 1, GDN_CONV_CH), f32)
    page_table = jax.random.permutation(ks[6], n_pool)[:DEC_BATCH * n_pages].reshape(
        DEC_BATCH, n_pages).astype(jnp.int32)
    norm_in_w = 1.0 + 0.1 * jax.random.normal(ks[7], (DEPTH, D_MODEL), f32)
    w_in = jax.random.normal(ks[8], (DEPTH, D_MODEL, IN_COLS), f32) * (D_MODEL ** -0.5)
    conv_w = jax.random.normal(ks[9], (DEPTH, GDN_CONV_W, GDN_CONV_CH), f32) * (GDN_CONV_W ** -0.5)
    a_log = jnp.log(jax.random.uniform(ks[10], (DEPTH, GDN_HEADS), f32, 1.0, 16.0))
    dt = jnp.exp(jax.random.uniform(ks[11], (DEPTH, GDN_HEADS), f32, math.log(1e-3), math.log(1e-1)))
    dt_bias = dt + jnp.log(-jnp.expm1(-dt))
    gdn_norm_w = 1.0 + 0.1 * jax.random.normal(ks[12], (DEPTH, GDN_DV), f32)
    w_out = jax.random.normal(ks[13], (DEPTH, MIX_WIDTH, D_MODEL), f32) * (MIX_WIDTH ** -0.5)
    norm_f_w = 1.0 + 0.1 * jax.random.normal(ks[14], (D_MODEL,), f32)
    return {'x_prompt': x_prompt, 'x_sample': x_sample, 'cache_k': cache_k, 'cache_v': cache_v,
            'state_gdn': state_gdn, 'state_conv': state_conv, 'page_table': page_table,
            'norm_in_w': norm_in_w, 'w_in': w_in, 'conv_w': conv_w, 'a_log': a_log,
            'dt_bias': dt_bias, 'gdn_norm_w': gdn_norm_w, 'w_out': w_out, 'norm_f_w': norm_f_w}


def reference(x_prompt, x_sample, cache_k, cache_v, state_gdn, state_conv, page_table,
              norm_in_w, w_in, conv_w, a_log, dt_bias, gdn_norm_w, w_out, norm_f_w):
    B, S = x_prompt.shape[0], x_prompt.shape[1]
    DB, T = x_sample.shape[0], x_sample.shape[1]
    pos_p = jnp.arange(S, dtype=jnp.int32)
    pos_s = PAST_LEN + jnp.arange(T, dtype=jnp.int32)
    hp, hs = x_prompt, x_sample
    k_p, v_p, gdn_p, conv_p = [], [], [], []
    k_s, v_s, gdn_s, conv_s = [], [], [], []
    for layer in range(DEPTH):
        qkv_a, z_a, b_a, a_a, q_b, k_b, v_b, z_b = in_project(hp, norm_in_w[layer], w_in[layer])
        conv_in = jnp.pad(qkv_a, ((0, 0), (GDN_CONV_W - 1, 0), (0, 0)))
        q, k, v, beta, g = gdn_inputs(conv_in, conv_w[layer], b_a, a_a, a_log[layer], dt_bias[layer])
        o_gdn, s_gdn = gdn_chunked(q, k, v, beta, g)
        qr = rope(q_b.reshape(B, S, MOBA_HEADS, HEAD_DIM), pos_p)
        kr = rope(k_b.reshape(B, S, MOBA_HEADS, HEAD_DIM), pos_p)
        vr = v_b.reshape(B, S, MOBA_HEADS, HEAD_DIM)
        o_moba = moba_prompt(qr, kr, vr)
        hp = hp + branch_merge(o_gdn, z_a, o_moba, z_b, gdn_norm_w[layer], w_out[layer], hp.dtype)
        k_p.append(kr)
        v_p.append(vr)
        gdn_p.append(s_gdn)
        conv_p.append(conv_in[:, -(GDN_CONV_W - 1):])
        qkv_a, z_a, b_a, a_a, q_b, k_b, v_b, z_b = in_project(hs, norm_in_w[layer], w_in[layer])
        conv_in = jnp.concatenate([state_conv[layer].astype(qkv_a.dtype), qkv_a], axis=1)
        q, k, v, beta, g = gdn_inputs(conv_in, conv_w[layer], b_a, a_a, a_log[layer], dt_bias[layer])
        o_gdn, s_gdn = gdn_recurrent(q, k, v, beta, g, state_gdn[layer])
        qr = rope(q_b.reshape(DB, T, MOBA_HEADS, HEAD_DIM), pos_s)
        kr = rope(k_b.reshape(DB, T, MOBA_HEADS, HEAD_DIM), pos_s)
        vr = v_b.reshape(DB, T, MOBA_HEADS, HEAD_DIM)
        o_moba = moba_sample(qr, kr, vr, cache_k, cache_v, page_table, layer)
        hs = hs + branch_merge(o_gdn, z_a, o_moba, z_b, gdn_norm_w[layer], w_out[layer], hs.dtype)
        k_s.append(kr)
        v_s.append(vr)
        gdn_s.append(s_gdn)
        conv_s.append(conv_in[:, -(GDN_CONV_W - 1):])
    y_prompt = rms_norm(hp, norm_f_w)
    y_sample = rms_norm(hs, norm_f_w)
    return (y_prompt, y_sample, jnp.stack(k_p), jnp.stack(v_p), jnp.stack(gdn_p), jnp.stack(conv_p),
            jnp.stack(k_s), jnp.stack(v_s), jnp.stack(gdn_s), jnp.stack(conv_s))
```

```python
import functools
import math

import jax
import jax.numpy as jnp
from jax import lax
from jax.experimental import pallas as pl
from jax.experimental.pallas import tpu as pltpu

F32 = jnp.float32
BF16 = jnp.bfloat16

HEAD_DIM = 128
GDN_HEADS = 4
MOBA_HEADS = 4
GDN_QK = GDN_HEADS * HEAD_DIM
GDN_WIDTH = GDN_HEADS * HEAD_DIM
GDN_CONV_W = 4
GDN_CONV_CH = 2 * GDN_QK + GDN_WIDTH
GDN_CHUNK = 64
MOBA_WIDTH = MOBA_HEADS * HEAD_DIM
MOBA_BLOCK = 256
MOBA_TOPK = 3
PAGE_SIZE = 128
ROPE_THETA = 10000.0
NORM_EPS = 1e-6
LANES = 128
SUBLANES = 8
MASK_BIAS = -1e30

C_QKV = 0
C_ZA = GDN_CONV_CH
C_QB = C_ZA + GDN_WIDTH
C_KB = C_QB + MOBA_WIDTH
C_VB = C_KB + MOBA_WIDTH
C_ZB = C_VB + MOBA_WIDTH
C_BA = C_ZB + MOBA_WIDTH
IN_COLS_PAD = C_BA + LANES

ROW_TILE = 256
VMEM_LIMIT = 56 * 1024 * 1024


def _dot(a, b):
    return jnp.dot(a, b, preferred_element_type=F32)


def _dot_nt(a, b):
    return lax.dot_general(a, b, (((1,), (1,)), ((), ())), preferred_element_type=F32)


def _dot_tn(a, b):
    return lax.dot_general(a, b, (((0,), (0,)), ((), ())), preferred_element_type=F32)


def _dot_nt_f32(a, b):
    return lax.dot_general(a, b, (((1,), (1,)), ((), ())), preferred_element_type=F32,
                           precision=lax.Precision.HIGHEST)


def _dot_f32(a, b):
    return jnp.dot(a, b, preferred_element_type=F32, precision=lax.Precision.HIGHEST)


def _split_bf16(a):
    hi = a.astype(BF16)
    lo = (a - hi.astype(F32)).astype(BF16)
    return hi, lo


def _dot3(a, b):
    ah, al = _split_bf16(a)
    bh, bl = _split_bf16(b)
    return _dot(ah, bh) + (_dot(ah, bl) + _dot(al, bh))


def _silu(x):
    return x * jax.nn.sigmoid(x)


def _softplus(x):
    return jnp.maximum(x, 0.0) + jnp.log1p(jnp.exp(-jnp.abs(x)))


def _top3_members(gate, n_valid):
    lane_i = lax.broadcasted_iota(jnp.int32, gate.shape, 1)
    lane = lane_i.astype(F32)
    valid = lane_i < n_valid
    g = jnp.where(valid, gate, -jnp.inf)
    member = jnp.zeros(gate.shape, dtype=jnp.bool_)
    for _ in range(MOBA_TOPK):
        m = jnp.max(g, axis=1, keepdims=True)
        idx = jnp.min(jnp.where(g == m, lane, float(LANES)), axis=1, keepdims=True)
        pick = lane == idx
        member = jnp.logical_or(member, pick)
        g = jnp.where(pick, -jnp.inf, g)
    return jnp.logical_and(member, valid)


def _inproj_kernel(x_ref, nw_ref, w_ref, cos_ref, sin_ref,
                   qkv_ref, za_ref, q_ref, k_ref, v_ref, zb_ref, ba_ref, *prompt_refs):
    x = x_ref[...]
    ms = jnp.mean(x * x, axis=-1, keepdims=True)
    hb = (x * lax.rsqrt(ms + NORM_EPS) * nw_ref[...]).astype(BF16)

    def proj(c0, n):
        return _dot(hb, w_ref[:, c0:c0 + n])

    def rope(t):
        parts = []
        for h in range(MOBA_HEADS):
            th = t[:, h * HEAD_DIM:(h + 1) * HEAD_DIM]
            parts.append(th * cos_ref[...] + pltpu.roll(th, HEAD_DIM // 2, 1) * sin_ref[...])
        return jnp.concatenate(parts, axis=1)

    for j in range(GDN_CONV_CH // GDN_QK):
        qkv_ref[:, j * GDN_QK:(j + 1) * GDN_QK] = proj(C_QKV + j * GDN_QK, GDN_QK)
    za_ref[...] = proj(C_ZA, GDN_WIDTH)
    q_ref[...] = rope(proj(C_QB, MOBA_WIDTH))
    k = rope(proj(C_KB, MOBA_WIDTH))
    k_ref[...] = k
    v = proj(C_VB, MOBA_WIDTH)
    v_ref[...] = v
    zb_ref[...] = proj(C_ZB, MOBA_WIDTH)
    ba_ref[...] = proj(C_BA, LANES)
    if prompt_refs:
        kbf_ref, vbf_ref, kmean_ref = prompt_refs
        kbf_ref[...] = k.astype(BF16)
        vbf_ref[...] = v.astype(BF16)
        kmean_ref[0] = jnp.mean(k, axis=0, keepdims=True)


def _inproj(x2d, norm_w, w_p, cosf, sinf, *, prompt):
    m, d = x2d.shape
    tm = ROW_TILE
    assert m % tm == 0 and cosf.shape[0] % tm == 0
    n_pos = cosf.shape[0] // tm
    row = lambda i: (i, 0)
    full = lambda i: (0, 0)
    pos = (lambda i: (i % n_pos, 0)) if n_pos > 1 else full
    out_shape = [jax.ShapeDtypeStruct((m, GDN_CONV_CH), F32)]
    out_specs = [pl.BlockSpec((tm, GDN_CONV_CH), row)]
    for _ in range(5):
        out_shape.append(jax.ShapeDtypeStruct((m, MOBA_WIDTH), F32))
        out_specs.append(pl.BlockSpec((tm, MOBA_WIDTH), row))
    out_shape.append(jax.ShapeDtypeStruct((m, LANES), F32))
    out_specs.append(pl.BlockSpec((tm, LANES), row))
    if prompt:
        assert tm == MOBA_BLOCK
        for _ in range(2):
            out_shape.append(jax.ShapeDtypeStruct((m, MOBA_WIDTH), BF16))
            out_specs.append(pl.BlockSpec((tm, MOBA_WIDTH), row))
        out_shape.append(jax.ShapeDtypeStruct((m // tm, 1, MOBA_WIDTH), F32))
        out_specs.append(pl.BlockSpec((1, 1, MOBA_WIDTH), lambda i: (i, 0, 0)))
    return pl.pallas_call(
        _inproj_kernel,
        grid=(m // tm,),
        in_specs=[pl.BlockSpec((tm, d), row), pl.BlockSpec((1, d), full),
                  pl.BlockSpec((d, IN_COLS_PAD), full),
                  pl.BlockSpec((tm, HEAD_DIM), pos), pl.BlockSpec((tm, HEAD_DIM), pos)],
        out_specs=out_specs,
        out_shape=out_shape,
        compiler_params=pltpu.CompilerParams(dimension_semantics=("parallel",),
                                             vmem_limit_bytes=VMEM_LIMIT),
        name="inproj_prompt" if prompt else "inproj_sample",
    )(x2d, norm_w, w_p, cosf, sinf)


def _conv_silu_norm(c):
    c = _silu(c)
    qs, ks = [], []
    for h in range(GDN_HEADS):
        qh = c[:, h * HEAD_DIM:(h + 1) * HEAD_DIM]
        kh = c[:, GDN_QK + h * HEAD_DIM:GDN_QK + (h + 1) * HEAD_DIM]
        qs.append(qh * (lax.rsqrt(jnp.sum(qh * qh, axis=-1, keepdims=True) + NORM_EPS)
                        * (HEAD_DIM ** -0.5)))
        ks.append(kh * lax.rsqrt(jnp.sum(kh * kh, axis=-1, keepdims=True) + NORM_EPS))
    return qs, ks, c[:, 2 * GDN_QK:]


def _gdn_prep_kernel(x_ref, halo_ref, ba_ref, bat_ref, cw_ref, arow_ref, acol_ref,
                     n_ref, intra_ref, rhs_ref, qdec_ref, kdec_ref, glast_ref, buf_ref):
    i = pl.program_id(1)
    tt = x_ref.shape[1]
    c = GDN_CHUNK
    halo = jnp.where(i > 0, halo_ref[0], 0.0)
    buf_ref[0:SUBLANES, :] = halo
    buf_ref[SUBLANES:SUBLANES + tt, :] = x_ref[0]
    conv = buf_ref[SUBLANES - 3:SUBLANES - 3 + tt, :] * cw_ref[0:1, :]
    for w in range(1, GDN_CONV_W):
        conv = conv + buf_ref[SUBLANES - 3 + w:SUBLANES - 3 + w + tt, :] * cw_ref[w:w + 1, :]
    qs, ks, v = _conv_silu_norm(conv)

    ba = ba_ref[0]
    beta_full = jax.nn.sigmoid(ba)
    g_full = -jnp.exp(arow_ref[0:1, :]) * _softplus(ba + arow_ref[1:2, :])
    bat = bat_ref[0]
    gt_full = -jnp.exp(acol_ref[0]) * _softplus(bat + acol_ref[1])

    ri = lax.broadcasted_iota(jnp.int32, (c, c), 0)
    ci = lax.broadcasted_iota(jnp.int32, (c, c), 1)
    tril = ri >= ci
    strict = ri > ci
    ltri = jnp.where(tril, 1.0, 0.0).astype(F32)
    utri = jnp.where(ri <= ci, 1.0, 0.0).astype(F32)

    for cc in range(tt // c):
        rows = slice(cc * c, (cc + 1) * c)
        gcol_all = _dot_f32(ltri, g_full[rows, :])
        grow_all = _dot_f32(gt_full[:, rows], utri)
        for h in range(GDN_HEADS):
            gc_c = gcol_all[:, GDN_HEADS + h:GDN_HEADS + h + 1]
            gc_r = grow_all[GDN_HEADS + h:GDN_HEADS + h + 1, :]
            gamma = jnp.where(tril, jnp.exp(jnp.where(tril, gc_c - gc_r, 0.0)), 0.0)
            beta = beta_full[rows, h:h + 1]
            qh, kh = qs[h][rows, :], ks[h][rows, :]
            vh = v[rows, h * HEAD_DIM:(h + 1) * HEAD_DIM]
            kb = kh * beta
            k16 = kh.astype(BF16)
            n_ref[h, cc] = jnp.where(strict, _dot_nt(kb.astype(BF16), k16) * gamma, 0.0)
            intra_ref[h, cc] = jnp.where(tril, _dot_nt(qh.astype(BF16), k16) * gamma, 0.0)
            egc = jnp.exp(gc_c)
            rhs_ref[h, cc, :, 0:HEAD_DIM] = vh * beta
            rhs_ref[h, cc, :, HEAD_DIM:2 * HEAD_DIM] = kb * egc
            qdec_ref[h, cc] = qh * egc
            gl = gc_c[c - 1:c, :]
            kdec_ref[h, cc] = kh * jnp.exp(gl - gc_c)
            glast_ref[h, cc] = jnp.broadcast_to(jnp.exp(gl), (SUBLANES, LANES))


def _gdn_prep(qkv, ba, bat, conv_w, arow, acol):
    b, s, _ = qkv.shape
    tt = ROW_TILE
    c = GDN_CHUNK
    cpt = tt // c
    nc = s // c
    hb = tt // SUBLANES
    bh = b * GDN_HEADS

    def o(x):
        return (jax.ShapeDtypeStruct((bh, nc, c, x), F32),
                pl.BlockSpec((GDN_HEADS, cpt, c, x), lambda bi, i: (bi, i, 0, 0)))

    outs = [o(c), o(c), o(2 * HEAD_DIM), o(HEAD_DIM), o(HEAD_DIM)]
    outs.append((jax.ShapeDtypeStruct((bh, nc, SUBLANES, LANES), F32),
                 pl.BlockSpec((GDN_HEADS, cpt, SUBLANES, LANES), lambda bi, i: (bi, i, 0, 0))))
    return pl.pallas_call(
        _gdn_prep_kernel,
        grid=(b, s // tt),
        in_specs=[
            pl.BlockSpec((1, tt, GDN_CONV_CH), lambda bi, i: (bi, i, 0)),
            pl.BlockSpec((1, SUBLANES, GDN_CONV_CH), lambda bi, i: (bi, jnp.maximum(i * hb - 1, 0), 0)),
            pl.BlockSpec((1, tt, LANES), lambda bi, i: (bi, i, 0)),
            pl.BlockSpec((1, SUBLANES, tt), lambda bi, i: (bi, 0, i)),
            pl.BlockSpec((GDN_CONV_W, GDN_CONV_CH), lambda bi, i: (0, 0)),
            pl.BlockSpec((2, LANES), lambda bi, i: (0, 0)),
            pl.BlockSpec((2, SUBLANES, 1), lambda bi, i: (0, 0, 0)),
        ],
        out_specs=[x[1] for x in outs],
        out_shape=[x[0] for x in outs],
        scratch_shapes=[pltpu.VMEM((SUBLANES + tt, GDN_CONV_CH), F32)],
        compiler_params=pltpu.CompilerParams(dimension_semantics=("parallel", "parallel"),
                                             vmem_limit_bytes=VMEM_LIMIT),
        name="gdn_prep",
    )(qkv, qkv, ba, bat, conv_w, arow, acol)


def _tri_inv_kernel(n_ref, t_ref):
    c = n_ref.shape[0]
    col = lax.broadcasted_iota(jnp.int32, (c, LANES), 0)
    t_ref[...] = jnp.zeros(t_ref.shape, F32)

    def outer(i, carry):
        def inner(jg, acc):
            j0 = pl.multiple_of(jg * SUBLANES, SUBLANES)
            n_rows = n_ref[i, pl.ds(j0, SUBLANES), :]
            for r in range(SUBLANES):
                acc = acc - n_rows[r:r + 1, :] * t_ref[j0 + r]
            return acc
        groups = lax.shift_right_logical(i + (SUBLANES - 1), SUBLANES.bit_length() - 1)
        acc = lax.fori_loop(0, groups, inner, jnp.where(col == i, 1.0, 0.0).astype(F32))
        t_ref[i] = acc
        return carry

    lax.fori_loop(0, c, outer, 0)


def _tri_inv(nt):
    c, _, nmat = nt.shape
    assert nmat % LANES == 0
    spec = pl.BlockSpec((c, c, LANES), lambda g: (0, 0, g))
    return pl.pallas_call(
        _tri_inv_kernel,
        grid=(nmat // LANES,),
        in_specs=[spec],
        out_specs=spec,
        out_shape=jax.ShapeDtypeStruct(nt.shape, F32),
        compiler_params=pltpu.CompilerParams(dimension_semantics=("parallel",)),
        name="gdn_tri_inv",
    )(nt)


def _gdn_scan_kernel(t_ref, rhs_ref, qdec_ref, kdec_ref, intra_ref, glast_ref, o_ref, s_ref):
    i = pl.program_id(0)
    bh, ct, c, _ = t_ref.shape

    @pl.when(i == 0)
    def _():
        s_ref[...] = jnp.zeros(s_ref.shape, F32)

    for cc in range(ct):
        for n in range(bh):
            sol = _dot3(t_ref[n, cc], rhs_ref[n, cc])
            u, w = sol[:, :HEAD_DIM], sol[:, HEAD_DIM:]
            s = s_ref[n]
            s16 = s.astype(BF16)
            wq = jnp.concatenate([w, qdec_ref[n, cc]], axis=0).astype(BF16)
            wqs = _dot(wq, s16)
            v_new = u - wqs[:c]
            vn16 = v_new.astype(BF16)
            o = wqs[c:] + _dot(intra_ref[n, cc].astype(BF16), vn16)
            s_ref[n] = s * glast_ref[n, cc, 0:1, :] + _dot_tn(kdec_ref[n, cc].astype(BF16), vn16)
            b, h = n // GDN_HEADS, n % GDN_HEADS
            o_ref[b, cc * c:(cc + 1) * c, h * HEAD_DIM:(h + 1) * HEAD_DIM] = o


def _gdn_scan(t, rhs, qdec, kdec, intra, glast, b):
    bh, nc, c, _ = t.shape
    ct = 4
    assert nc % ct == 0

    def spec(x):
        return pl.BlockSpec((bh, ct, x.shape[2], x.shape[3]), lambda i: (0, i, 0, 0))

    return pl.pallas_call(
        _gdn_scan_kernel,
        grid=(nc // ct,),
        in_specs=[spec(t), spec(rhs), spec(qdec), spec(kdec), spec(intra), spec(glast)],
        out_specs=[pl.BlockSpec((b, ct * c, GDN_WIDTH), lambda i: (0, i, 0)),
                   pl.BlockSpec((bh, HEAD_DIM, HEAD_DIM), lambda i: (0, 0, 0))],
        out_shape=[jax.ShapeDtypeStruct((b, nc * c, GDN_WIDTH), F32),
                   jax.ShapeDtypeStruct((bh, HEAD_DIM, HEAD_DIM), F32)],
        compiler_params=pltpu.CompilerParams(dimension_semantics=("arbitrary",),
                                             vmem_limit_bytes=VMEM_LIMIT),
        name="gdn_scan",
    )(t, rhs, qdec, kdec, intra, glast)


def _moba_prompt_kernel(q_ref, k_ref, v_ref, kmean_ref, o_ref):
    i = pl.program_id(2)
    tq = q_ref.shape[1]
    blk = MOBA_BLOCK
    scale = HEAD_DIM ** -0.5
    q = q_ref[0]
    q16 = q.astype(BF16)
    kmean = jnp.concatenate(
        [kmean_ref[0], jnp.zeros((LANES - kmean_ref.shape[1], HEAD_DIM), F32)], axis=0)
    member = _top3_members(_dot_nt_f32(q, kmean), i)
    q_aug = jnp.concatenate([q16, jnp.where(member, 0.0, MASK_BIAS).astype(BF16)], axis=1)
    lane = lax.broadcasted_iota(jnp.int32, (blk, LANES), 1)

    start = pl.multiple_of(i * blk, blk)
    s = _dot_nt(q16, k_ref[0, pl.ds(start, blk), :]) * scale
    rq = lax.broadcasted_iota(jnp.int32, (tq, blk), 0)
    ck = lax.broadcasted_iota(jnp.int32, (tq, blk), 1)
    s = jnp.where(ck <= rq, s, -jnp.inf)
    m = jnp.max(s, axis=1, keepdims=True)
    p = jnp.exp(s - m)
    l = jnp.sum(p, axis=1, keepdims=True)
    acc = _dot(p.astype(BF16), v_ref[0, pl.ds(start, blk), :])

    def body(j, carry):
        m, l, acc = carry
        st = pl.multiple_of(j * blk, blk)
        k_aug = jnp.concatenate(
            [k_ref[0, pl.ds(st, blk), :], jnp.where(lane == j, 1.0, 0.0).astype(BF16)], axis=1)
        s = _dot_nt(q_aug, k_aug) * scale
        m_new = jnp.maximum(m, jnp.max(s, axis=1, keepdims=True))
        alpha = jnp.exp(m - m_new)
        p = jnp.exp(s - m_new)
        l = l * alpha + jnp.sum(p, axis=1, keepdims=True)
        acc = acc * alpha + _dot(p.astype(BF16), v_ref[0, pl.ds(st, blk), :])
        return m_new, l, acc

    m, l, acc = lax.fori_loop(0, i, body, (m, l, acc))
    o_ref[0] = acc / l


def _moba_prompt(q, kbf, vbf, kmean):
    b, s, _ = q.shape
    tq = MOBA_BLOCK
    nb = s // MOBA_BLOCK
    assert s % MOBA_BLOCK == 0 and nb <= LANES
    return pl.pallas_call(
        _moba_prompt_kernel,
        grid=(b, MOBA_HEADS, s // tq),
        in_specs=[pl.BlockSpec((1, tq, HEAD_DIM), lambda bi, h, i: (bi, i, h)),
                  pl.BlockSpec((1, s, HEAD_DIM), lambda bi, h, i: (bi, 0, h)),
                  pl.BlockSpec((1, s, HEAD_DIM), lambda bi, h, i: (bi, 0, h)),
                  pl.BlockSpec((1, nb, HEAD_DIM), lambda bi, h, i: (bi, 0, h))],
        out_specs=pl.BlockSpec((1, tq, HEAD_DIM), lambda bi, h, i: (bi, i, h)),
        out_shape=jax.ShapeDtypeStruct((b, s, MOBA_WIDTH), F32),
        compiler_params=pltpu.CompilerParams(
            dimension_semantics=("parallel", "parallel", "arbitrary"),
            vmem_limit_bytes=VMEM_LIMIT),
        name="moba_prompt",
    )(q, kbf, vbf, kmean)


def _gdn_sample_prep_kernel(x_ref, st_ref, ba_ref, cw_ref, arow_ref,
                            q_ref, k_ref, v_ref, beta_ref, dec_ref):
    t_len = x_ref.shape[0]
    ctx = GDN_CONV_W - 1

    def src(tt):
        return st_ref[tt + ctx] if tt < 0 else x_ref[tt]

    for t in range(t_len):
        conv = src(t - ctx) * cw_ref[0:1, :]
        for w in range(1, GDN_CONV_W):
            conv = conv + src(t - ctx + w) * cw_ref[w:w + 1, :]
        qs, ks, v = _conv_silu_norm(conv)
        q_ref[t] = jnp.concatenate(qs, axis=1)
        k_ref[t] = jnp.concatenate(ks, axis=1)
        v_ref[t] = v
        ba = ba_ref[t]
        beta_full = jax.nn.sigmoid(ba)
        dec_full = jnp.exp(-jnp.exp(arow_ref[0:1, :]) * _softplus(ba + arow_ref[1:2, :]))
        rows = ba.shape[0]
        beta_ref[t] = jnp.concatenate(
            [jnp.broadcast_to(beta_full[:, h:h + 1], (rows, HEAD_DIM)) for h in range(GDN_HEADS)], axis=1)
        dec_ref[t] = jnp.concatenate(
            [jnp.broadcast_to(dec_full[:, GDN_HEADS + h:GDN_HEADS + h + 1], (rows, HEAD_DIM))
             for h in range(GDN_HEADS)], axis=1)


def _gdn_sample_prep(x_t, st_t, ba_t, conv_w, arow):
    t_len, db, _ = x_t.shape
    bt = min(db, 32)
    assert db % bt == 0

    def spec(t, w):
        return pl.BlockSpec((t, bt, w), lambda i: (0, i, 0))

    out = jax.ShapeDtypeStruct((t_len, db, GDN_WIDTH), F32)
    return pl.pallas_call(
        _gdn_sample_prep_kernel,
        grid=(db // bt,),
        in_specs=[spec(t_len, GDN_CONV_CH), spec(GDN_CONV_W - 1, GDN_CONV_CH), spec(t_len, LANES),
                  pl.BlockSpec((GDN_CONV_W, GDN_CONV_CH), lambda i: (0, 0)),
                  pl.BlockSpec((2, LANES), lambda i: (0, 0))],
        out_specs=[spec(t_len, GDN_WIDTH)] * 5,
        out_shape=[out] * 5,
        compiler_params=pltpu.CompilerParams(dimension_semantics=("parallel",),
                                             vmem_limit_bytes=VMEM_LIMIT),
        name="gdn_sample_prep",
    )(x_t, st_t, ba_t, conv_w, arow)


def _gdn_sample_rec_kernel(s0_ref, qt_ref, kt_ref, v_ref, beta_ref, dec_ref, o_ref, s_ref):
    bt = s0_ref.shape[0]
    t_len = v_ref.shape[1]

    def per_batch(bi, carry):
        v, beta, dec = v_ref[bi], beta_ref[bi], dec_ref[bi]
        outs = []
        for h in range(GDN_HEADS):
            cols = slice(h * HEAD_DIM, (h + 1) * HEAD_DIM)
            s = s0_ref[bi, h]
            kt = kt_ref[bi, h]
            qt = qt_ref[bi, h]
            rows = []
            for t in range(t_len):
                s = s * dec[t:t + 1, cols]
                kcol = kt[:, t:t + 1]
                kv = jnp.sum(kcol * s, axis=0, keepdims=True)
                upd = (v[t:t + 1, cols] - kv) * beta[t:t + 1, cols]
                s = s + kcol * upd
                rows.append(jnp.sum(qt[:, t:t + 1] * s, axis=0, keepdims=True))
            s_ref[bi, h] = s
            outs.append(jnp.concatenate(rows, axis=0))
        o_ref[bi] = jnp.concatenate(outs, axis=1)
        return carry

    lax.fori_loop(0, bt, per_batch, 0)


def _gdn_sample_rec(s0, qt, kt, v, beta, dec):
    db, t_len, _ = v.shape
    bt = min(db, 8)
    assert db % bt == 0
    st_spec = pl.BlockSpec((bt, GDN_HEADS, HEAD_DIM, HEAD_DIM), lambda i: (i, 0, 0, 0))
    tr_spec = pl.BlockSpec((bt, GDN_HEADS, HEAD_DIM, t_len), lambda i: (i, 0, 0, 0))
    tok_spec = pl.BlockSpec((bt, t_len, GDN_WIDTH), lambda i: (i, 0, 0))
    return pl.pallas_call(
        _gdn_sample_rec_kernel,
        grid=(db // bt,),
        in_specs=[st_spec, tr_spec, tr_spec, tok_spec, tok_spec, tok_spec],
        out_specs=[tok_spec, st_spec],
        out_shape=[jax.ShapeDtypeStruct((db, t_len, GDN_WIDTH), F32),
                   jax.ShapeDtypeStruct(s0.shape, F32)],
        compiler_params=pltpu.CompilerParams(dimension_semantics=("parallel",),
                                             vmem_limit_bytes=VMEM_LIMIT),
        name="gdn_sample_rec",
    )(s0, qt, kt, v, beta, dec)


PAGES_PER_STEP = 16


def _page_specs(n_pages):
    def mk(slot):
        return pl.BlockSpec((1, PAGE_SIZE, MOBA_WIDTH),
                            lambda b, g, pt: (pt[b * n_pages + g * PAGES_PER_STEP + slot], 0, 0))
    return [mk(s) for s in range(PAGES_PER_STEP)]


def _kmean_sample_kernel(pt_ref, *refs):
    k_refs, o_ref = refs[:PAGES_PER_STEP], refs[PAGES_PER_STEP]
    ppb = MOBA_BLOCK // PAGE_SIZE
    rows = []
    for n in range(PAGES_PER_STEP // ppb):
        tot = jnp.sum(k_refs[n * ppb][0], axis=0, keepdims=True)
        for r in range(1, ppb):
            tot = tot + jnp.sum(k_refs[n * ppb + r][0], axis=0, keepdims=True)
        rows.append(tot * (1.0 / MOBA_BLOCK))
    o_ref[0] = jnp.concatenate(rows, axis=0)


def _kmean_sample(page_flat, cache_k, db, n_pages):
    ppb = MOBA_BLOCK // PAGE_SIZE
    nb = n_pages // ppb
    bps = PAGES_PER_STEP // ppb
    assert n_pages % PAGES_PER_STEP == 0
    return pl.pallas_call(
        _kmean_sample_kernel,
        grid_spec=pltpu.PrefetchScalarGridSpec(
            num_scalar_prefetch=1,
            grid=(db, n_pages // PAGES_PER_STEP),
            in_specs=_page_specs(n_pages),
            out_specs=pl.BlockSpec((1, bps, MOBA_WIDTH), lambda b, g, pt: (b, g, 0)),
        ),
        out_shape=jax.ShapeDtypeStruct((db, nb, MOBA_WIDTH), F32),
        compiler_params=pltpu.CompilerParams(dimension_semantics=("parallel", "arbitrary"),
                                             vmem_limit_bytes=VMEM_LIMIT),
        name="moba_sample_kmean",
    )(page_flat, *([cache_k] * PAGES_PER_STEP))


def _moba_sample_kernel(pt_ref, q_ref, kn_ref, vn_ref, kmean_ref, *refs):
    k_refs = refs[:PAGES_PER_STEP]
    v_refs = refs[PAGES_PER_STEP:2 * PAGES_PER_STEP]
    o_ref, qbd_ref, mem_ref, m_ref, l_ref, acc_ref = refs[2 * PAGES_PER_STEP:]
    g = pl.program_id(1)
    t_len = q_ref.shape[1]
    rows = MOBA_HEADS * t_len
    ppb = MOBA_BLOCK // PAGE_SIZE
    scale = HEAD_DIM ** -0.5
    t_shift, d_shift = t_len.bit_length() - 1, HEAD_DIM.bit_length() - 1
    assert 1 << t_shift == t_len and 1 << d_shift == HEAD_DIM

    @pl.when(g == 0)
    def _():
        q = q_ref[0]
        rh = lax.shift_right_logical(lax.broadcasted_iota(jnp.int32, (rows, MOBA_WIDTH), 0), t_shift)
        lh = lax.shift_right_logical(lax.broadcasted_iota(jnp.int32, (rows, MOBA_WIDTH), 1), d_shift)
        qbd = jnp.where(rh == lh, jnp.concatenate([q] * MOBA_HEADS, axis=0), 0.0)
        qbd_ref[...] = qbd
        nb = kmean_ref.shape[1]
        kmean = jnp.concatenate([kmean_ref[0], jnp.zeros((LANES - nb, MOBA_WIDTH), F32)], axis=0)
        member = _top3_members(_dot_nt_f32(qbd, kmean), nb)
        mem_ref[...] = jnp.where(member, 1.0, 0.0)
        pad = jnp.zeros((LANES - t_len, MOBA_WIDTH), F32)
        kn = jnp.concatenate([kn_ref[0], pad], axis=0).astype(BF16)
        vn = jnp.concatenate([vn_ref[0], pad], axis=0).astype(BF16)
        s = _dot_nt(qbd.astype(BF16), kn) * scale
        tq = jnp.bitwise_and(lax.broadcasted_iota(jnp.int32, (rows, LANES), 0), t_len - 1)
        tk = lax.broadcasted_iota(jnp.int32, (rows, LANES), 1)
        s = jnp.where(tk <= tq, s, -jnp.inf)
        m = jnp.max(s, axis=1, keepdims=True)
        p = jnp.exp(s - m)
        m_ref[...] = jnp.broadcast_to(m, m_ref.shape)
        l_ref[...] = jnp.broadcast_to(jnp.sum(p, axis=1, keepdims=True), l_ref.shape)
        acc_ref[...] = _dot(p.astype(BF16), vn)

    qbd16 = qbd_ref[...].astype(BF16)
    member = mem_ref[...]
    lane = lax.broadcasted_iota(jnp.int32, (rows, LANES), 1)
    m = m_ref[:, 0:1]
    l = l_ref[:, 0:1]
    acc = acc_ref[...]
    for slot in range(PAGES_PER_STEP):
        blk = g * (PAGES_PER_STEP // ppb) + slot // ppb
        sel = jnp.sum(jnp.where(lane == blk, member, 0.0), axis=1, keepdims=True) > 0.0
        s = _dot_nt(qbd16, k_refs[slot][0].astype(BF16)) * scale
        s = jnp.where(sel, s, -jnp.inf)
        m_new = jnp.maximum(m, jnp.max(s, axis=1, keepdims=True))
        alpha = jnp.exp(m - m_new)
        p = jnp.exp(s - m_new)
        l = l * alpha + jnp.sum(p, axis=1, keepdims=True)
        acc = acc * alpha + _dot(p.astype(BF16), v_refs[slot][0].astype(BF16))
        m = m_new
    m_ref[...] = jnp.broadcast_to(m, m_ref.shape)
    l_ref[...] = jnp.broadcast_to(l, l_ref.shape)
    acc_ref[...] = acc

    @pl.when(g == pl.num_programs(1) - 1)
    def _():
        res = acc / l
        o_ref[0] = jnp.concatenate(
            [res[h * t_len:(h + 1) * t_len, h * HEAD_DIM:(h + 1) * HEAD_DIM]
             for h in range(MOBA_HEADS)], axis=1)


def _moba_sample(page_flat, q, kn, vn, kmean, cache_k, cache_v, n_pages):
    db, t_len, _ = q.shape
    nb = kmean.shape[1]
    rows = MOBA_HEADS * t_len
    tok = pl.BlockSpec((1, t_len, MOBA_WIDTH), lambda b, g, pt: (b, 0, 0))
    return pl.pallas_call(
        _moba_sample_kernel,
        grid_spec=pltpu.PrefetchScalarGridSpec(
            num_scalar_prefetch=1,
            grid=(db, n_pages // PAGES_PER_STEP),
            in_specs=[tok, tok, tok, pl.BlockSpec((1, nb, MOBA_WIDTH), lambda b, g, pt: (b, 0, 0))]
            + _page_specs(n_pages) + _page_specs(n_pages),
            out_specs=tok,
            scratch_shapes=[pltpu.VMEM((rows, MOBA_WIDTH), F32), pltpu.VMEM((rows, LANES), F32),
                            pltpu.VMEM((rows, LANES), F32), pltpu.VMEM((rows, LANES), F32),
                            pltpu.VMEM((rows, MOBA_WIDTH), F32)],
        ),
        out_shape=jax.ShapeDtypeStruct((db, t_len, MOBA_WIDTH), F32),
        compiler_params=pltpu.CompilerParams(dimension_semantics=("parallel", "arbitrary"),
                                             vmem_limit_bytes=VMEM_LIMIT),
        name="moba_sample_attn",
    )(page_flat, q, kn, vn, kmean, *([cache_k] * PAGES_PER_STEP), *([cache_v] * PAGES_PER_STEP))


def _merge_kernel(og_ref, za_ref, om_ref, zb_ref, x_ref, gw_ref, wo_ref, nf_ref, y_ref):
    og = og_ref[...]
    parts = []
    for h in range(GDN_HEADS):
        oh = og[:, h * HEAD_DIM:(h + 1) * HEAD_DIM]
        parts.append(oh * lax.rsqrt(jnp.mean(oh * oh, axis=-1, keepdims=True) + NORM_EPS) * gw_ref[...])
    ga = jnp.concatenate(parts, axis=1) * _silu(za_ref[...])
    gb = om_ref[...] * _silu(zb_ref[...])
    cat = jnp.concatenate([ga, gb], axis=1).astype(BF16)
    hp = x_ref[...] + _dot(cat, wo_ref[...])
    y_ref[...] = hp * lax.rsqrt(jnp.mean(hp * hp, axis=-1, keepdims=True) + NORM_EPS) * nf_ref[...]


def _merge(og, za, om, zb, x2d, gdn_norm_w, w_out16, norm_f_w):
    m, d = x2d.shape
    tm = ROW_TILE
    assert m % tm == 0
    row = lambda i: (i, 0)
    full = lambda i: (0, 0)
    half = pl.BlockSpec((tm, GDN_WIDTH), row)
    return pl.pallas_call(
        _merge_kernel,
        grid=(m // tm,),
        in_specs=[half, half, half, half, pl.BlockSpec((tm, d), row),
                  pl.BlockSpec((1, HEAD_DIM), full), pl.BlockSpec(w_out16.shape, full),
                  pl.BlockSpec((1, d), full)],
        out_specs=pl.BlockSpec((tm, d), row),
        out_shape=jax.ShapeDtypeStruct((m, d), F32),
        compiler_params=pltpu.CompilerParams(dimension_semantics=("parallel",),
                                             vmem_limit_bytes=VMEM_LIMIT),
        name="merge_out",
    )(og, za, om, zb, x2d, gdn_norm_w, w_out16, norm_f_w)


def _rope_tables(pos):
    half = HEAD_DIM // 2
    inv_freq = ROPE_THETA ** (-jnp.arange(half, dtype=F32) / half)
    ang = pos.astype(F32)[:, None] * inv_freq[None, :]
    cos, sin = jnp.cos(ang), jnp.sin(ang)
    return jnp.concatenate([cos, cos], axis=1), jnp.concatenate([-sin, sin], axis=1)


def _relayout_w_in(w):
    o = 0
    parts = {}
    for name, n in (("qkv", GDN_CONV_CH), ("za", GDN_WIDTH), ("b", GDN_HEADS), ("a", GDN_HEADS),
                    ("qb", MOBA_WIDTH), ("kb", MOBA_WIDTH), ("vb", MOBA_WIDTH), ("zb", MOBA_WIDTH)):
        parts[name] = w[:, o:o + n]
        o += n
    pad = jnp.zeros((w.shape[0], LANES - 2 * GDN_HEADS), w.dtype)
    return jnp.concatenate([parts["qkv"], parts["za"], parts["qb"], parts["kb"], parts["vb"],
                            parts["zb"], parts["b"], parts["a"], pad], axis=1).astype(BF16)


def kernel(x_prompt, x_sample, cache_k, cache_v, state_gdn, state_conv, page_table, norm_in_w, w_in,
           conv_w, a_log, dt_bias, gdn_norm_w, w_out, norm_f_w):
    b, s, d = x_prompt.shape
    db, t_len, _ = x_sample.shape
    depth = w_in.shape[0]
    assert depth == 1
    n_pages = page_table.shape[1]
    past_len = n_pages * PAGE_SIZE
    assert past_len % MOBA_BLOCK == 0
    assert t_len <= PAGE_SIZE and t_len >= GDN_CONV_W - 1 and (db * t_len) % ROW_TILE == 0
    assert ROW_TILE % t_len == 0

    w_p = _relayout_w_in(w_in[0])
    w_o = w_out[0].astype(BF16)
    nw = norm_in_w[0][None, :]
    cw = conv_w[0]
    zeros4 = jnp.zeros((GDN_HEADS,), F32)
    lane_pad = jnp.zeros((LANES - 2 * GDN_HEADS,), F32)
    arow = jnp.stack([jnp.concatenate([zeros4, a_log[0], lane_pad]),
                      jnp.concatenate([zeros4, dt_bias[0], lane_pad])])
    acol = jnp.stack([jnp.concatenate([zeros4, a_log[0]]),
                      jnp.concatenate([zeros4, dt_bias[0]])])[:, :, None]
    gw = gdn_norm_w[0][None, :]
    nf = norm_f_w[None, :]

    cos_p, sin_p = _rope_tables(jnp.arange(s, dtype=jnp.int32))
    (qkv_p, za_p, q_p, k_p, v_p, zb_p, ba_p, kbf_p, vbf_p, kmean_p) = _inproj(
        x_prompt.reshape(b * s, d), nw, w_p, cos_p, sin_p, prompt=True)
    qkv_p3 = qkv_p.reshape(b, s, GDN_CONV_CH)
    ba_p3 = ba_p.reshape(b, s, LANES)
    bat_p = jnp.transpose(ba_p3[:, :, :SUBLANES], (0, 2, 1))
    n_m, intra, rhs, qdec, kdec, glast = _gdn_prep(qkv_p3, ba_p3, bat_p, cw, arow, acol)
    bh, nc, c, _ = n_m.shape
    t_m = _tri_inv(jnp.transpose(n_m.reshape(bh * nc, c, c), (1, 2, 0)))
    t_m = jnp.transpose(t_m, (2, 0, 1)).reshape(bh, nc, c, c)
    o_gdn_p, s_gdn_p = _gdn_scan(t_m, rhs, qdec, kdec, intra, glast, b)
    o_moba_p = _moba_prompt(q_p.reshape(b, s, MOBA_WIDTH), kbf_p.reshape(b, s, MOBA_WIDTH),
                            vbf_p.reshape(b, s, MOBA_WIDTH),
                            kmean_p.reshape(b, s // MOBA_BLOCK, MOBA_WIDTH))
    y_p = _merge(o_gdn_p.reshape(b * s, GDN_WIDTH), za_p, o_moba_p.reshape(b * s, MOBA_WIDTH), zb_p,
                 x_prompt.reshape(b * s, d), gw, w_o, nf)

    cos_s, sin_s = _rope_tables(past_len + jnp.arange(t_len, dtype=jnp.int32))
    reps = ROW_TILE // t_len
    (qkv_s, za_s, q_s, k_s, v_s, zb_s, ba_s) = _inproj(
        x_sample.reshape(db * t_len, d), nw, w_p, jnp.tile(cos_s, (reps, 1)), jnp.tile(sin_s, (reps, 1)),
        prompt=False)
    qkv_s3 = qkv_s.reshape(db, t_len, GDN_CONV_CH)
    tb = lambda a: jnp.transpose(a, (1, 0, 2))
    qg, kg, vg, beta_t, dec_t = _gdn_sample_prep(
        tb(qkv_s3), tb(state_conv[0]), tb(ba_s.reshape(db, t_len, LANES)), cw, arow)
    to_cols = lambda a: jnp.transpose(a.reshape(t_len, db, GDN_HEADS, HEAD_DIM), (1, 2, 3, 0))
    o_gdn_s, s_gdn_s = _gdn_sample_rec(state_gdn[0], to_cols(qg), to_cols(kg), tb(vg), tb(beta_t),
                                       tb(dec_t))

    page_flat = page_table.reshape(-1).astype(jnp.int32)
    ck = cache_k[0].reshape(cache_k.shape[1], PAGE_SIZE, MOBA_WIDTH)
    cv = cache_v[0].reshape(cache_v.shape[1], PAGE_SIZE, MOBA_WIDTH)
    kmean_s = _kmean_sample(page_flat, ck, db, n_pages)
    tok3 = lambda a: a.reshape(db, t_len, MOBA_WIDTH)
    o_moba_s = _moba_sample(page_flat, tok3(q_s), tok3(k_s), tok3(v_s), kmean_s, ck, cv, n_pages)
    y_s = _merge(o_gdn_s.reshape(db * t_len, GDN_WIDTH), za_s, o_moba_s.reshape(db * t_len, MOBA_WIDTH),
                 zb_s, x_sample.reshape(db * t_len, d), gw, w_o, nf)

    ctx = GDN_CONV_W - 1
    heads = lambda a, n, t: a.reshape(1, n, t, MOBA_HEADS, HEAD_DIM)
    return (y_p.reshape(b, s, d), y_s.reshape(db, t_len, d),
            heads(k_p, b, s), heads(v_p, b, s),
            s_gdn_p.reshape(1, b, GDN_HEADS, HEAD_DIM, HEAD_DIM),
            qkv_p3[:, s - ctx:, :][None],
            heads(k_s, db, t_len), heads(v_s, db, t_len),
            s_gdn_s[None],
            qkv_s3[:, t_len - ctx:, :][None])
```

```python
import functools
import math

import jax
import jax.numpy as jnp
from jax import lax
from jax.experimental import pallas as pl
from jax.experimental.pallas import tpu as pltpu

F32 = jnp.float32
BF16 = jnp.bfloat16

HEAD_DIM = 128
GDN_HEADS = 4
MOBA_HEADS = 4
GDN_QK = GDN_HEADS * HEAD_DIM
GDN_WIDTH = GDN_HEADS * HEAD_DIM
GDN_CONV_W = 4
GDN_CONV_CH = 2 * GDN_QK + GDN_WIDTH
GDN_CHUNK = 64
MOBA_WIDTH = MOBA_HEADS * HEAD_DIM
MOBA_BLOCK = 256
MOBA_TOPK = 3
PAGE_SIZE = 128
ROPE_THETA = 10000.0
NORM_EPS = 1e-6
LANES = 128
SUBLANES = 8
MASK_BIAS = -1e30

C_QKV = 0
C_ZA = GDN_CONV_CH
C_QB = C_ZA + GDN_WIDTH
C_KB = C_QB + MOBA_WIDTH
C_VB = C_KB + MOBA_WIDTH
C_ZB = C_VB + MOBA_WIDTH
C_BA = C_ZB + MOBA_WIDTH
IN_COLS_PAD = C_BA + LANES

ROW_TILE = 256
VMEM_LIMIT = 56 * 1024 * 1024


def _dot(a, b):
    return jnp.dot(a, b, preferred_element_type=F32)


def _dot_nt(a, b):
    return lax.dot_general(a, b, (((1,), (1,)), ((), ())), preferred_element_type=F32)


def _dot_tn(a, b):
    return lax.dot_general(a, b, (((0,), (0,)), ((), ())), preferred_element_type=F32)


def _dot_nt_f32(a, b):
    return lax.dot_general(a, b, (((1,), (1,)), ((), ())), preferred_element_type=F32,
                           precision=lax.Precision.HIGHEST)


def _dot_f32(a, b):
    return jnp.dot(a, b, preferred_element_type=F32, precision=lax.Precision.HIGHEST)


def _split_bf16(a):
    hi = a.astype(BF16)
    lo = (a - hi.astype(F32)).astype(BF16)
    return hi, lo


def _dot3(a, b):
    ah, al = _split_bf16(a)
    bh, bl = _split_bf16(b)
    return _dot(ah, bh) + (_dot(ah, bl) + _dot(al, bh))


def _silu(x):
    return x * jax.nn.sigmoid(x)


def _softplus(x):
    return jnp.maximum(x, 0.0) + jnp.log1p(jnp.exp(-jnp.abs(x)))


def _top3_members(gate, valid):
    lane = lax.broadcasted_iota(jnp.int32, gate.shape, 1).astype(F32)
    g = jnp.where(valid, gate, -jnp.inf)
    member = jnp.zeros(gate.shape, dtype=jnp.bool_)
    for _ in range(MOBA_TOPK):
        m = jnp.max(g, axis=1, keepdims=True)
        idx = jnp.min(jnp.where(g == m, lane, float(LANES)), axis=1, keepdims=True)
        pick = lane == idx
        member = jnp.logical_or(member, pick)
        g = jnp.where(pick, -jnp.inf, g)
    return jnp.logical_and(member, valid)


def _inproj_kernel(x_ref, nw_ref, w_ref, cos_ref, sin_ref,
                   qkv_ref, za_ref, q_ref, k_ref, v_ref, zb_ref, ba_ref, *prompt_refs):
    x = x_ref[...]
    ms = jnp.mean(x * x, axis=-1, keepdims=True)
    hb = (x * lax.rsqrt(ms + NORM_EPS) * nw_ref[...]).astype(BF16)

    def proj(c0, n):
        return _dot(hb, w_ref[:, c0:c0 + n])

    def rope(t):
        parts = []
        for h in range(MOBA_HEADS):
            th = t[:, h * HEAD_DIM:(h + 1) * HEAD_DIM]
            parts.append(th * cos_ref[...] + pltpu.roll(th, HEAD_DIM // 2, 1) * sin_ref[...])
        return jnp.concatenate(parts, axis=1)

    for j in range(GDN_CONV_CH // GDN_QK):
        qkv_ref[:, j * GDN_QK:(j + 1) * GDN_QK] = proj(C_QKV + j * GDN_QK, GDN_QK)
    za_ref[...] = proj(C_ZA, GDN_WIDTH)
    q_ref[...] = rope(proj(C_QB, MOBA_WIDTH))
    k = rope(proj(C_KB, MOBA_WIDTH))
    k_ref[...] = k
    v = proj(C_VB, MOBA_WIDTH)
    v_ref[...] = v
    zb_ref[...] = proj(C_ZB, MOBA_WIDTH)
    ba_ref[...] = proj(C_BA, LANES)
    if prompt_refs:
        kbf_ref, vbf_ref, kmean_ref = prompt_refs
        kbf_ref[...] = k.astype(BF16)
        vbf_ref[...] = v.astype(BF16)
        kmean_ref[0] = jnp.mean(k, axis=0, keepdims=True)


def _inproj(x2d, norm_w, w_p, cosf, sinf, *, prompt):
    m, d = x2d.shape
    tm = ROW_TILE
    assert m % tm == 0 and cosf.shape[0] % tm == 0
    n_pos = cosf.shape[0] // tm
    row = lambda i: (i, 0)
    full = lambda i: (0, 0)
    pos = (lambda i: (i % n_pos, 0)) if n_pos > 1 else full
    out_shape = [jax.ShapeDtypeStruct((m, GDN_CONV_CH), F32)]
    out_specs = [pl.BlockSpec((tm, GDN_CONV_CH), row)]
    for _ in range(5):
        out_shape.append(jax.ShapeDtypeStruct((m, MOBA_WIDTH), F32))
        out_specs.append(pl.BlockSpec((tm, MOBA_WIDTH), row))
    out_shape.append(jax.ShapeDtypeStruct((m, LANES), F32))
    out_specs.append(pl.BlockSpec((tm, LANES), row))
    if prompt:
        assert tm == MOBA_BLOCK
        for _ in range(2):
            out_shape.append(jax.ShapeDtypeStruct((m, MOBA_WIDTH), BF16))
            out_specs.append(pl.BlockSpec((tm, MOBA_WIDTH), row))
        out_shape.append(jax.ShapeDtypeStruct((m // tm, 1, MOBA_WIDTH), F32))
        out_specs.append(pl.BlockSpec((1, 1, MOBA_WIDTH), lambda i: (i, 0, 0)))
    return pl.pallas_call(
        _inproj_kernel,
        grid=(m // tm,),
        in_specs=[pl.BlockSpec((tm, d), row), pl.BlockSpec((1, d), full),
                  pl.BlockSpec((d, IN_COLS_PAD), full),
                  pl.BlockSpec((tm, HEAD_DIM), pos), pl.BlockSpec((tm, HEAD_DIM), pos)],
        out_specs=out_specs,
        out_shape=out_shape,
        compiler_params=pltpu.CompilerParams(dimension_semantics=("parallel",),
                                             vmem_limit_bytes=VMEM_LIMIT),
        name="inproj_prompt" if prompt else "inproj_sample",
    )(x2d, norm_w, w_p, cosf, sinf)


def _conv_silu_norm(c):
    c = _silu(c)
    qs, ks = [], []
    for h in range(GDN_HEADS):
        qh = c[:, h * HEAD_DIM:(h + 1) * HEAD_DIM]
        kh = c[:, GDN_QK + h * HEAD_DIM:GDN_QK + (h + 1) * HEAD_DIM]
        qs.append(qh * (lax.rsqrt(jnp.sum(qh * qh, axis=-1, keepdims=True) + NORM_EPS)
                        * (HEAD_DIM ** -0.5)))
        ks.append(kh * lax.rsqrt(jnp.sum(kh * kh, axis=-1, keepdims=True) + NORM_EPS))
    return qs, ks, c[:, 2 * GDN_QK:]


def _gdn_prep_kernel(x_ref, halo_ref, ba_ref, bat_ref, cw_ref, arow_ref, acol_ref,
                     n_ref, intra_ref, rhs_ref, qdec_ref, kdec_ref, glast_ref, buf_ref):
    i = pl.program_id(1)
    tt = x_ref.shape[1]
    c = GDN_CHUNK
    halo = jnp.where(i > 0, halo_ref[0], 0.0)
    buf_ref[0:SUBLANES, :] = halo
    buf_ref[SUBLANES:SUBLANES + tt, :] = x_ref[0]
    conv = buf_ref[SUBLANES - 3:SUBLANES - 3 + tt, :] * cw_ref[0:1, :]
    for w in range(1, GDN_CONV_W):
        conv = conv + buf_ref[SUBLANES - 3 + w:SUBLANES - 3 + w + tt, :] * cw_ref[w:w + 1, :]
    qs, ks, v = _conv_silu_norm(conv)

    ba = ba_ref[0]
    beta_full = jax.nn.sigmoid(ba)
    g_full = -jnp.exp(arow_ref[0:1, :]) * _softplus(ba + arow_ref[1:2, :])
    bat = bat_ref[0]
    gt_full = -jnp.exp(acol_ref[0]) * _softplus(bat + acol_ref[1])

    ri = lax.broadcasted_iota(jnp.int32, (c, c), 0)
    ci = lax.broadcasted_iota(jnp.int32, (c, c), 1)
    tril = ri >= ci
    strict = ri > ci
    ltri = jnp.where(tril, 1.0, 0.0).astype(F32)
    utri = jnp.where(ri <= ci, 1.0, 0.0).astype(F32)

    for cc in range(tt // c):
        rows = slice(cc * c, (cc + 1) * c)
        gcol_all = _dot_f32(ltri, g_full[rows, :])
        grow_all = _dot_f32(gt_full[:, rows], utri)
        for h in range(GDN_HEADS):
            gc_c = gcol_all[:, GDN_HEADS + h:GDN_HEADS + h + 1]
            gc_r = grow_all[GDN_HEADS + h:GDN_HEADS + h + 1, :]
            gamma = jnp.where(tril, jnp.exp(jnp.where(tril, gc_c - gc_r, 0.0)), 0.0)
            beta = beta_full[rows, h:h + 1]
            qh, kh = qs[h][rows, :], ks[h][rows, :]
            vh = v[rows, h * HEAD_DIM:(h + 1) * HEAD_DIM]
            kb = kh * beta
            k16 = kh.astype(BF16)
            n_ref[h, cc] = jnp.where(strict, _dot_nt(kb.astype(BF16), k16) * gamma, 0.0)
            intra_ref[h, cc] = jnp.where(tril, _dot_nt(qh.astype(BF16), k16) * gamma, 0.0)
            egc = jnp.exp(gc_c)
            rhs_ref[h, cc, :, 0:HEAD_DIM] = vh * beta
            rhs_ref[h, cc, :, HEAD_DIM:2 * HEAD_DIM] = kb * egc
            qdec_ref[h, cc] = qh * egc
            gl = gc_c[c - 1:c, :]
            kdec_ref[h, cc] = kh * jnp.exp(gl - gc_c)
            glast_ref[h, cc] = jnp.broadcast_to(jnp.exp(gl), (SUBLANES, LANES))


def _gdn_prep(qkv, ba, bat, conv_w, arow, acol):
    b, s, _ = qkv.shape
    tt = ROW_TILE
    c = GDN_CHUNK
    cpt = tt // c
    nc = s // c
    hb = tt // SUBLANES
    bh = b * GDN_HEADS

    def o(x):
        return (jax.ShapeDtypeStruct((bh, nc, c, x), F32),
                pl.BlockSpec((GDN_HEADS, cpt, c, x), lambda bi, i: (bi, i, 0, 0)))

    outs = [o(c), o(c), o(2 * HEAD_DIM), o(HEAD_DIM), o(HEAD_DIM)]
    outs.append((jax.ShapeDtypeStruct((bh, nc, SUBLANES, LANES), F32),
                 pl.BlockSpec((GDN_HEADS, cpt, SUBLANES, LANES), lambda bi, i: (bi, i, 0, 0))))
    return pl.pallas_call(
        _gdn_prep_kernel,
        grid=(b, s // tt),
        in_specs=[
            pl.BlockSpec((1, tt, GDN_CONV_CH), lambda bi, i: (bi, i, 0)),
            pl.BlockSpec((1, SUBLANES, GDN_CONV_CH), lambda bi, i: (bi, jnp.maximum(i * hb - 1, 0), 0)),
            pl.BlockSpec((1, tt, LANES), lambda bi, i: (bi, i, 0)),
            pl.BlockSpec((1, SUBLANES, tt), lambda bi, i: (bi, 0, i)),
            pl.BlockSpec((GDN_CONV_W, GDN_CONV_CH), lambda bi, i: (0, 0)),
            pl.BlockSpec((2, LANES), lambda bi, i: (0, 0)),
            pl.BlockSpec((2, SUBLANES, 1), lambda bi, i: (0, 0, 0)),
        ],
        out_specs=[x[1] for x in outs],
        out_shape=[x[0] for x in outs],
        scratch_shapes=[pltpu.VMEM((SUBLANES + tt, GDN_CONV_CH), F32)],
        compiler_params=pltpu.CompilerParams(dimension_semantics=("parallel", "parallel"),
                                             vmem_limit_bytes=VMEM_LIMIT),
        name="gdn_prep",
    )(qkv, qkv, ba, bat, conv_w, arow, acol)


def _tri_inv_kernel(n_ref, t_ref):
    c = n_ref.shape[0]
    col = lax.broadcasted_iota(jnp.int32, (c, LANES), 0)
    t_ref[...] = jnp.zeros(t_ref.shape, F32)

    def outer(i, carry):
        def inner(jg, acc):
            j0 = pl.multiple_of(jg * SUBLANES, SUBLANES)
            n_rows = n_ref[i, pl.ds(j0, SUBLANES), :]
            for r in range(SUBLANES):
                acc = acc - n_rows[r:r + 1, :] * t_ref[j0 + r]
            return acc
        groups = lax.shift_right_logical(i + (SUBLANES - 1), SUBLANES.bit_length() - 1)
        acc = lax.fori_loop(0, groups, inner, jnp.where(col == i, 1.0, 0.0).astype(F32))
        t_ref[i] = acc
        return carry

    lax.fori_loop(0, c, outer, 0)


def _tri_inv(nt):
    c, _, nmat = nt.shape
    assert nmat % LANES == 0
    spec = pl.BlockSpec((c, c, LANES), lambda g: (0, 0, g))
    return pl.pallas_call(
        _tri_inv_kernel,
        grid=(nmat // LANES,),
        in_specs=[spec],
        out_specs=spec,
        out_shape=jax.ShapeDtypeStruct(nt.shape, F32),
        compiler_params=pltpu.CompilerParams(dimension_semantics=("parallel",)),
        name="gdn_tri_inv",
    )(nt)


def _gdn_scan_kernel(t_ref, rhs_ref, qdec_ref, kdec_ref, intra_ref, glast_ref, o_ref, s_ref):
    i = pl.program_id(0)
    bh, ct, c, _ = t_ref.shape

    @pl.when(i == 0)
    def _():
        s_ref[...] = jnp.zeros(s_ref.shape, F32)

    for cc in range(ct):
        for n in range(bh):
            sol = _dot3(t_ref[n, cc], rhs_ref[n, cc])
            u, w = sol[:, :HEAD_DIM], sol[:, HEAD_DIM:]
            s = s_ref[n]
            s16 = s.astype(BF16)
            wq = jnp.concatenate([w, qdec_ref[n, cc]], axis=0).astype(BF16)
            wqs = _dot(wq, s16)
            v_new = u - wqs[:c]
            vn16 = v_new.astype(BF16)
            o = wqs[c:] + _dot(intra_ref[n, cc].astype(BF16), vn16)
            s_ref[n] = s * glast_ref[n, cc, 0:1, :] + _dot_tn(kdec_ref[n, cc].astype(BF16), vn16)
            b, h = n // GDN_HEADS, n % GDN_HEADS
            o_ref[b, cc * c:(cc + 1) * c, h * HEAD_DIM:(h + 1) * HEAD_DIM] = o


def _gdn_scan(t, rhs, qdec, kdec, intra, glast, b):
    bh, nc, c, _ = t.shape
    ct = 4
    assert nc % ct == 0

    def spec(x):
        return pl.BlockSpec((bh, ct, x.shape[2], x.shape[3]), lambda i: (0, i, 0, 0))

    return pl.pallas_call(
        _gdn_scan_kernel,
        grid=(nc // ct,),
        in_specs=[spec(t), spec(rhs), spec(qdec), spec(kdec), spec(intra), spec(glast)],
        out_specs=[pl.BlockSpec((b, ct * c, GDN_WIDTH), lambda i: (0, i, 0)),
                   pl.BlockSpec((bh, HEAD_DIM, HEAD_DIM), lambda i: (0, 0, 0))],
        out_shape=[jax.ShapeDtypeStruct((b, nc * c, GDN_WIDTH), F32),
                   jax.ShapeDtypeStruct((bh, HEAD_DIM, HEAD_DIM), F32)],
        compiler_params=pltpu.CompilerParams(dimension_semantics=("arbitrary",),
                                             vmem_limit_bytes=VMEM_LIMIT),
        name="gdn_scan",
    )(t, rhs, qdec, kdec, intra, glast)


MOBA_GROUP = 4


def _moba_prompt_kernel(q_ref, k_ref, v_ref, kmean_ref, o_ref):
    i = pl.program_id(2)
    tq = q_ref.shape[1]
    blk = MOBA_BLOCK
    scale = HEAD_DIM ** -0.5
    q = q_ref[0]
    q16 = q.astype(BF16)
    kmean = jnp.concatenate(
        [kmean_ref[0], jnp.zeros((LANES - kmean_ref.shape[1], HEAD_DIM), F32)], axis=0)
    past = lax.broadcasted_iota(jnp.int32, (tq, LANES), 1) < i
    member = _top3_members(_dot_nt_f32(q, kmean), past)
    q_aug = jnp.concatenate([q16, jnp.where(member, 0.0, MASK_BIAS).astype(BF16)], axis=1)
    grp = MOBA_GROUP * blk
    key_blk = lax.shift_right_logical(lax.broadcasted_iota(jnp.int32, (grp, LANES), 0),
                                      blk.bit_length() - 1)
    lane = lax.broadcasted_iota(jnp.int32, (grp, LANES), 1)

    start = pl.multiple_of(i * blk, blk)
    s = _dot_nt(q16, k_ref[0, pl.ds(start, blk), :]) * scale
    rq = lax.broadcasted_iota(jnp.int32, (tq, blk), 0)
    ck = lax.broadcasted_iota(jnp.int32, (tq, blk), 1)
    s = jnp.where(ck <= rq, s, -jnp.inf)
    m = jnp.max(s, axis=1, keepdims=True)
    p = jnp.exp(s - m)
    l = jnp.sum(p, axis=1, keepdims=True)
    acc = _dot(p.astype(BF16), v_ref[0, pl.ds(start, blk), :])

    def body(j, carry):
        m, l, acc = carry
        st = pl.multiple_of(j * grp, grp)
        onehot = jnp.where(lane == j * MOBA_GROUP + key_blk, 1.0, 0.0).astype(BF16)
        k_aug = jnp.concatenate([k_ref[0, pl.ds(st, grp), :], onehot], axis=1)
        s = _dot_nt(q_aug, k_aug) * scale
        m_new = jnp.maximum(m, jnp.max(s, axis=1, keepdims=True))
        alpha = jnp.exp(m - m_new)
        p = jnp.exp(s - m_new)
        l = l * alpha + jnp.sum(p, axis=1, keepdims=True)
        acc = acc * alpha + _dot(p.astype(BF16), v_ref[0, pl.ds(st, grp), :])
        return m_new, l, acc

    n_groups = lax.shift_right_logical(i + (MOBA_GROUP - 1), MOBA_GROUP.bit_length() - 1)
    m, l, acc = lax.fori_loop(0, n_groups, body, (m, l, acc))
    o_ref[0] = acc / l


def _moba_prompt(q, kbf, vbf, kmean):
    b, s, _ = q.shape
    tq = MOBA_BLOCK
    nb = s // MOBA_BLOCK
    assert s % MOBA_BLOCK == 0 and nb <= LANES and nb % MOBA_GROUP == 0
    return pl.pallas_call(
        _moba_prompt_kernel,
        grid=(b, MOBA_HEADS, s // tq),
        in_specs=[pl.BlockSpec((1, tq, HEAD_DIM), lambda bi, h, i: (bi, i, h)),
                  pl.BlockSpec((1, s, HEAD_DIM), lambda bi, h, i: (bi, 0, h)),
                  pl.BlockSpec((1, s, HEAD_DIM), lambda bi, h, i: (bi, 0, h)),
                  pl.BlockSpec((1, nb, HEAD_DIM), lambda bi, h, i: (bi, 0, h))],
        out_specs=pl.BlockSpec((1, tq, HEAD_DIM), lambda bi, h, i: (bi, i, h)),
        out_shape=jax.ShapeDtypeStruct((b, s, MOBA_WIDTH), F32),
        compiler_params=pltpu.CompilerParams(
            dimension_semantics=("parallel", "parallel", "arbitrary"),
            vmem_limit_bytes=VMEM_LIMIT),
        name="moba_prompt",
    )(q, kbf, vbf, kmean)


def _gdn_sample_prep_kernel(x_ref, st_ref, ba_ref, cw_ref, arow_ref,
                            q_ref, k_ref, v_ref, beta_ref, dec_ref):
    t_len = x_ref.shape[0]
    ctx = GDN_CONV_W - 1

    def src(tt):
        return st_ref[tt + ctx] if tt < 0 else x_ref[tt]

    for t in range(t_len):
        conv = src(t - ctx) * cw_ref[0:1, :]
        for w in range(1, GDN_CONV_W):
            conv = conv + src(t - ctx + w) * cw_ref[w:w + 1, :]
        qs, ks, v = _conv_silu_norm(conv)
        q_ref[t] = jnp.concatenate(qs, axis=1)
        k_ref[t] = jnp.concatenate(ks, axis=1)
        v_ref[t] = v
        ba = ba_ref[t]
        beta_full = jax.nn.sigmoid(ba)
        dec_full = jnp.exp(-jnp.exp(arow_ref[0:1, :]) * _softplus(ba + arow_ref[1:2, :]))
        rows = ba.shape[0]
        beta_ref[t] = jnp.concatenate(
            [jnp.broadcast_to(beta_full[:, h:h + 1], (rows, HEAD_DIM)) for h in range(GDN_HEADS)], axis=1)
        dec_ref[t] = jnp.concatenate(
            [jnp.broadcast_to(dec_full[:, GDN_HEADS + h:GDN_HEADS + h + 1], (rows, HEAD_DIM))
             for h in range(GDN_HEADS)], axis=1)


def _gdn_sample_prep(x_t, st_t, ba_t, conv_w, arow):
    t_len, db, _ = x_t.shape
    bt = min(db, 32)
    assert db % bt == 0

    def spec(t, w):
        return pl.BlockSpec((t, bt, w), lambda i: (0, i, 0))

    out = jax.ShapeDtypeStruct((t_len, db, GDN_WIDTH), F32)
    return pl.pallas_call(
        _gdn_sample_prep_kernel,
        grid=(db // bt,),
        in_specs=[spec(t_len, GDN_CONV_CH), spec(GDN_CONV_W - 1, GDN_CONV_CH), spec(t_len, LANES),
                  pl.BlockSpec((GDN_CONV_W, GDN_CONV_CH), lambda i: (0, 0)),
                  pl.BlockSpec((2, LANES), lambda i: (0, 0))],
        out_specs=[spec(t_len, GDN_WIDTH)] * 5,
        out_shape=[out] * 5,
        compiler_params=pltpu.CompilerParams(dimension_semantics=("parallel",),
                                             vmem_limit_bytes=VMEM_LIMIT),
        name="gdn_sample_prep",
    )(x_t, st_t, ba_t, conv_w, arow)


def _gdn_sample_rec_kernel(s0_ref, qt_ref, kt_ref, v_ref, beta_ref, dec_ref, o_ref, s_ref):
    bt = s0_ref.shape[0]
    t_len = v_ref.shape[1]

    def per_batch(bi, carry):
        v, beta, dec = v_ref[bi], beta_ref[bi], dec_ref[bi]
        outs = []
        for h in range(GDN_HEADS):
            cols = slice(h * HEAD_DIM, (h + 1) * HEAD_DIM)
            s = s0_ref[bi, h]
            kt = kt_ref[bi, h]
            qt = qt_ref[bi, h]
            rows = []
            for t in range(t_len):
                s = s * dec[t:t + 1, cols]
                kcol = kt[:, t:t + 1]
                kv = jnp.sum(kcol * s, axis=0, keepdims=True)
                upd = (v[t:t + 1, cols] - kv) * beta[t:t + 1, cols]
                s = s + kcol * upd
                rows.append(jnp.sum(qt[:, t:t + 1] * s, axis=0, keepdims=True))
            s_ref[bi, h] = s
            outs.append(jnp.concatenate(rows, axis=0))
        o_ref[bi] = jnp.concatenate(outs, axis=1)
        return carry

    lax.fori_loop(0, bt, per_batch, 0)


def _gdn_sample_rec(s0, qt, kt, v, beta, dec):
    db, t_len, _ = v.shape
    bt = min(db, 8)
    assert db % bt == 0
    st_spec = pl.BlockSpec((bt, GDN_HEADS, HEAD_DIM, HEAD_DIM), lambda i: (i, 0, 0, 0))
    tr_spec = pl.BlockSpec((bt, GDN_HEADS, HEAD_DIM, t_len), lambda i: (i, 0, 0, 0))
    tok_spec = pl.BlockSpec((bt, t_len, GDN_WIDTH), lambda i: (i, 0, 0))
    return pl.pallas_call(
        _gdn_sample_rec_kernel,
        grid=(db // bt,),
        in_specs=[st_spec, tr_spec, tr_spec, tok_spec, tok_spec, tok_spec],
        out_specs=[tok_spec, st_spec],
        out_shape=[jax.ShapeDtypeStruct((db, t_len, GDN_WIDTH), F32),
                   jax.ShapeDtypeStruct(s0.shape, F32)],
        compiler_params=pltpu.CompilerParams(dimension_semantics=("parallel",),
                                             vmem_limit_bytes=VMEM_LIMIT),
        name="gdn_sample_rec",
    )(s0, qt, kt, v, beta, dec)


PAGES_PER_STEP = 16
PAGE_ROWS = PAGE_SIZE * MOBA_HEADS
PAGES_PER_BLOCK = MOBA_BLOCK // PAGE_SIZE
BLOCKS_PER_STEP = PAGES_PER_STEP // PAGES_PER_BLOCK
STEP_ROWS = PAGES_PER_STEP * PAGE_ROWS
HEAD_SHIFT = MOBA_HEADS.bit_length() - 1
assert 1 << HEAD_SHIFT == MOBA_HEADS and SUBLANES == 2 * MOBA_HEADS and PAGES_PER_BLOCK == 2


def _page_specs(n_pages):
    def mk(slot):
        return pl.BlockSpec((1, PAGE_ROWS, HEAD_DIM),
                            lambda b, g, pt: (pt[b * n_pages + g * PAGES_PER_STEP + slot], 0, 0))
    return [mk(s) for s in range(PAGES_PER_STEP)]


def _kmean_sample_kernel(pt_ref, *refs):
    k_refs, o_ref = refs[:PAGES_PER_STEP], refs[PAGES_PER_STEP]
    low = lax.broadcasted_iota(jnp.int32, (SUBLANES, HEAD_DIM), 0) < MOBA_HEADS

    def block_mean(n):
        tot = None
        for r in range(PAGES_PER_BLOCK):
            page = k_refs[n * PAGES_PER_BLOCK + r][0]
            part = jnp.sum(page.reshape(PAGE_ROWS // SUBLANES, SUBLANES, HEAD_DIM), axis=0)
            tot = part if tot is None else tot + part
        return (tot + pltpu.roll(tot, MOBA_HEADS, 0)) * (1.0 / MOBA_BLOCK)

    tiles = [jnp.where(low, block_mean(2 * n), block_mean(2 * n + 1))
             for n in range(BLOCKS_PER_STEP // 2)]
    o_ref[0] = jnp.concatenate(tiles, axis=0)


def _kmean_sample(page_flat, cache_k, db, n_pages):
    nb = n_pages // PAGES_PER_BLOCK
    assert n_pages % PAGES_PER_STEP == 0
    return pl.pallas_call(
        _kmean_sample_kernel,
        grid_spec=pltpu.PrefetchScalarGridSpec(
            num_scalar_prefetch=1,
            grid=(db, n_pages // PAGES_PER_STEP),
            in_specs=_page_specs(n_pages),
            out_specs=pl.BlockSpec((1, BLOCKS_PER_STEP * MOBA_HEADS, HEAD_DIM),
                                   lambda b, g, pt: (b, g, 0)),
        ),
        out_shape=jax.ShapeDtypeStruct((db, nb * MOBA_HEADS, HEAD_DIM), F32),
        compiler_params=pltpu.CompilerParams(dimension_semantics=("parallel", "arbitrary"),
                                             vmem_limit_bytes=VMEM_LIMIT),
        name="moba_sample_kmean",
    )(page_flat, *([cache_k] * PAGES_PER_STEP))


def _moba_sample_kernel(pt_ref, q_ref, kn_ref, vn_ref, kmean_ref, *refs):
    k_refs = refs[:PAGES_PER_STEP]
    v_refs = refs[PAGES_PER_STEP:2 * PAGES_PER_STEP]
    o_ref, qaug_ref, kaug_ref, vbuf_ref, m_ref, l_ref, acc_ref = refs[2 * PAGES_PER_STEP:]
    b, g = pl.program_id(0), pl.program_id(1)
    rows = q_ref.shape[1]
    t_len = rows // MOBA_HEADS
    n_steps = qaug_ref.shape[0]
    scale = HEAD_DIM ** -0.5
    t_shift = t_len.bit_length() - 1
    assert 1 << t_shift == t_len and kn_ref.shape[1] == rows
    step_lanes = BLOCKS_PER_STEP * MOBA_HEADS
    row_i = lax.broadcasted_iota(jnp.int32, (rows, LANES), 0)
    lane_i = lax.broadcasted_iota(jnp.int32, (rows, LANES), 1)
    row_head = lax.shift_right_logical(row_i, t_shift)
    lane_head = jnp.bitwise_and(lane_i, MOBA_HEADS - 1)

    @pl.when(jnp.logical_and(b == 0, g == 0))
    def _():
        pr = lax.broadcasted_iota(jnp.int32, (PAGE_ROWS, LANES), 0)
        pl_ = lax.broadcasted_iota(jnp.int32, (PAGE_ROWS, LANES), 1)
        for slot in range(PAGES_PER_STEP):
            tgt = (slot // PAGES_PER_BLOCK) * MOBA_HEADS + jnp.bitwise_and(pr, MOBA_HEADS - 1)
            kaug_ref[slot * PAGE_ROWS:(slot + 1) * PAGE_ROWS, HEAD_DIM:] = (
                jnp.where(pl_ == tgt, 1.0, 0.0).astype(BF16))

    @pl.when(g == 0)
    def _():
        q = q_ref[0]
        q16 = q.astype(BF16)
        n_cand = kmean_ref.shape[1]
        kmean = kmean_ref[0]
        if n_cand < LANES:
            kmean = jnp.concatenate([kmean, jnp.zeros((LANES - n_cand, HEAD_DIM), F32)], axis=0)
        valid = jnp.logical_and(lane_head == row_head, lane_i < n_cand)
        member = _top3_members(_dot_nt_f32(q, kmean), valid)
        bias = jnp.where(member, 0.0, MASK_BIAS)
        for st in range(n_steps):
            shifted = bias if st == 0 else pltpu.roll(bias, LANES - st * step_lanes, 1)
            qaug_ref[st] = jnp.concatenate(
                [q16, jnp.where(lane_i < step_lanes, shifted, 0.0).astype(BF16)], axis=1)
        pad = jnp.zeros((LANES - rows, HEAD_DIM), F32)
        kn = jnp.concatenate([kn_ref[0], pad], axis=0).astype(BF16)
        vn = jnp.concatenate([vn_ref[0], pad], axis=0).astype(BF16)
        s = _dot_nt(q16, kn) * scale
        ok = jnp.logical_and(lane_head == row_head, lane_i < rows)
        ok = jnp.logical_and(ok, lax.shift_right_logical(lane_i, HEAD_SHIFT)
                             <= jnp.bitwise_and(row_i, t_len - 1))
        s = jnp.where(ok, s, -jnp.inf)
        m = jnp.max(s, axis=1, keepdims=True)
        p = jnp.exp(s - m)
        m_ref[...] = jnp.broadcast_to(m, m_ref.shape)
        l_ref[...] = jnp.broadcast_to(jnp.sum(p, axis=1, keepdims=True), l_ref.shape)
        acc_ref[...] = _dot(p.astype(BF16), vn)

    for slot in range(PAGES_PER_STEP):
        rs = slice(slot * PAGE_ROWS, (slot + 1) * PAGE_ROWS)
        kaug_ref[rs, 0:HEAD_DIM] = k_refs[slot][0].astype(BF16)
        vbuf_ref[rs, :] = v_refs[slot][0].astype(BF16)
    m = m_ref[:, 0:1]
    s = _dot_nt(qaug_ref[g], kaug_ref[...]) * scale
    m_new = jnp.maximum(m, jnp.max(s, axis=1, keepdims=True))
    alpha = jnp.exp(m - m_new)
    p = jnp.exp(s - m_new)
    l = l_ref[:, 0:1] * alpha + jnp.sum(p, axis=1, keepdims=True)
    acc = acc_ref[...] * alpha + _dot(p.astype(BF16), vbuf_ref[...])
    m_ref[...] = jnp.broadcast_to(m_new, m_ref.shape)
    l_ref[...] = jnp.broadcast_to(l, l_ref.shape)
    acc_ref[...] = acc

    @pl.when(g == pl.num_programs(1) - 1)
    def _():
        o_ref[0] = acc / l


def _moba_sample(page_flat, q2, kn2, vn2, kmean, cache_k, cache_v, n_pages):
    db, rows, _ = q2.shape
    n_cand = kmean.shape[1]
    n_steps = n_pages // PAGES_PER_STEP
    assert rows <= LANES and n_cand <= LANES
    tok = pl.BlockSpec((1, rows, HEAD_DIM), lambda b, g, pt: (b, 0, 0))
    return pl.pallas_call(
        _moba_sample_kernel,
        grid_spec=pltpu.PrefetchScalarGridSpec(
            num_scalar_prefetch=1,
            grid=(db, n_steps),
            in_specs=[tok, tok, tok, pl.BlockSpec((1, n_cand, HEAD_DIM), lambda b, g, pt: (b, 0, 0))]
            + _page_specs(n_pages) + _page_specs(n_pages),
            out_specs=tok,
            scratch_shapes=[pltpu.VMEM((n_steps, rows, 2 * HEAD_DIM), BF16),
                            pltpu.VMEM((STEP_ROWS, 2 * HEAD_DIM), BF16),
                            pltpu.VMEM((STEP_ROWS, HEAD_DIM), BF16),
                            pltpu.VMEM((rows, LANES), F32), pltpu.VMEM((rows, LANES), F32),
                            pltpu.VMEM((rows, HEAD_DIM), F32)],
        ),
        out_shape=jax.ShapeDtypeStruct((db, rows, HEAD_DIM), F32),
        compiler_params=pltpu.CompilerParams(dimension_semantics=("arbitrary", "arbitrary"),
                                             vmem_limit_bytes=VMEM_LIMIT),
        name="moba_sample_attn",
    )(page_flat, q2, kn2, vn2, kmean, *([cache_k] * PAGES_PER_STEP), *([cache_v] * PAGES_PER_STEP))


def _merge_kernel(og_ref, za_ref, om_ref, zb_ref, x_ref, gw_ref, wo_ref, nf_ref, y_ref):
    og = og_ref[...]
    parts = []
    for h in range(GDN_HEADS):
        oh = og[:, h * HEAD_DIM:(h + 1) * HEAD_DIM]
        parts.append(oh * lax.rsqrt(jnp.mean(oh * oh, axis=-1, keepdims=True) + NORM_EPS) * gw_ref[...])
    ga = jnp.concatenate(parts, axis=1) * _silu(za_ref[...])
    gb = om_ref[...] * _silu(zb_ref[...])
    cat = jnp.concatenate([ga, gb], axis=1).astype(BF16)
    hp = x_ref[...] + _dot(cat, wo_ref[...])
    y_ref[...] = hp * lax.rsqrt(jnp.mean(hp * hp, axis=-1, keepdims=True) + NORM_EPS) * nf_ref[...]


def _merge(og, za, om, zb, x2d, gdn_norm_w, w_out16, norm_f_w):
    m, d = x2d.shape
    tm = ROW_TILE
    assert m % tm == 0
    row = lambda i: (i, 0)
    full = lambda i: (0, 0)
    half = pl.BlockSpec((tm, GDN_WIDTH), row)
    return pl.pallas_call(
        _merge_kernel,
        grid=(m // tm,),
        in_specs=[half, half, half, half, pl.BlockSpec((tm, d), row),
                  pl.BlockSpec((1, HEAD_DIM), full), pl.BlockSpec(w_out16.shape, full),
                  pl.BlockSpec((1, d), full)],
        out_specs=pl.BlockSpec((tm, d), row),
        out_shape=jax.ShapeDtypeStruct((m, d), F32),
        compiler_params=pltpu.CompilerParams(dimension_semantics=("parallel",),
                                             vmem_limit_bytes=VMEM_LIMIT),
        name="merge_out",
    )(og, za, om, zb, x2d, gdn_norm_w, w_out16, norm_f_w)


def _rope_tables(pos):
    half = HEAD_DIM // 2
    inv_freq = ROPE_THETA ** (-jnp.arange(half, dtype=F32) / half)
    ang = pos.astype(F32)[:, None] * inv_freq[None, :]
    cos, sin = jnp.cos(ang), jnp.sin(ang)
    return jnp.concatenate([cos, cos], axis=1), jnp.concatenate([-sin, sin], axis=1)


def _relayout_w_in(w):
    o = 0
    parts = {}
    for name, n in (("qkv", GDN_CONV_CH), ("za", GDN_WIDTH), ("b", GDN_HEADS), ("a", GDN_HEADS),
                    ("qb", MOBA_WIDTH), ("kb", MOBA_WIDTH), ("vb", MOBA_WIDTH), ("zb", MOBA_WIDTH)):
        parts[name] = w[:, o:o + n]
        o += n
    pad = jnp.zeros((w.shape[0], LANES - 2 * GDN_HEADS), w.dtype)
    return jnp.concatenate([parts["qkv"], parts["za"], parts["qb"], parts["kb"], parts["vb"],
                            parts["zb"], parts["b"], parts["a"], pad], axis=1).astype(BF16)


def kernel(x_prompt, x_sample, cache_k, cache_v, state_gdn, state_conv, page_table, norm_in_w, w_in,
           conv_w, a_log, dt_bias, gdn_norm_w, w_out, norm_f_w):
    b, s, d = x_prompt.shape
    db, t_len, _ = x_sample.shape
    depth = w_in.shape[0]
    assert depth == 1
    n_pages = page_table.shape[1]
    past_len = n_pages * PAGE_SIZE
    assert past_len % MOBA_BLOCK == 0
    assert t_len <= PAGE_SIZE and t_len >= GDN_CONV_W - 1 and (db * t_len) % ROW_TILE == 0
    assert ROW_TILE % t_len == 0

    w_p = _relayout_w_in(w_in[0])
    w_o = w_out[0].astype(BF16)
    nw = norm_in_w[0][None, :]
    cw = conv_w[0]
    zeros4 = jnp.zeros((GDN_HEADS,), F32)
    lane_pad = jnp.zeros((LANES - 2 * GDN_HEADS,), F32)
    arow = jnp.stack([jnp.concatenate([zeros4, a_log[0], lane_pad]),
                      jnp.concatenate([zeros4, dt_bias[0], lane_pad])])
    acol = jnp.stack([jnp.concatenate([zeros4, a_log[0]]),
                      jnp.concatenate([zeros4, dt_bias[0]])])[:, :, None]
    gw = gdn_norm_w[0][None, :]
    nf = norm_f_w[None, :]

    cos_p, sin_p = _rope_tables(jnp.arange(s, dtype=jnp.int32))
    (qkv_p, za_p, q_p, k_p, v_p, zb_p, ba_p, kbf_p, vbf_p, kmean_p) = _inproj(
        x_prompt.reshape(b * s, d), nw, w_p, cos_p, sin_p, prompt=True)
    qkv_p3 = qkv_p.reshape(b, s, GDN_CONV_CH)
    ba_p3 = ba_p.reshape(b, s, LANES)
    bat_p = jnp.transpose(ba_p3[:, :, :SUBLANES], (0, 2, 1))
    n_m, intra, rhs, qdec, kdec, glast = _gdn_prep(qkv_p3, ba_p3, bat_p, cw, arow, acol)
    bh, nc, c, _ = n_m.shape
    t_m = _tri_inv(jnp.transpose(n_m.reshape(bh * nc, c, c), (1, 2, 0)))
    t_m = jnp.transpose(t_m, (2, 0, 1)).reshape(bh, nc, c, c)
    o_gdn_p, s_gdn_p = _gdn_scan(t_m, rhs, qdec, kdec, intra, glast, b)
    o_moba_p = _moba_prompt(q_p.reshape(b, s, MOBA_WIDTH), kbf_p.reshape(b, s, MOBA_WIDTH),
                            vbf_p.reshape(b, s, MOBA_WIDTH),
                            kmean_p.reshape(b, s // MOBA_BLOCK, MOBA_WIDTH))
    y_p = _merge(o_gdn_p.reshape(b * s, GDN_WIDTH), za_p, o_moba_p.reshape(b * s, MOBA_WIDTH), zb_p,
                 x_prompt.reshape(b * s, d), gw, w_o, nf)

    cos_s, sin_s = _rope_tables(past_len + jnp.arange(t_len, dtype=jnp.int32))
    reps = ROW_TILE // t_len
    (qkv_s, za_s, q_s, k_s, v_s, zb_s, ba_s) = _inproj(
        x_sample.reshape(db * t_len, d), nw, w_p, jnp.tile(cos_s, (reps, 1)), jnp.tile(sin_s, (reps, 1)),
        prompt=False)
    qkv_s3 = qkv_s.reshape(db, t_len, GDN_CONV_CH)
    tb = lambda a: jnp.transpose(a, (1, 0, 2))
    qg, kg, vg, beta_t, dec_t = _gdn_sample_prep(
        tb(qkv_s3), tb(state_conv[0]), tb(ba_s.reshape(db, t_len, LANES)), cw, arow)
    to_cols = lambda a: jnp.transpose(a.reshape(t_len, db, GDN_HEADS, HEAD_DIM), (1, 2, 3, 0))
    o_gdn_s, s_gdn_s = _gdn_sample_rec(state_gdn[0], to_cols(qg), to_cols(kg), tb(vg), tb(beta_t),
                                       tb(dec_t))

    page_flat = page_table.reshape(-1).astype(jnp.int32)
    ck = cache_k.reshape(cache_k.shape[1], PAGE_ROWS, HEAD_DIM)
    cv = cache_v.reshape(cache_v.shape[1], PAGE_ROWS, HEAD_DIM)
    kmean_s = _kmean_sample(page_flat, ck, db, n_pages)
    rows_s = t_len * MOBA_HEADS
    q2 = jnp.transpose(q_s.reshape(db, t_len, MOBA_HEADS, HEAD_DIM), (0, 2, 1, 3))
    o2 = _moba_sample(page_flat, q2.reshape(db, rows_s, HEAD_DIM), k_s.reshape(db, rows_s, HEAD_DIM),
                      v_s.reshape(db, rows_s, HEAD_DIM), kmean_s, ck, cv, n_pages)
    o_moba_s = jnp.transpose(o2.reshape(db, MOBA_HEADS, t_len, HEAD_DIM), (0, 2, 1, 3))
    y_s = _merge(o_gdn_s.reshape(db * t_len, GDN_WIDTH), za_s, o_moba_s.reshape(db * t_len, MOBA_WIDTH),
                 zb_s, x_sample.reshape(db * t_len, d), gw, w_o, nf)

    ctx = GDN_CONV_W - 1
    heads = lambda a, n, t: a.reshape(1, n, t, MOBA_HEADS, HEAD_DIM)
    return (y_p.reshape(b, s, d), y_s.reshape(db, t_len, d),
            heads(k_p, b, s), heads(v_p, b, s),
            s_gdn_p.reshape(1, b, GDN_HEADS, HEAD_DIM, HEAD_DIM),
            qkv_p3[:, s - ctx:, :][None],
            heads(k_s, db, t_len), heads(v_s, db, t_len),
            s_gdn_s[None],
            qkv_s3[:, t_len - ctx:, :][None])
```

```python
import functools
import math

import jax
import jax.numpy as jnp
from jax import lax
from jax.experimental import pallas as pl
from jax.experimental.pallas import tpu as pltpu

F32 = jnp.float32
BF16 = jnp.bfloat16

HEAD_DIM = 128
GDN_HEADS = 4
MOBA_HEADS = 4
GDN_QK = GDN_HEADS * HEAD_DIM
GDN_WIDTH = GDN_HEADS * HEAD_DIM
GDN_CONV_W = 4
GDN_CONV_CH = 2 * GDN_QK + GDN_WIDTH
GDN_CHUNK = 64
MOBA_WIDTH = MOBA_HEADS * HEAD_DIM
MOBA_BLOCK = 256
MOBA_TOPK = 3
PAGE_SIZE = 128
ROPE_THETA = 10000.0
NORM_EPS = 1e-6
LANES = 128
SUBLANES = 8
MASK_BIAS = -1e30

C_QKV = 0
C_ZA = GDN_CONV_CH
C_QB = C_ZA + GDN_WIDTH
C_KB = C_QB + MOBA_WIDTH
C_VB = C_KB + MOBA_WIDTH
C_ZB = C_VB + MOBA_WIDTH
C_BA = C_ZB + MOBA_WIDTH
IN_COLS_PAD = C_BA + LANES

ROW_TILE = 256
VMEM_LIMIT = 56 * 1024 * 1024


def _dot(a, b):
    return jnp.dot(a, b, preferred_element_type=F32)


def _dot_nt(a, b):
    return lax.dot_general(a, b, (((1,), (1,)), ((), ())), preferred_element_type=F32)


def _dot_tn(a, b):
    return lax.dot_general(a, b, (((0,), (0,)), ((), ())), preferred_element_type=F32)


def _dot_nt_f32(a, b):
    return lax.dot_general(a, b, (((1,), (1,)), ((), ())), preferred_element_type=F32,
                           precision=lax.Precision.HIGHEST)


def _dot_f32(a, b):
    return jnp.dot(a, b, preferred_element_type=F32, precision=lax.Precision.HIGHEST)


def _split_bf16(a):
    hi = a.astype(BF16)
    lo = (a - hi.astype(F32)).astype(BF16)
    return hi, lo


def _dot3(a, b):
    ah, al = _split_bf16(a)
    bh, bl = _split_bf16(b)
    return _dot(ah, bh) + (_dot(ah, bl) + _dot(al, bh))


def _silu(x):
    return x * jax.nn.sigmoid(x)


def _softplus(x):
    return jnp.maximum(x, 0.0) + jnp.log1p(jnp.exp(-jnp.abs(x)))


def _top3_members(gate, valid):
    lane = lax.broadcasted_iota(jnp.int32, gate.shape, 1).astype(F32)
    g = jnp.where(valid, gate, -jnp.inf)
    member = jnp.zeros(gate.shape, dtype=jnp.bool_)
    for _ in range(MOBA_TOPK):
        m = jnp.max(g, axis=1, keepdims=True)
        idx = jnp.min(jnp.where(g == m, lane, float(LANES)), axis=1, keepdims=True)
        pick = lane == idx
        member = jnp.logical_or(member, pick)
        g = jnp.where(pick, -jnp.inf, g)
    return jnp.logical_and(member, valid)


def _inproj_kernel(x_ref, nw_ref, w_ref, cos_ref, sin_ref,
                   qkv_ref, za_ref, q_ref, k_ref, v_ref, zb_ref, ba_ref, *prompt_refs):
    x = x_ref[...]
    ms = jnp.mean(x * x, axis=-1, keepdims=True)
    hb = (x * lax.rsqrt(ms + NORM_EPS) * nw_ref[...]).astype(BF16)

    def proj(c0, n):
        return _dot(hb, w_ref[:, c0:c0 + n])

    def rope(t):
        parts = []
        for h in range(MOBA_HEADS):
            th = t[:, h * HEAD_DIM:(h + 1) * HEAD_DIM]
            parts.append(th * cos_ref[...] + pltpu.roll(th, HEAD_DIM // 2, 1) * sin_ref[...])
        return jnp.concatenate(parts, axis=1)

    for j in range(GDN_CONV_CH // GDN_QK):
        qkv_ref[:, j * GDN_QK:(j + 1) * GDN_QK] = proj(C_QKV + j * GDN_QK, GDN_QK)
    za_ref[...] = proj(C_ZA, GDN_WIDTH)
    q_ref[...] = rope(proj(C_QB, MOBA_WIDTH))
    k = rope(proj(C_KB, MOBA_WIDTH))
    k_ref[...] = k
    v = proj(C_VB, MOBA_WIDTH)
    v_ref[...] = v
    zb_ref[...] = proj(C_ZB, MOBA_WIDTH)
    ba_ref[...] = proj(C_BA, LANES)
    if prompt_refs:
        kbf_ref, vbf_ref, kmean_ref = prompt_refs
        kbf_ref[...] = k.astype(BF16)
        vbf_ref[...] = v.astype(BF16)
        kmean_ref[0] = jnp.mean(k, axis=0, keepdims=True)


def _inproj(x2d, norm_w, w_p, cosf, sinf, *, prompt):
    m, d = x2d.shape
    tm = ROW_TILE
    assert m % tm == 0 and cosf.shape[0] % tm == 0
    n_pos = cosf.shape[0] // tm
    row = lambda i: (i, 0)
    full = lambda i: (0, 0)
    pos = (lambda i: (i % n_pos, 0)) if n_pos > 1 else full
    out_shape = [jax.ShapeDtypeStruct((m, GDN_CONV_CH), F32)]
    out_specs = [pl.BlockSpec((tm, GDN_CONV_CH), row)]
    for _ in range(5):
        out_shape.append(jax.ShapeDtypeStruct((m, MOBA_WIDTH), F32))
        out_specs.append(pl.BlockSpec((tm, MOBA_WIDTH), row))
    out_shape.append(jax.ShapeDtypeStruct((m, LANES), F32))
    out_specs.append(pl.BlockSpec((tm, LANES), row))
    if prompt:
        assert tm == MOBA_BLOCK
        for _ in range(2):
            out_shape.append(jax.ShapeDtypeStruct((m, MOBA_WIDTH), BF16))
            out_specs.append(pl.BlockSpec((tm, MOBA_WIDTH), row))
        out_shape.append(jax.ShapeDtypeStruct((m // tm, 1, MOBA_WIDTH), F32))
        out_specs.append(pl.BlockSpec((1, 1, MOBA_WIDTH), lambda i: (i, 0, 0)))
    return pl.pallas_call(
        _inproj_kernel,
        grid=(m // tm,),
        in_specs=[pl.BlockSpec((tm, d), row), pl.BlockSpec((1, d), full),
                  pl.BlockSpec((d, IN_COLS_PAD), full),
                  pl.BlockSpec((tm, HEAD_DIM), pos), pl.BlockSpec((tm, HEAD_DIM), pos)],
        out_specs=out_specs,
        out_shape=out_shape,
        compiler_params=pltpu.CompilerParams(dimension_semantics=("parallel",),
                                             vmem_limit_bytes=VMEM_LIMIT),
        name="inproj_prompt" if prompt else "inproj_sample",
    )(x2d, norm_w, w_p, cosf, sinf)


def _conv_silu_norm(c):
    c = _silu(c)
    qs, ks = [], []
    for h in range(GDN_HEADS):
        qh = c[:, h * HEAD_DIM:(h + 1) * HEAD_DIM]
        kh = c[:, GDN_QK + h * HEAD_DIM:GDN_QK + (h + 1) * HEAD_DIM]
        qs.append(qh * (lax.rsqrt(jnp.sum(qh * qh, axis=-1, keepdims=True) + NORM_EPS)
                        * (HEAD_DIM ** -0.5)))
        ks.append(kh * lax.rsqrt(jnp.sum(kh * kh, axis=-1, keepdims=True) + NORM_EPS))
    return qs, ks, c[:, 2 * GDN_QK:]


def _gdn_prep_kernel(x_ref, halo_ref, ba_ref, bat_ref, cw_ref, arow_ref, acol_ref,
                     n_ref, intra_ref, rhs_ref, qdec_ref, kdec_ref, glast_ref, buf_ref):
    i = pl.program_id(1)
    tt = x_ref.shape[1]
    c = GDN_CHUNK
    halo = jnp.where(i > 0, halo_ref[0], 0.0)
    buf_ref[0:SUBLANES, :] = halo
    buf_ref[SUBLANES:SUBLANES + tt, :] = x_ref[0]
    conv = buf_ref[SUBLANES - 3:SUBLANES - 3 + tt, :] * cw_ref[0:1, :]
    for w in range(1, GDN_CONV_W):
        conv = conv + buf_ref[SUBLANES - 3 + w:SUBLANES - 3 + w + tt, :] * cw_ref[w:w + 1, :]
    qs, ks, v = _conv_silu_norm(conv)

    ba = ba_ref[0]
    beta_full = jax.nn.sigmoid(ba)
    g_full = -jnp.exp(arow_ref[0:1, :]) * _softplus(ba + arow_ref[1:2, :])
    bat = bat_ref[0]
    gt_full = -jnp.exp(acol_ref[0]) * _softplus(bat + acol_ref[1])

    ri = lax.broadcasted_iota(jnp.int32, (c, c), 0)
    ci = lax.broadcasted_iota(jnp.int32, (c, c), 1)
    tril = ri >= ci
    strict = ri > ci
    ltri = jnp.where(tril, 1.0, 0.0).astype(F32)
    utri = jnp.where(ri <= ci, 1.0, 0.0).astype(F32)

    for cc in range(tt // c):
        rows = slice(cc * c, (cc + 1) * c)
        gcol_all = _dot_f32(ltri, g_full[rows, :])
        grow_all = _dot_f32(gt_full[:, rows], utri)
        for h in range(GDN_HEADS):
            gc_c = gcol_all[:, GDN_HEADS + h:GDN_HEADS + h + 1]
            gc_r = grow_all[GDN_HEADS + h:GDN_HEADS + h + 1, :]
            gamma = jnp.where(tril, jnp.exp(jnp.where(tril, gc_c - gc_r, 0.0)), 0.0)
            beta = beta_full[rows, h:h + 1]
            qh, kh = qs[h][rows, :], ks[h][rows, :]
            vh = v[rows, h * HEAD_DIM:(h + 1) * HEAD_DIM]
            kb = kh * beta
            k16 = kh.astype(BF16)
            n_ref[h, cc] = jnp.where(strict, _dot_nt(kb.astype(BF16), k16) * gamma, 0.0)
            intra_ref[h, cc] = jnp.where(tril, _dot_nt(qh.astype(BF16), k16) * gamma, 0.0)
            egc = jnp.exp(gc_c)
            rhs_ref[h, cc, :, 0:HEAD_DIM] = vh * beta
            rhs_ref[h, cc, :, HEAD_DIM:2 * HEAD_DIM] = kb * egc
            qdec_ref[h, cc] = qh * egc
            gl = gc_c[c - 1:c, :]
            kdec_ref[h, cc] = kh * jnp.exp(gl - gc_c)
            glast_ref[h, cc] = jnp.broadcast_to(jnp.exp(gl), (SUBLANES, LANES))


def _gdn_prep(qkv, ba, bat, conv_w, arow, acol):
    b, s, _ = qkv.shape
    tt = ROW_TILE
    c = GDN_CHUNK
    cpt = tt // c
    nc = s // c
    hb = tt // SUBLANES
    bh = b * GDN_HEADS

    def o(x):
        return (jax.ShapeDtypeStruct((bh, nc, c, x), F32),
                pl.BlockSpec((GDN_HEADS, cpt, c, x), lambda bi, i: (bi, i, 0, 0)))

    outs = [o(c), o(c), o(2 * HEAD_DIM), o(HEAD_DIM), o(HEAD_DIM)]
    outs.append((jax.ShapeDtypeStruct((bh, nc, SUBLANES, LANES), F32),
                 pl.BlockSpec((GDN_HEADS, cpt, SUBLANES, LANES), lambda bi, i: (bi, i, 0, 0))))
    return pl.pallas_call(
        _gdn_prep_kernel,
        grid=(b, s // tt),
        in_specs=[
            pl.BlockSpec((1, tt, GDN_CONV_CH), lambda bi, i: (bi, i, 0)),
            pl.BlockSpec((1, SUBLANES, GDN_CONV_CH), lambda bi, i: (bi, jnp.maximum(i * hb - 1, 0), 0)),
            pl.BlockSpec((1, tt, LANES), lambda bi, i: (bi, i, 0)),
            pl.BlockSpec((1, SUBLANES, tt), lambda bi, i: (bi, 0, i)),
            pl.BlockSpec((GDN_CONV_W, GDN_CONV_CH), lambda bi, i: (0, 0)),
            pl.BlockSpec((2, LANES), lambda bi, i: (0, 0)),
            pl.BlockSpec((2, SUBLANES, 1), lambda bi, i: (0, 0, 0)),
        ],
        out_specs=[x[1] for x in outs],
        out_shape=[x[0] for x in outs],
        scratch_shapes=[pltpu.VMEM((SUBLANES + tt, GDN_CONV_CH), F32)],
        compiler_params=pltpu.CompilerParams(dimension_semantics=("parallel", "parallel"),
                                             vmem_limit_bytes=VMEM_LIMIT),
        name="gdn_prep",
    )(qkv, qkv, ba, bat, conv_w, arow, acol)


def _tri_inv_kernel(n_ref, t_ref):
    c = n_ref.shape[0]
    col = lax.broadcasted_iota(jnp.int32, (c, LANES), 0)
    t_ref[...] = jnp.zeros(t_ref.shape, F32)

    def outer(i, carry):
        def inner(jg, acc):
            j0 = pl.multiple_of(jg * SUBLANES, SUBLANES)
            n_rows = n_ref[i, pl.ds(j0, SUBLANES), :]
            for r in range(SUBLANES):
                acc = acc - n_rows[r:r + 1, :] * t_ref[j0 + r]
            return acc
        groups = lax.shift_right_logical(i + (SUBLANES - 1), SUBLANES.bit_length() - 1)
        acc = lax.fori_loop(0, groups, inner, jnp.where(col == i, 1.0, 0.0).astype(F32))
        t_ref[i] = acc
        return carry

    lax.fori_loop(0, c, outer, 0)


def _tri_inv(nt):
    c, _, nmat = nt.shape
    assert nmat % LANES == 0
    spec = pl.BlockSpec((c, c, LANES), lambda g: (0, 0, g))
    return pl.pallas_call(
        _tri_inv_kernel,
        grid=(nmat // LANES,),
        in_specs=[spec],
        out_specs=spec,
        out_shape=jax.ShapeDtypeStruct(nt.shape, F32),
        compiler_params=pltpu.CompilerParams(dimension_semantics=("parallel",)),
        name="gdn_tri_inv",
    )(nt)


def _gdn_scan_kernel(t_ref, rhs_ref, qdec_ref, kdec_ref, intra_ref, glast_ref, o_ref, s_ref):
    i = pl.program_id(0)
    bh, ct, c, _ = t_ref.shape

    @pl.when(i == 0)
    def _():
        s_ref[...] = jnp.zeros(s_ref.shape, F32)

    for cc in range(ct):
        for n in range(bh):
            sol = _dot3(t_ref[n, cc], rhs_ref[n, cc])
            u, w = sol[:, :HEAD_DIM], sol[:, HEAD_DIM:]
            s = s_ref[n]
            s16 = s.astype(BF16)
            wq = jnp.concatenate([w, qdec_ref[n, cc]], axis=0).astype(BF16)
            wqs = _dot(wq, s16)
            v_new = u - wqs[:c]
            vn16 = v_new.astype(BF16)
            o = wqs[c:] + _dot(intra_ref[n, cc].astype(BF16), vn16)
            s_ref[n] = s * glast_ref[n, cc, 0:1, :] + _dot_tn(kdec_ref[n, cc].astype(BF16), vn16)
            b, h = n // GDN_HEADS, n % GDN_HEADS
            o_ref[b, cc * c:(cc + 1) * c, h * HEAD_DIM:(h + 1) * HEAD_DIM] = o


def _gdn_scan(t, rhs, qdec, kdec, intra, glast, b):
    bh, nc, c, _ = t.shape
    ct = 4
    assert nc % ct == 0

    def spec(x):
        return pl.BlockSpec((bh, ct, x.shape[2], x.shape[3]), lambda i: (0, i, 0, 0))

    return pl.pallas_call(
        _gdn_scan_kernel,
        grid=(nc // ct,),
        in_specs=[spec(t), spec(rhs), spec(qdec), spec(kdec), spec(intra), spec(glast)],
        out_specs=[pl.BlockSpec((b, ct * c, GDN_WIDTH), lambda i: (0, i, 0)),
                   pl.BlockSpec((bh, HEAD_DIM, HEAD_DIM), lambda i: (0, 0, 0))],
        out_shape=[jax.ShapeDtypeStruct((b, nc * c, GDN_WIDTH), F32),
                   jax.ShapeDtypeStruct((bh, HEAD_DIM, HEAD_DIM), F32)],
        compiler_params=pltpu.CompilerParams(dimension_semantics=("arbitrary",),
                                             vmem_limit_bytes=VMEM_LIMIT),
        name="gdn_scan",
    )(t, rhs, qdec, kdec, intra, glast)


MOBA_GROUP = 4
MOBA_SPLIT = 2


def _moba_prompt_kernel(q_ref, k_ref, v_ref, kmean_ref, o_ref):
    i = pl.program_id(2)
    tq = q_ref.shape[1]
    blk = MOBA_BLOCK
    scale = HEAD_DIM ** -0.5
    q = q_ref[0]
    q16 = q.astype(BF16)
    kmean = jnp.concatenate(
        [kmean_ref[0], jnp.zeros((LANES - kmean_ref.shape[1], HEAD_DIM), F32)], axis=0)
    past = lax.broadcasted_iota(jnp.int32, (tq, LANES), 1) < i
    member = _top3_members(_dot_nt_f32(q, kmean), past)
    q_aug = jnp.concatenate([q16, jnp.where(member, 0.0, MASK_BIAS).astype(BF16)], axis=1)
    grp = MOBA_GROUP * blk
    c_exp = scale * math.log2(math.e)
    tile_rows = 2 * SUBLANES
    tile_lane = lax.broadcasted_iota(jnp.int32, (tile_rows, LANES), 1)

    start = pl.multiple_of(i * blk, blk)
    s = _dot_nt(q16, k_ref[0, pl.ds(start, blk), :])
    rq = lax.broadcasted_iota(jnp.int32, (tq, blk), 0)
    ck = lax.broadcasted_iota(jnp.int32, (tq, blk), 1)
    s = jnp.where(ck <= rq, s, -jnp.inf)
    m = jnp.max(s, axis=1, keepdims=True)
    p = jnp.exp2((s - m) * c_exp)
    l = jnp.sum(p, axis=1, keepdims=True)
    acc = _dot(p.astype(BF16), v_ref[0, pl.ds(start, blk), :])

    sub = grp // MOBA_SPLIT

    def body(j, carry):
        m, l, acc = carry
        st = pl.multiple_of(j * grp, grp)
        onehot = []
        for n in range(MOBA_GROUP):
            tile = jnp.where(tile_lane == j * MOBA_GROUP + n, 1.0, 0.0).astype(BF16)
            onehot += [tile] * (blk // tile_rows)
        k_aug = jnp.concatenate([k_ref[0, pl.ds(st, grp), :], jnp.concatenate(onehot, axis=0)], axis=1)
        raw = [_dot_nt(q_aug, k_aug[u * sub:(u + 1) * sub, :]) for u in range(MOBA_SPLIT)]
        for u in range(MOBA_SPLIT):
            m_new = jnp.maximum(m, jnp.max(raw[u], axis=1, keepdims=True))
            alpha = jnp.exp2((m - m_new) * c_exp)
            p = jnp.exp2((raw[u] - m_new) * c_exp)
            l = l * alpha + jnp.sum(p, axis=1, keepdims=True)
            acc = acc * alpha + _dot(p.astype(BF16), v_ref[0, pl.ds(st + u * sub, sub), :])
            m = m_new
        return m, l, acc

    n_groups = lax.shift_right_logical(i + (MOBA_GROUP - 1), MOBA_GROUP.bit_length() - 1)
    m, l, acc = lax.fori_loop(0, n_groups, body, (m, l, acc))
    o_ref[0] = acc / l


def _moba_prompt(q, kbf, vbf, kmean):
    b, s, _ = q.shape
    tq = MOBA_BLOCK
    nb = s // MOBA_BLOCK
    assert s % MOBA_BLOCK == 0 and nb <= LANES and nb % MOBA_GROUP == 0
    return pl.pallas_call(
        _moba_prompt_kernel,
        grid=(b, MOBA_HEADS, s // tq),
        in_specs=[pl.BlockSpec((1, tq, HEAD_DIM), lambda bi, h, i: (bi, i, h)),
                  pl.BlockSpec((1, s, HEAD_DIM), lambda bi, h, i: (bi, 0, h)),
                  pl.BlockSpec((1, s, HEAD_DIM), lambda bi, h, i: (bi, 0, h)),
                  pl.BlockSpec((1, nb, HEAD_DIM), lambda bi, h, i: (bi, 0, h))],
        out_specs=pl.BlockSpec((1, tq, HEAD_DIM), lambda bi, h, i: (bi, i, h)),
        out_shape=jax.ShapeDtypeStruct((b, s, MOBA_WIDTH), F32),
        compiler_params=pltpu.CompilerParams(
            dimension_semantics=("parallel", "parallel", "arbitrary"),
            vmem_limit_bytes=VMEM_LIMIT),
        name="moba_prompt",
    )(q, kbf, vbf, kmean)


def _gdn_sample_prep_kernel(x_ref, st_ref, ba_ref, cw_ref, arow_ref,
                            q_ref, k_ref, v_ref, beta_ref, dec_ref):
    t_len = x_ref.shape[0]
    ctx = GDN_CONV_W - 1

    def src(tt):
        return st_ref[tt + ctx] if tt < 0 else x_ref[tt]

    for t in range(t_len):
        conv = src(t - ctx) * cw_ref[0:1, :]
        for w in range(1, GDN_CONV_W):
            conv = conv + src(t - ctx + w) * cw_ref[w:w + 1, :]
        qs, ks, v = _conv_silu_norm(conv)
        q_ref[t] = jnp.concatenate(qs, axis=1)
        k_ref[t] = jnp.concatenate(ks, axis=1)
        v_ref[t] = v
        ba = ba_ref[t]
        beta_full = jax.nn.sigmoid(ba)
        dec_full = jnp.exp(-jnp.exp(arow_ref[0:1, :]) * _softplus(ba + arow_ref[1:2, :]))
        rows = ba.shape[0]
        beta_ref[t] = jnp.concatenate(
            [jnp.broadcast_to(beta_full[:, h:h + 1], (rows, HEAD_DIM)) for h in range(GDN_HEADS)], axis=1)
        dec_ref[t] = jnp.concatenate(
            [jnp.broadcast_to(dec_full[:, GDN_HEADS + h:GDN_HEADS + h + 1], (rows, HEAD_DIM))
             for h in range(GDN_HEADS)], axis=1)


def _gdn_sample_prep(x_t, st_t, ba_t, conv_w, arow):
    t_len, db, _ = x_t.shape
    bt = min(db, 32)
    assert db % bt == 0

    def spec(t, w):
        return pl.BlockSpec((t, bt, w), lambda i: (0, i, 0))

    out = jax.ShapeDtypeStruct((t_len, db, GDN_WIDTH), F32)
    return pl.pallas_call(
        _gdn_sample_prep_kernel,
        grid=(db // bt,),
        in_specs=[spec(t_len, GDN_CONV_CH), spec(GDN_CONV_W - 1, GDN_CONV_CH), spec(t_len, LANES),
                  pl.BlockSpec((GDN_CONV_W, GDN_CONV_CH), lambda i: (0, 0)),
                  pl.BlockSpec((2, LANES), lambda i: (0, 0))],
        out_specs=[spec(t_len, GDN_WIDTH)] * 5,
        out_shape=[out] * 5,
        compiler_params=pltpu.CompilerParams(dimension_semantics=("parallel",),
                                             vmem_limit_bytes=VMEM_LIMIT),
        name="gdn_sample_prep",
    )(x_t, st_t, ba_t, conv_w, arow)


def _gdn_sample_rec_kernel(s0_ref, qt_ref, kt_ref, v_ref, beta_ref, dec_ref, o_ref, s_ref):
    bt = s0_ref.shape[0]
    t_len = v_ref.shape[1]

    def per_batch(bi, carry):
        v, beta, dec = v_ref[bi], beta_ref[bi], dec_ref[bi]
        outs = []
        for h in range(GDN_HEADS):
            cols = slice(h * HEAD_DIM, (h + 1) * HEAD_DIM)
            s = s0_ref[bi, h]
            kt = kt_ref[bi, h]
            qt = qt_ref[bi, h]
            rows = []
            for t in range(t_len):
                s = s * dec[t:t + 1, cols]
                kcol = kt[:, t:t + 1]
                kv = jnp.sum(kcol * s, axis=0, keepdims=True)
                upd = (v[t:t + 1, cols] - kv) * beta[t:t + 1, cols]
                s = s + kcol * upd
                rows.append(jnp.sum(qt[:, t:t + 1] * s, axis=0, keepdims=True))
            s_ref[bi, h] = s
            outs.append(jnp.concatenate(rows, axis=0))
        o_ref[bi] = jnp.concatenate(outs, axis=1)
        return carry

    lax.fori_loop(0, bt, per_batch, 0)


def _gdn_sample_rec(s0, qt, kt, v, beta, dec):
    db, t_len, _ = v.shape
    bt = min(db, 8)
    assert db % bt == 0
    st_spec = pl.BlockSpec((bt, GDN_HEADS, HEAD_DIM, HEAD_DIM), lambda i: (i, 0, 0, 0))
    tr_spec = pl.BlockSpec((bt, GDN_HEADS, HEAD_DIM, t_len), lambda i: (i, 0, 0, 0))
    tok_spec = pl.BlockSpec((bt, t_len, GDN_WIDTH), lambda i: (i, 0, 0))
    return pl.pallas_call(
        _gdn_sample_rec_kernel,
        grid=(db // bt,),
        in_specs=[st_spec, tr_spec, tr_spec, tok_spec, tok_spec, tok_spec],
        out_specs=[tok_spec, st_spec],
        out_shape=[jax.ShapeDtypeStruct((db, t_len, GDN_WIDTH), F32),
                   jax.ShapeDtypeStruct(s0.shape, F32)],
        compiler_params=pltpu.CompilerParams(dimension_semantics=("parallel",),
                                             vmem_limit_bytes=VMEM_LIMIT),
        name="gdn_sample_rec",
    )(s0, qt, kt, v, beta, dec)


PAGES_PER_STEP = 16
PAGE_ROWS = PAGE_SIZE * MOBA_HEADS
PAGES_PER_BLOCK = MOBA_BLOCK // PAGE_SIZE
BLOCKS_PER_STEP = PAGES_PER_STEP // PAGES_PER_BLOCK
STEP_ROWS = PAGES_PER_STEP * PAGE_ROWS
HEAD_SHIFT = MOBA_HEADS.bit_length() - 1
assert 1 << HEAD_SHIFT == MOBA_HEADS and SUBLANES == 2 * MOBA_HEADS and PAGES_PER_BLOCK == 2


def _page_specs(n_pages, batch_of):
    def mk(slot):
        return pl.BlockSpec(
            (1, PAGE_ROWS, HEAD_DIM),
            lambda bp, g, pt: (pt[batch_of(bp) * n_pages + g * PAGES_PER_STEP + slot], 0, 0))
    return [mk(s) for s in range(PAGES_PER_STEP)]


def _moba_sample_kernel(pt_ref, q_ref, kn_ref, vn_ref, *refs):
    k_refs = refs[:PAGES_PER_STEP]
    v_refs = refs[PAGES_PER_STEP:2 * PAGES_PER_STEP]
    o_ref, kbuf_ref, vbuf_ref, kmean_ref, mem_ref, m_ref, l_ref, acc_ref = refs[2 * PAGES_PER_STEP:]
    bp, g = pl.program_id(0), pl.program_id(1)
    n_batch = pl.num_programs(0) - 1
    n_steps = pl.num_programs(1)
    rows = q_ref.shape[1]
    t_len = rows // MOBA_HEADS
    scale = HEAD_DIM ** -0.5
    t_shift = t_len.bit_length() - 1
    assert 1 << t_shift == t_len and kn_ref.shape[1] == rows
    step_lanes = BLOCKS_PER_STEP * MOBA_HEADS
    par = jnp.bitwise_and(bp, 1)
    row_i = lax.broadcasted_iota(jnp.int32, (rows, LANES), 0)
    lane_i = lax.broadcasted_iota(jnp.int32, (rows, LANES), 1)
    row_head = lax.shift_right_logical(row_i, t_shift)
    lane_head = jnp.bitwise_and(lane_i, MOBA_HEADS - 1)

    @pl.when(jnp.logical_and(bp == 0, g == 0))
    def _():
        kmean_ref[...] = jnp.zeros(kmean_ref.shape, F32)

    @pl.when(jnp.logical_and(bp > 0, g == 0))
    def _():
        q = q_ref[0]
        q16 = q.astype(BF16)
        n_cand = n_steps * step_lanes
        valid = jnp.logical_and(lane_head == row_head, lane_i < n_cand)
        member = _top3_members(_dot_nt_f32(q, kmean_ref[...]), valid)
        mem_ref[...] = jnp.where(member, 1.0, 0.0)
        pad = jnp.zeros((LANES - rows, HEAD_DIM), F32)
        kn = jnp.concatenate([kn_ref[0], pad], axis=0).astype(BF16)
        vn = jnp.concatenate([vn_ref[0], pad], axis=0).astype(BF16)
        s = _dot_nt(q16, kn) * scale
        ok = jnp.logical_and(lane_head == row_head, lane_i < rows)
        ok = jnp.logical_and(ok, lax.shift_right_logical(lane_i, HEAD_SHIFT)
                             <= jnp.bitwise_and(row_i, t_len - 1))
        s = jnp.where(ok, s, -jnp.inf)
        m = jnp.max(s, axis=1, keepdims=True)
        p = jnp.exp(s - m)
        m_ref[...] = jnp.broadcast_to(m, m_ref.shape)
        l_ref[...] = jnp.broadcast_to(jnp.sum(p, axis=1, keepdims=True), l_ref.shape)
        acc_ref[...] = _dot(p.astype(BF16), vn)

    @pl.when(bp > 0)
    def _():
        for slot in range(PAGES_PER_STEP):
            vbuf_ref[slot * PAGE_ROWS:(slot + 1) * PAGE_ROWS, :] = v_refs[slot][0].astype(BF16)
        member = mem_ref[...]
        blk_rows = PAGES_PER_BLOCK * PAGE_ROWS
        col_head = jnp.bitwise_and(lax.broadcasted_iota(jnp.int32, (rows, blk_rows), 1), MOBA_HEADS - 1)
        own_head = col_head == lax.shift_right_logical(
            lax.broadcasted_iota(jnp.int32, (rows, blk_rows), 0), t_shift)
        cand_blk = lax.shift_right_logical(lane_i, HEAD_SHIFT)
        bias = []
        for n in range(BLOCKS_PER_STEP):
            picked = jnp.sum(jnp.where(cand_blk == g * BLOCKS_PER_STEP + n, member, 0.0),
                             axis=1, keepdims=True) > 0.0
            bias.append(jnp.where(jnp.logical_and(own_head, picked), 0.0, -jnp.inf))
        s = _dot_nt(q_ref[0].astype(BF16), kbuf_ref[1 - par, g]) * scale + jnp.concatenate(bias, axis=1)
        m = m_ref[:, 0:1]
        m_new = jnp.maximum(m, jnp.max(s, axis=1, keepdims=True))
        alpha = jnp.exp(m - m_new)
        p = jnp.exp(s - m_new)
        l = l_ref[:, 0:1] * alpha + jnp.sum(p, axis=1, keepdims=True)
        acc = acc_ref[...] * alpha + _dot(p.astype(BF16), vbuf_ref[...])
        m_ref[...] = jnp.broadcast_to(m_new, m_ref.shape)
        l_ref[...] = jnp.broadcast_to(l, l_ref.shape)
        acc_ref[...] = acc

        @pl.when(g == n_steps - 1)
        def _():
            o_ref[0] = acc / l

    @pl.when(bp < n_batch)
    def _():
        low = lax.broadcasted_iota(jnp.int32, (SUBLANES, HEAD_DIM), 0) < MOBA_HEADS

        def block_mean(n):
            tot = None
            for r in range(PAGES_PER_BLOCK):
                slot = n * PAGES_PER_BLOCK + r
                page = k_refs[slot][0]
                kbuf_ref[par, g, slot * PAGE_ROWS:(slot + 1) * PAGE_ROWS, :] = page.astype(BF16)
                part = jnp.sum(page.reshape(PAGE_ROWS // SUBLANES, SUBLANES, HEAD_DIM), axis=0)
                tot = part if tot is None else tot + part
            return (tot + pltpu.roll(tot, MOBA_HEADS, 0)) * (1.0 / MOBA_BLOCK)

        tiles = [jnp.where(low, block_mean(2 * n), block_mean(2 * n + 1))
                 for n in range(BLOCKS_PER_STEP // 2)]
        kmean_ref[pl.ds(pl.multiple_of(g * step_lanes, step_lanes), step_lanes), :] = (
            jnp.concatenate(tiles, axis=0))


def _moba_sample(page_flat, q2, kn2, vn2, cache_k, cache_v, n_pages):
    db, rows, _ = q2.shape
    assert n_pages % PAGES_PER_STEP == 0
    n_steps = n_pages // PAGES_PER_STEP
    n_cand = n_steps * BLOCKS_PER_STEP * MOBA_HEADS
    assert rows <= LANES and n_cand <= LANES
    stage1 = lambda bp: jnp.minimum(bp, db - 1)
    stage2 = lambda bp: jnp.maximum(bp - 1, 0)
    tok = pl.BlockSpec((1, rows, HEAD_DIM), lambda bp, g, pt: (stage2(bp), 0, 0))
    return pl.pallas_call(
        _moba_sample_kernel,
        grid_spec=pltpu.PrefetchScalarGridSpec(
            num_scalar_prefetch=1,
            grid=(db + 1, n_steps),
            in_specs=[tok, tok, tok] + _page_specs(n_pages, stage1) + _page_specs(n_pages, stage2),
            out_specs=tok,
            scratch_shapes=[pltpu.VMEM((2, n_steps, STEP_ROWS, HEAD_DIM), BF16),
                            pltpu.VMEM((STEP_ROWS, HEAD_DIM), BF16),
                            pltpu.VMEM((LANES, HEAD_DIM), F32),
                            pltpu.VMEM((rows, LANES), F32),
                            pltpu.VMEM((rows, LANES), F32), pltpu.VMEM((rows, LANES), F32),
                            pltpu.VMEM((rows, HEAD_DIM), F32)],
        ),
        out_shape=jax.ShapeDtypeStruct((db, rows, HEAD_DIM), F32),
        compiler_params=pltpu.CompilerParams(dimension_semantics=("arbitrary", "arbitrary"),
                                             vmem_limit_bytes=VMEM_LIMIT),
        name="moba_sample",
    )(page_flat, q2, kn2, vn2, *([cache_k] * PAGES_PER_STEP), *([cache_v] * PAGES_PER_STEP))


def _merge_kernel(og_ref, za_ref, om_ref, zb_ref, x_ref, gw_ref, wo_ref, nf_ref, y_ref):
    og = og_ref[...]
    parts = []
    for h in range(GDN_HEADS):
        oh = og[:, h * HEAD_DIM:(h + 1) * HEAD_DIM]
        parts.append(oh * lax.rsqrt(jnp.mean(oh * oh, axis=-1, keepdims=True) + NORM_EPS) * gw_ref[...])
    ga = jnp.concatenate(parts, axis=1) * _silu(za_ref[...])
    gb = om_ref[...] * _silu(zb_ref[...])
    cat = jnp.concatenate([ga, gb], axis=1).astype(BF16)
    hp = x_ref[...] + _dot(cat, wo_ref[...])
    y_ref[...] = hp * lax.rsqrt(jnp.mean(hp * hp, axis=-1, keepdims=True) + NORM_EPS) * nf_ref[...]


def _merge(og, za, om, zb, x2d, gdn_norm_w, w_out16, norm_f_w):
    m, d = x2d.shape
    tm = ROW_TILE
    assert m % tm == 0
    row = lambda i: (i, 0)
    full = lambda i: (0, 0)
    half = pl.BlockSpec((tm, GDN_WIDTH), row)
    return pl.pallas_call(
        _merge_kernel,
        grid=(m // tm,),
        in_specs=[half, half, half, half, pl.BlockSpec((tm, d), row),
                  pl.BlockSpec((1, HEAD_DIM), full), pl.BlockSpec(w_out16.shape, full),
                  pl.BlockSpec((1, d), full)],
        out_specs=pl.BlockSpec((tm, d), row),
        out_shape=jax.ShapeDtypeStruct((m, d), F32),
        compiler_params=pltpu.CompilerParams(dimension_semantics=("parallel",),
                                             vmem_limit_bytes=VMEM_LIMIT),
        name="merge_out",
    )(og, za, om, zb, x2d, gdn_norm_w, w_out16, norm_f_w)


def _rope_tables(pos):
    half = HEAD_DIM // 2
    inv_freq = ROPE_THETA ** (-jnp.arange(half, dtype=F32) / half)
    ang = pos.astype(F32)[:, None] * inv_freq[None, :]
    cos, sin = jnp.cos(ang), jnp.sin(ang)
    return jnp.concatenate([cos, cos], axis=1), jnp.concatenate([-sin, sin], axis=1)


def _relayout_w_in(w):
    o = 0
    parts = {}
    for name, n in (("qkv", GDN_CONV_CH), ("za", GDN_WIDTH), ("b", GDN_HEADS), ("a", GDN_HEADS),
                    ("qb", MOBA_WIDTH), ("kb", MOBA_WIDTH), ("vb", MOBA_WIDTH), ("zb", MOBA_WIDTH)):
        parts[name] = w[:, o:o + n]
        o += n
    pad = jnp.zeros((w.shape[0], LANES - 2 * GDN_HEADS), w.dtype)
    return jnp.concatenate([parts["qkv"], parts["za"], parts["qb"], parts["kb"], parts["vb"],
                            parts["zb"], parts["b"], parts["a"], pad], axis=1).astype(BF16)


def kernel(x_prompt, x_sample, cache_k, cache_v, state_gdn, state_conv, page_table, norm_in_w, w_in,
           conv_w, a_log, dt_bias, gdn_norm_w, w_out, norm_f_w):
    b, s, d = x_prompt.shape
    db, t_len, _ = x_sample.shape
    depth = w_in.shape[0]
    assert depth == 1
    n_pages = page_table.shape[1]
    past_len = n_pages * PAGE_SIZE
    assert past_len % MOBA_BLOCK == 0
    assert t_len <= PAGE_SIZE and t_len >= GDN_CONV_W - 1 and (db * t_len) % ROW_TILE == 0
    assert ROW_TILE % t_len == 0

    w_p = _relayout_w_in(w_in[0])
    w_o = w_out[0].astype(BF16)
    nw = norm_in_w[0][None, :]
    cw = conv_w[0]
    zeros4 = jnp.zeros((GDN_HEADS,), F32)
    lane_pad = jnp.zeros((LANES - 2 * GDN_HEADS,), F32)
    arow = jnp.stack([jnp.concatenate([zeros4, a_log[0], lane_pad]),
                      jnp.concatenate([zeros4, dt_bias[0], lane_pad])])
    acol = jnp.stack([jnp.concatenate([zeros4, a_log[0]]),
                      jnp.concatenate([zeros4, dt_bias[0]])])[:, :, None]
    gw = gdn_norm_w[0][None, :]
    nf = norm_f_w[None, :]

    cos_p, sin_p = _rope_tables(jnp.arange(s, dtype=jnp.int32))
    (qkv_p, za_p, q_p, k_p, v_p, zb_p, ba_p, kbf_p, vbf_p, kmean_p) = _inproj(
        x_prompt.reshape(b * s, d), nw, w_p, cos_p, sin_p, prompt=True)
    qkv_p3 = qkv_p.reshape(b, s, GDN_CONV_CH)
    ba_p3 = ba_p.reshape(b, s, LANES)
    bat_p = jnp.transpose(ba_p3[:, :, :SUBLANES], (0, 2, 1))
    n_m, intra, rhs, qdec, kdec, glast = _gdn_prep(qkv_p3, ba_p3, bat_p, cw, arow, acol)
    bh, nc, c, _ = n_m.shape
    t_m = _tri_inv(jnp.transpose(n_m.reshape(bh * nc, c, c), (1, 2, 0)))
    t_m = jnp.transpose(t_m, (2, 0, 1)).reshape(bh, nc, c, c)
    o_gdn_p, s_gdn_p = _gdn_scan(t_m, rhs, qdec, kdec, intra, glast, b)
    o_moba_p = _moba_prompt(q_p.reshape(b, s, MOBA_WIDTH), kbf_p.reshape(b, s, MOBA_WIDTH),
                            vbf_p.reshape(b, s, MOBA_WIDTH),
                            kmean_p.reshape(b, s // MOBA_BLOCK, MOBA_WIDTH))
    y_p = _merge(o_gdn_p.reshape(b * s, GDN_WIDTH), za_p, o_moba_p.reshape(b * s, MOBA_WIDTH), zb_p,
                 x_prompt.reshape(b * s, d), gw, w_o, nf)

    cos_s, sin_s = _rope_tables(past_len + jnp.arange(t_len, dtype=jnp.int32))
    reps = ROW_TILE // t_len
    (qkv_s, za_s, q_s, k_s, v_s, zb_s, ba_s) = _inproj(
        x_sample.reshape(db * t_len, d), nw, w_p, jnp.tile(cos_s, (reps, 1)), jnp.tile(sin_s, (reps, 1)),
        prompt=False)
    qkv_s3 = qkv_s.reshape(db, t_len, GDN_CONV_CH)
    tb = lambda a: jnp.transpose(a, (1, 0, 2))
    qg, kg, vg, beta_t, dec_t = _gdn_sample_prep(
        tb(qkv_s3), tb(state_conv[0]), tb(ba_s.reshape(db, t_len, LANES)), cw, arow)
    to_cols = lambda a: jnp.transpose(a.reshape(t_len, db, GDN_HEADS, HEAD_DIM), (1, 2, 3, 0))
    o_gdn_s, s_gdn_s = _gdn_sample_rec(state_gdn[0], to_cols(qg), to_cols(kg), tb(vg), tb(beta_t),
                                       tb(dec_t))

    page_flat = page_table.reshape(-1).astype(jnp.int32)
    ck = cache_k.reshape(cache_k.shape[1], PAGE_ROWS, HEAD_DIM)
    cv = cache_v.reshape(cache_v.shape[1], PAGE_ROWS, HEAD_DIM)
    rows_s = t_len * MOBA_HEADS
    q2 = jnp.transpose(q_s.reshape(db, t_len, MOBA_HEADS, HEAD_DIM), (0, 2, 1, 3))
    o2 = _moba_sample(page_flat, q2.reshape(db, rows_s, HEAD_DIM), k_s.reshape(db, rows_s, HEAD_DIM),
                      v_s.reshape(db, rows_s, HEAD_DIM), ck, cv, n_pages)
    o_moba_s = jnp.transpose(o2.reshape(db, MOBA_HEADS, t_len, HEAD_DIM), (0, 2, 1, 3))
    y_s = _merge(o_gdn_s.reshape(db * t_len, GDN_WIDTH), za_s, o_moba_s.reshape(db * t_len, MOBA_WIDTH),
                 zb_s, x_sample.reshape(db * t_len, d), gw, w_o, nf)

    ctx = GDN_CONV_W - 1
    heads = lambda a, n, t: a.reshape(1, n, t, MOBA_HEADS, HEAD_DIM)
    return (y_p.reshape(b, s, d), y_s.reshape(db, t_len, d),
            heads(k_p, b, s), heads(v_p, b, s),
            s_gdn_p.reshape(1, b, GDN_HEADS, HEAD_DIM, HEAD_DIM),
            qkv_p3[:, s - ctx:, :][None],
            heads(k_s, db, t_len), heads(v_s, db, t_len),
            s_gdn_s[None],
            qkv_s3[:, t_len - ctx:, :][None])
```

```python
import functools
import math

import jax
import jax.numpy as jnp
from jax import lax
from jax.experimental import pallas as pl
from jax.experimental.pallas import tpu as pltpu

F32 = jnp.float32
BF16 = jnp.bfloat16

HEAD_DIM = 128
GDN_HEADS = 4
MOBA_HEADS = 4
GDN_QK = GDN_HEADS * HEAD_DIM
GDN_WIDTH = GDN_HEADS * HEAD_DIM
GDN_CONV_W = 4
GDN_CONV_CH = 2 * GDN_QK + GDN_WIDTH
GDN_CHUNK = 64
MOBA_WIDTH = MOBA_HEADS * HEAD_DIM
MOBA_BLOCK = 256
MOBA_TOPK = 3
PAGE_SIZE = 128
ROPE_THETA = 10000.0
NORM_EPS = 1e-6
LANES = 128
SUBLANES = 8
MASK_BIAS = -1e30

C_QKV = 0
C_ZA = GDN_CONV_CH
C_QB = C_ZA + GDN_WIDTH
C_KB = C_QB + MOBA_WIDTH
C_VB = C_KB + MOBA_WIDTH
C_ZB = C_VB + MOBA_WIDTH
C_BA = C_ZB + MOBA_WIDTH
IN_COLS_PAD = C_BA + LANES

ROW_TILE = 256
VMEM_LIMIT = 56 * 1024 * 1024


def _dot(a, b):
    return jnp.dot(a, b, preferred_element_type=F32)


def _dot_nt(a, b):
    return lax.dot_general(a, b, (((1,), (1,)), ((), ())), preferred_element_type=F32)


def _dot_tn(a, b):
    return lax.dot_general(a, b, (((0,), (0,)), ((), ())), preferred_element_type=F32)


def _dot_nt_f32(a, b):
    return lax.dot_general(a, b, (((1,), (1,)), ((), ())), preferred_element_type=F32,
                           precision=lax.Precision.HIGHEST)


def _dot_f32(a, b):
    return jnp.dot(a, b, preferred_element_type=F32, precision=lax.Precision.HIGHEST)


def _split_bf16(a):
    hi = a.astype(BF16)
    lo = (a - hi.astype(F32)).astype(BF16)
    return hi, lo


def _dot3(a, b):
    ah, al = _split_bf16(a)
    bh, bl = _split_bf16(b)
    return _dot(ah, bh) + (_dot(ah, bl) + _dot(al, bh))


def _silu(x):
    return x * jax.nn.sigmoid(x)


def _softplus(x):
    return jnp.maximum(x, 0.0) + jnp.log1p(jnp.exp(-jnp.abs(x)))


def _top3_members(gate, valid):
    lane = lax.broadcasted_iota(jnp.int32, gate.shape, 1).astype(F32)
    g = jnp.where(valid, gate, -jnp.inf)
    member = jnp.zeros(gate.shape, dtype=jnp.bool_)
    for _ in range(MOBA_TOPK):
        m = jnp.max(g, axis=1, keepdims=True)
        idx = jnp.min(jnp.where(g == m, lane, float(LANES)), axis=1, keepdims=True)
        pick = lane == idx
        member = jnp.logical_or(member, pick)
        g = jnp.where(pick, -jnp.inf, g)
    return jnp.logical_and(member, valid)


def _inproj_kernel(x_ref, nw_ref, w_ref, cos_ref, sin_ref,
                   qkv_ref, za_ref, q_ref, k_ref, v_ref, zb_ref, ba_ref, *prompt_refs):
    x = x_ref[...]
    ms = jnp.mean(x * x, axis=-1, keepdims=True)
    hb = (x * lax.rsqrt(ms + NORM_EPS) * nw_ref[...]).astype(BF16)

    def proj(c0, n):
        return _dot(hb, w_ref[:, c0:c0 + n])

    def rope(t):
        parts = []
        for h in range(MOBA_HEADS):
            th = t[:, h * HEAD_DIM:(h + 1) * HEAD_DIM]
            parts.append(th * cos_ref[...] + pltpu.roll(th, HEAD_DIM // 2, 1) * sin_ref[...])
        return jnp.concatenate(parts, axis=1)

    for j in range(GDN_CONV_CH // GDN_QK):
        qkv_ref[:, j * GDN_QK:(j + 1) * GDN_QK] = proj(C_QKV + j * GDN_QK, GDN_QK)
    za_ref[...] = proj(C_ZA, GDN_WIDTH)
    q_ref[...] = rope(proj(C_QB, MOBA_WIDTH))
    k = rope(proj(C_KB, MOBA_WIDTH))
    k_ref[...] = k
    v = proj(C_VB, MOBA_WIDTH)
    v_ref[...] = v
    zb_ref[...] = proj(C_ZB, MOBA_WIDTH)
    ba_ref[...] = proj(C_BA, LANES)
    if prompt_refs:
        kbf_ref, vbf_ref, kmean_ref = prompt_refs
        kbf_ref[...] = k.astype(BF16)
        vbf_ref[...] = v.astype(BF16)
        kmean_ref[0] = jnp.mean(k, axis=0, keepdims=True)


def _inproj(x2d, norm_w, w_p, cosf, sinf, *, prompt):
    m, d = x2d.shape
    tm = ROW_TILE
    assert m % tm == 0 and cosf.shape[0] % tm == 0
    n_pos = cosf.shape[0] // tm
    row = lambda i: (i, 0)
    full = lambda i: (0, 0)
    pos = (lambda i: (i % n_pos, 0)) if n_pos > 1 else full
    out_shape = [jax.ShapeDtypeStruct((m, GDN_CONV_CH), F32)]
    out_specs = [pl.BlockSpec((tm, GDN_CONV_CH), row)]
    for _ in range(5):
        out_shape.append(jax.ShapeDtypeStruct((m, MOBA_WIDTH), F32))
        out_specs.append(pl.BlockSpec((tm, MOBA_WIDTH), row))
    out_shape.append(jax.ShapeDtypeStruct((m, LANES), F32))
    out_specs.append(pl.BlockSpec((tm, LANES), row))
    if prompt:
        assert tm == MOBA_BLOCK
        for _ in range(2):
            out_shape.append(jax.ShapeDtypeStruct((m, MOBA_WIDTH), BF16))
            out_specs.append(pl.BlockSpec((tm, MOBA_WIDTH), row))
        out_shape.append(jax.ShapeDtypeStruct((m // tm, 1, MOBA_WIDTH), F32))
        out_specs.append(pl.BlockSpec((1, 1, MOBA_WIDTH), lambda i: (i, 0, 0)))
    return pl.pallas_call(
        _inproj_kernel,
        grid=(m // tm,),
        in_specs=[pl.BlockSpec((tm, d), row), pl.BlockSpec((1, d), full),
                  pl.BlockSpec((d, IN_COLS_PAD), full),
                  pl.BlockSpec((tm, HEAD_DIM), pos), pl.BlockSpec((tm, HEAD_DIM), pos)],
        out_specs=out_specs,
        out_shape=out_shape,
        compiler_params=pltpu.CompilerParams(dimension_semantics=("parallel",),
                                             vmem_limit_bytes=VMEM_LIMIT),
        name="inproj_prompt" if prompt else "inproj_sample",
    )(x2d, norm_w, w_p, cosf, sinf)


def _conv_silu_norm(c):
    c = _silu(c)
    qs, ks = [], []
    for h in range(GDN_HEADS):
        qh = c[:, h * HEAD_DIM:(h + 1) * HEAD_DIM]
        kh = c[:, GDN_QK + h * HEAD_DIM:GDN_QK + (h + 1) * HEAD_DIM]
        qs.append(qh * (lax.rsqrt(jnp.sum(qh * qh, axis=-1, keepdims=True) + NORM_EPS)
                        * (HEAD_DIM ** -0.5)))
        ks.append(kh * lax.rsqrt(jnp.sum(kh * kh, axis=-1, keepdims=True) + NORM_EPS))
    return qs, ks, c[:, 2 * GDN_QK:]


def _gdn_prep_kernel(x_ref, halo_ref, ba_ref, bat_ref, cw_ref, arow_ref, acol_ref,
                     n_ref, intra_ref, rhs_ref, qdec_ref, kdec_ref, glast_ref, buf_ref):
    i = pl.program_id(1)
    tt = x_ref.shape[1]
    c = GDN_CHUNK
    halo = jnp.where(i > 0, halo_ref[0], 0.0)
    buf_ref[0:SUBLANES, :] = halo
    buf_ref[SUBLANES:SUBLANES + tt, :] = x_ref[0]
    conv = buf_ref[SUBLANES - 3:SUBLANES - 3 + tt, :] * cw_ref[0:1, :]
    for w in range(1, GDN_CONV_W):
        conv = conv + buf_ref[SUBLANES - 3 + w:SUBLANES - 3 + w + tt, :] * cw_ref[w:w + 1, :]
    qs, ks, v = _conv_silu_norm(conv)

    ba = ba_ref[0]
    beta_full = jax.nn.sigmoid(ba)
    g_full = -jnp.exp(arow_ref[0:1, :]) * _softplus(ba + arow_ref[1:2, :])
    bat = bat_ref[0]
    gt_full = -jnp.exp(acol_ref[0]) * _softplus(bat + acol_ref[1])

    ri = lax.broadcasted_iota(jnp.int32, (c, c), 0)
    ci = lax.broadcasted_iota(jnp.int32, (c, c), 1)
    tril = ri >= ci
    strict = ri > ci
    ltri = jnp.where(tril, 1.0, 0.0).astype(F32)
    utri = jnp.where(ri <= ci, 1.0, 0.0).astype(F32)

    for cc in range(tt // c):
        rows = slice(cc * c, (cc + 1) * c)
        gcol_all = _dot_f32(ltri, g_full[rows, :])
        grow_all = _dot_f32(gt_full[:, rows], utri)
        for h in range(GDN_HEADS):
            gc_c = gcol_all[:, GDN_HEADS + h:GDN_HEADS + h + 1]
            gc_r = grow_all[GDN_HEADS + h:GDN_HEADS + h + 1, :]
            gamma = jnp.where(tril, jnp.exp(jnp.where(tril, gc_c - gc_r, 0.0)), 0.0)
            beta = beta_full[rows, h:h + 1]
            qh, kh = qs[h][rows, :], ks[h][rows, :]
            vh = v[rows, h * HEAD_DIM:(h + 1) * HEAD_DIM]
            kb = kh * beta
            k16 = kh.astype(BF16)
            n_ref[h, cc] = jnp.where(strict, _dot_nt(kb.astype(BF16), k16) * gamma, 0.0)
            intra_ref[h, cc] = jnp.where(tril, _dot_nt(qh.astype(BF16), k16) * gamma, 0.0)
            egc = jnp.exp(gc_c)
            rhs_ref[h, cc, :, 0:HEAD_DIM] = vh * beta
            rhs_ref[h, cc, :, HEAD_DIM:2 * HEAD_DIM] = kb * egc
            qdec_ref[h, cc] = qh * egc
            gl = gc_c[c - 1:c, :]
            kdec_ref[h, cc] = kh * jnp.exp(gl - gc_c)
            glast_ref[h, cc] = jnp.broadcast_to(jnp.exp(gl), (SUBLANES, LANES))


def _gdn_prep(qkv, ba, bat, conv_w, arow, acol):
    b, s, _ = qkv.shape
    tt = ROW_TILE
    c = GDN_CHUNK
    cpt = tt // c
    nc = s // c
    hb = tt // SUBLANES
    bh = b * GDN_HEADS

    def o(x):
        return (jax.ShapeDtypeStruct((bh, nc, c, x), F32),
                pl.BlockSpec((GDN_HEADS, cpt, c, x), lambda bi, i: (bi, i, 0, 0)))

    outs = [o(c), o(c), o(2 * HEAD_DIM), o(HEAD_DIM), o(HEAD_DIM)]
    outs.append((jax.ShapeDtypeStruct((bh, nc, SUBLANES, LANES), F32),
                 pl.BlockSpec((GDN_HEADS, cpt, SUBLANES, LANES), lambda bi, i: (bi, i, 0, 0))))
    return pl.pallas_call(
        _gdn_prep_kernel,
        grid=(b, s // tt),
        in_specs=[
            pl.BlockSpec((1, tt, GDN_CONV_CH), lambda bi, i: (bi, i, 0)),
            pl.BlockSpec((1, SUBLANES, GDN_CONV_CH), lambda bi, i: (bi, jnp.maximum(i * hb - 1, 0), 0)),
            pl.BlockSpec((1, tt, LANES), lambda bi, i: (bi, i, 0)),
            pl.BlockSpec((1, SUBLANES, tt), lambda bi, i: (bi, 0, i)),
            pl.BlockSpec((GDN_CONV_W, GDN_CONV_CH), lambda bi, i: (0, 0)),
            pl.BlockSpec((2, LANES), lambda bi, i: (0, 0)),
            pl.BlockSpec((2, SUBLANES, 1), lambda bi, i: (0, 0, 0)),
        ],
        out_specs=[x[1] for x in outs],
        out_shape=[x[0] for x in outs],
        scratch_shapes=[pltpu.VMEM((SUBLANES + tt, GDN_CONV_CH), F32)],
        compiler_params=pltpu.CompilerParams(dimension_semantics=("parallel", "parallel"),
                                             vmem_limit_bytes=VMEM_LIMIT),
        name="gdn_prep",
    )(qkv, qkv, ba, bat, conv_w, arow, acol)


def _tri_inv_kernel(n_ref, t_ref):
    c = n_ref.shape[0]
    col = lax.broadcasted_iota(jnp.int32, (c, LANES), 0)
    t_ref[...] = jnp.zeros(t_ref.shape, F32)

    def outer(i, carry):
        def inner(jg, acc):
            j0 = pl.multiple_of(jg * SUBLANES, SUBLANES)
            n_rows = n_ref[i, pl.ds(j0, SUBLANES), :]
            for r in range(SUBLANES):
                acc = acc - n_rows[r:r + 1, :] * t_ref[j0 + r]
            return acc
        groups = lax.shift_right_logical(i + (SUBLANES - 1), SUBLANES.bit_length() - 1)
        acc = lax.fori_loop(0, groups, inner, jnp.where(col == i, 1.0, 0.0).astype(F32))
        t_ref[i] = acc
        return carry

    lax.fori_loop(0, c, outer, 0)


def _tri_inv(nt):
    c, _, nmat = nt.shape
    assert nmat % LANES == 0
    spec = pl.BlockSpec((c, c, LANES), lambda g: (0, 0, g))
    return pl.pallas_call(
        _tri_inv_kernel,
        grid=(nmat // LANES,),
        in_specs=[spec],
        out_specs=spec,
        out_shape=jax.ShapeDtypeStruct(nt.shape, F32),
        compiler_params=pltpu.CompilerParams(dimension_semantics=("parallel",)),
        name="gdn_tri_inv",
    )(nt)


def _gdn_scan_kernel(t_ref, rhs_ref, qdec_ref, kdec_ref, intra_ref, glast_ref, o_ref, s_ref):
    i = pl.program_id(0)
    bh, ct, c, _ = t_ref.shape

    @pl.when(i == 0)
    def _():
        s_ref[...] = jnp.zeros(s_ref.shape, F32)

    heads = range(bh)
    for cc in range(ct):
        sol = [_dot3(t_ref[n, cc], rhs_ref[n, cc]) for n in heads]
        wqs = [_dot(jnp.concatenate([sol[n][:, HEAD_DIM:], qdec_ref[n, cc]], axis=0).astype(BF16),
                    s_ref[n].astype(BF16)) for n in heads]
        vn16 = [(sol[n][:, :HEAD_DIM] - wqs[n][:c]).astype(BF16) for n in heads]
        for n in heads:
            s_ref[n] = (s_ref[n] * glast_ref[n, cc, 0:1, :]
                        + _dot_tn(kdec_ref[n, cc].astype(BF16), vn16[n]))
        for n in heads:
            b, h = n // GDN_HEADS, n % GDN_HEADS
            o_ref[b, cc * c:(cc + 1) * c, h * HEAD_DIM:(h + 1) * HEAD_DIM] = (
                wqs[n][c:] + _dot(intra_ref[n, cc].astype(BF16), vn16[n]))


def _gdn_scan(t, rhs, qdec, kdec, intra, glast, b):
    bh, nc, c, _ = t.shape
    ct = 4
    assert nc % ct == 0

    def spec(x):
        return pl.BlockSpec((bh, ct, x.shape[2], x.shape[3]), lambda i: (0, i, 0, 0))

    return pl.pallas_call(
        _gdn_scan_kernel,
        grid=(nc // ct,),
        in_specs=[spec(t), spec(rhs), spec(qdec), spec(kdec), spec(intra), spec(glast)],
        out_specs=[pl.BlockSpec((b, ct * c, GDN_WIDTH), lambda i: (0, i, 0)),
                   pl.BlockSpec((bh, HEAD_DIM, HEAD_DIM), lambda i: (0, 0, 0))],
        out_shape=[jax.ShapeDtypeStruct((b, nc * c, GDN_WIDTH), F32),
                   jax.ShapeDtypeStruct((bh, HEAD_DIM, HEAD_DIM), F32)],
        compiler_params=pltpu.CompilerParams(dimension_semantics=("arbitrary",),
                                             vmem_limit_bytes=VMEM_LIMIT),
        name="gdn_scan",
    )(t, rhs, qdec, kdec, intra, glast)


MOBA_GROUP = 4
MOBA_SPLIT = 2
MOBA_QTILE = MOBA_BLOCK


def _moba_prompt_kernel(q_ref, k_ref, v_ref, kmean_ref, o_ref):
    i = pl.program_id(2)
    tq = q_ref.shape[1]
    blk = MOBA_BLOCK
    scale = HEAD_DIM ** -0.5
    q = q_ref[0]
    q16 = q.astype(BF16)
    kmean = jnp.concatenate(
        [kmean_ref[0], jnp.zeros((LANES - kmean_ref.shape[1], HEAD_DIM), F32)], axis=0)
    bpt = tq // blk
    blk_shift = blk.bit_length() - 1
    row_blk = i * bpt + lax.shift_right_logical(
        lax.broadcasted_iota(jnp.int32, (tq, LANES), 0), blk_shift)
    past = lax.broadcasted_iota(jnp.int32, (tq, LANES), 1) < row_blk
    member = _top3_members(_dot_nt_f32(q, kmean), past)
    q_aug = jnp.concatenate([q16, jnp.where(member, 0.0, MASK_BIAS).astype(BF16)], axis=1)
    grp = MOBA_GROUP * blk
    c_exp = scale * math.log2(math.e)
    tile_rows = 2 * SUBLANES
    tile_lane = lax.broadcasted_iota(jnp.int32, (tile_rows, LANES), 1)

    rq = lax.broadcasted_iota(jnp.int32, (blk, blk), 0)
    ck = lax.broadcasted_iota(jnp.int32, (blk, blk), 1)
    ms, ls, accs = [], [], []
    for r in range(bpt):
        start = pl.multiple_of((i * bpt + r) * blk, blk)
        s = _dot_nt(q16[r * blk:(r + 1) * blk, :], k_ref[0, pl.ds(start, blk), :])
        s = jnp.where(ck <= rq, s, -jnp.inf)
        m_r = jnp.max(s, axis=1, keepdims=True)
        p = jnp.exp2((s - m_r) * c_exp)
        ms.append(m_r)
        ls.append(jnp.sum(p, axis=1, keepdims=True))
        accs.append(_dot(p.astype(BF16), v_ref[0, pl.ds(start, blk), :]))
    m, l, acc = (jnp.concatenate(x, axis=0) for x in (ms, ls, accs))

    sub = grp // MOBA_SPLIT

    def body(j, carry):
        m, l, acc = carry
        st = pl.multiple_of(j * grp, grp)
        onehot = []
        for n in range(MOBA_GROUP):
            tile = jnp.where(tile_lane == j * MOBA_GROUP + n, 1.0, 0.0).astype(BF16)
            onehot += [tile] * (blk // tile_rows)
        k_aug = jnp.concatenate([k_ref[0, pl.ds(st, grp), :], jnp.concatenate(onehot, axis=0)], axis=1)
        raw = [_dot_nt(q_aug, k_aug[u * sub:(u + 1) * sub, :]) for u in range(MOBA_SPLIT)]
        for u in range(MOBA_SPLIT):
            m_new = jnp.maximum(m, jnp.max(raw[u], axis=1, keepdims=True))
            alpha = jnp.exp2((m - m_new) * c_exp)
            p = jnp.exp2((raw[u] - m_new) * c_exp)
            l = l * alpha + jnp.sum(p, axis=1, keepdims=True)
            acc = acc * alpha + _dot(p.astype(BF16), v_ref[0, pl.ds(st + u * sub, sub), :])
            m = m_new
        return m, l, acc

    n_groups = lax.shift_right_logical(i * bpt + (bpt - 1) + (MOBA_GROUP - 1),
                                       MOBA_GROUP.bit_length() - 1)
    m, l, acc = lax.fori_loop(0, n_groups, body, (m, l, acc))
    o_ref[0] = acc / l


def _moba_prompt(q, kbf, vbf, kmean):
    b, s, _ = q.shape
    tq = MOBA_QTILE
    nb = s // MOBA_BLOCK
    assert s % tq == 0 and tq % MOBA_BLOCK == 0 and nb <= LANES and nb % MOBA_GROUP == 0
    return pl.pallas_call(
        _moba_prompt_kernel,
        grid=(b, MOBA_HEADS, s // tq),
        in_specs=[pl.BlockSpec((1, tq, HEAD_DIM), lambda bi, h, i: (bi, i, h)),
                  pl.BlockSpec((1, s, HEAD_DIM), lambda bi, h, i: (bi, 0, h)),
                  pl.BlockSpec((1, s, HEAD_DIM), lambda bi, h, i: (bi, 0, h)),
                  pl.BlockSpec((1, nb, HEAD_DIM), lambda bi, h, i: (bi, 0, h))],
        out_specs=pl.BlockSpec((1, tq, HEAD_DIM), lambda bi, h, i: (bi, i, h)),
        out_shape=jax.ShapeDtypeStruct((b, s, MOBA_WIDTH), F32),
        compiler_params=pltpu.CompilerParams(
            dimension_semantics=("parallel", "parallel", "arbitrary"),
            vmem_limit_bytes=VMEM_LIMIT),
        name="moba_prompt",
    )(q, kbf, vbf, kmean)


def _gdn_sample_prep_kernel(x_ref, st_ref, ba_ref, cw_ref, arow_ref,
                            q_ref, k_ref, v_ref, beta_ref, dec_ref):
    t_len = x_ref.shape[0]
    ctx = GDN_CONV_W - 1

    def src(tt):
        return st_ref[tt + ctx] if tt < 0 else x_ref[tt]

    for t in range(t_len):
        conv = src(t - ctx) * cw_ref[0:1, :]
        for w in range(1, GDN_CONV_W):
            conv = conv + src(t - ctx + w) * cw_ref[w:w + 1, :]
        qs, ks, v = _conv_silu_norm(conv)
        q_ref[t] = jnp.concatenate(qs, axis=1)
        k_ref[t] = jnp.concatenate(ks, axis=1)
        v_ref[t] = v
        ba = ba_ref[t]
        beta_full = jax.nn.sigmoid(ba)
        dec_full = jnp.exp(-jnp.exp(arow_ref[0:1, :]) * _softplus(ba + arow_ref[1:2, :]))
        rows = ba.shape[0]
        beta_ref[t] = jnp.concatenate(
            [jnp.broadcast_to(beta_full[:, h:h + 1], (rows, HEAD_DIM)) for h in range(GDN_HEADS)], axis=1)
        dec_ref[t] = jnp.concatenate(
            [jnp.broadcast_to(dec_full[:, GDN_HEADS + h:GDN_HEADS + h + 1], (rows, HEAD_DIM))
             for h in range(GDN_HEADS)], axis=1)


def _gdn_sample_prep(x_t, st_t, ba_t, conv_w, arow):
    t_len, db, _ = x_t.shape
    bt = min(db, 32)
    assert db % bt == 0

    def spec(t, w):
        return pl.BlockSpec((t, bt, w), lambda i: (0, i, 0))

    out = jax.ShapeDtypeStruct((t_len, db, GDN_WIDTH), F32)
    return pl.pallas_call(
        _gdn_sample_prep_kernel,
        grid=(db // bt,),
        in_specs=[spec(t_len, GDN_CONV_CH), spec(GDN_CONV_W - 1, GDN_CONV_CH), spec(t_len, LANES),
                  pl.BlockSpec((GDN_CONV_W, GDN_CONV_CH), lambda i: (0, 0)),
                  pl.BlockSpec((2, LANES), lambda i: (0, 0))],
        out_specs=[spec(t_len, GDN_WIDTH)] * 5,
        out_shape=[out] * 5,
        compiler_params=pltpu.CompilerParams(dimension_semantics=("parallel",),
                                             vmem_limit_bytes=VMEM_LIMIT),
        name="gdn_sample_prep",
    )(x_t, st_t, ba_t, conv_w, arow)


def _gdn_sample_rec_kernel(s0_ref, qt_ref, kt_ref, v_ref, beta_ref, dec_ref, o_ref, s_ref):
    bt = s0_ref.shape[0]
    t_len = v_ref.shape[1]

    def per_batch(bi, carry):
        v, beta, dec = v_ref[bi], beta_ref[bi], dec_ref[bi]
        outs = []
        for h in range(GDN_HEADS):
            cols = slice(h * HEAD_DIM, (h + 1) * HEAD_DIM)
            s = s0_ref[bi, h]
            kt = kt_ref[bi, h]
            qt = qt_ref[bi, h]
            rows = []
            for t in range(t_len):
                s = s * dec[t:t + 1, cols]
                kcol = kt[:, t:t + 1]
                kv = jnp.sum(kcol * s, axis=0, keepdims=True)
                upd = (v[t:t + 1, cols] - kv) * beta[t:t + 1, cols]
                s = s + kcol * upd
                rows.append(jnp.sum(qt[:, t:t + 1] * s, axis=0, keepdims=True))
            s_ref[bi, h] = s
            outs.append(jnp.concatenate(rows, axis=0))
        o_ref[bi] = jnp.concatenate(outs, axis=1)
        return carry

    lax.fori_loop(0, bt, per_batch, 0)


def _gdn_sample_rec(s0, qt, kt, v, beta, dec):
    db, t_len, _ = v.shape
    bt = min(db, 8)
    assert db % bt == 0
    st_spec = pl.BlockSpec((bt, GDN_HEADS, HEAD_DIM, HEAD_DIM), lambda i: (i, 0, 0, 0))
    tr_spec = pl.BlockSpec((bt, GDN_HEADS, HEAD_DIM, t_len), lambda i: (i, 0, 0, 0))
    tok_spec = pl.BlockSpec((bt, t_len, GDN_WIDTH), lambda i: (i, 0, 0))
    return pl.pallas_call(
        _gdn_sample_rec_kernel,
        grid=(db // bt,),
        in_specs=[st_spec, tr_spec, tr_spec, tok_spec, tok_spec, tok_spec],
        out_specs=[tok_spec, st_spec],
        out_shape=[jax.ShapeDtypeStruct((db, t_len, GDN_WIDTH), F32),
                   jax.ShapeDtypeStruct(s0.shape, F32)],
        compiler_params=pltpu.CompilerParams(dimension_semantics=("parallel",),
                                             vmem_limit_bytes=VMEM_LIMIT),
        name="gdn_sample_rec",
    )(s0, qt, kt, v, beta, dec)


PAGES_PER_STEP = 16
PAGE_ROWS = PAGE_SIZE * MOBA_HEADS
PAGES_PER_BLOCK = MOBA_BLOCK // PAGE_SIZE
BLOCKS_PER_STEP = PAGES_PER_STEP // PAGES_PER_BLOCK
STEP_ROWS = PAGES_PER_STEP * PAGE_ROWS
HEAD_SHIFT = MOBA_HEADS.bit_length() - 1
assert 1 << HEAD_SHIFT == MOBA_HEADS and SUBLANES == 2 * MOBA_HEADS and PAGES_PER_BLOCK == 2


def _page_specs(n_pages, batch_of):
    def mk(slot):
        return pl.BlockSpec(
            (1, PAGE_ROWS, HEAD_DIM),
            lambda bp, g, pt: (pt[batch_of(bp) * n_pages + g * PAGES_PER_STEP + slot], 0, 0))
    return [mk(s) for s in range(PAGES_PER_STEP)]


def _moba_sample_kernel(pt_ref, q_ref, kn_ref, vn_ref, *refs):
    k_refs = refs[:PAGES_PER_STEP]
    v_refs = refs[PAGES_PER_STEP:2 * PAGES_PER_STEP]
    o_ref, kbuf_ref, vbuf_ref, kmean_ref, mem_ref, m_ref, l_ref, acc_ref = refs[2 * PAGES_PER_STEP:]
    bp, g = pl.program_id(0), pl.program_id(1)
    n_batch = pl.num_programs(0) - 1
    n_steps = pl.num_programs(1)
    rows = q_ref.shape[1]
    t_len = rows // MOBA_HEADS
    scale = HEAD_DIM ** -0.5
    t_shift = t_len.bit_length() - 1
    assert 1 << t_shift == t_len and kn_ref.shape[1] == rows
    step_lanes = BLOCKS_PER_STEP * MOBA_HEADS
    par = jnp.bitwise_and(bp, 1)
    row_i = lax.broadcasted_iota(jnp.int32, (rows, LANES), 0)
    lane_i = lax.broadcasted_iota(jnp.int32, (rows, LANES), 1)
    row_head = lax.shift_right_logical(row_i, t_shift)
    lane_head = jnp.bitwise_and(lane_i, MOBA_HEADS - 1)

    @pl.when(jnp.logical_and(bp == 0, g == 0))
    def _():
        kmean_ref[...] = jnp.zeros(kmean_ref.shape, F32)

    @pl.when(jnp.logical_and(bp > 0, g == 0))
    def _():
        q = q_ref[0]
        q16 = q.astype(BF16)
        n_cand = n_steps * step_lanes
        valid = jnp.logical_and(lane_head == row_head, lane_i < n_cand)
        member = _top3_members(_dot_nt_f32(q, kmean_ref[...]), valid)
        mem_ref[...] = jnp.where(member, 1.0, 0.0)
        pad = jnp.zeros((LANES - rows, HEAD_DIM), F32)
        kn = jnp.concatenate([kn_ref[0], pad], axis=0).astype(BF16)
        vn = jnp.concatenate([vn_ref[0], pad], axis=0).astype(BF16)
        s = _dot_nt(q16, kn) * scale
        ok = jnp.logical_and(lane_head == row_head, lane_i < rows)
        ok = jnp.logical_and(ok, lax.shift_right_logical(lane_i, HEAD_SHIFT)
                             <= jnp.bitwise_and(row_i, t_len - 1))
        s = jnp.where(ok, s, -jnp.inf)
        m = jnp.max(s, axis=1, keepdims=True)
        p = jnp.exp(s - m)
        m_ref[...] = jnp.broadcast_to(m, m_ref.shape)
        l_ref[...] = jnp.broadcast_to(jnp.sum(p, axis=1, keepdims=True), l_ref.shape)
        acc_ref[...] = _dot(p.astype(BF16), vn)

    @pl.when(bp > 0)
    def _():
        for slot in range(PAGES_PER_STEP):
            vbuf_ref[slot * PAGE_ROWS:(slot + 1) * PAGE_ROWS, :] = v_refs[slot][0].astype(BF16)
        member = mem_ref[...]
        blk_rows = PAGES_PER_BLOCK * PAGE_ROWS
        col_head = jnp.bitwise_and(lax.broadcasted_iota(jnp.int32, (rows, blk_rows), 1), MOBA_HEADS - 1)
        own_head = col_head == lax.shift_right_logical(
            lax.broadcasted_iota(jnp.int32, (rows, blk_rows), 0), t_shift)
        cand_blk = lax.shift_right_logical(lane_i, HEAD_SHIFT)
        bias = []
        for n in range(BLOCKS_PER_STEP):
            picked = jnp.sum(jnp.where(cand_blk == g * BLOCKS_PER_STEP + n, member, 0.0),
                             axis=1, keepdims=True) > 0.0
            bias.append(jnp.where(jnp.logical_and(own_head, picked), 0.0, -jnp.inf))
        s = _dot_nt(q_ref[0].astype(BF16), kbuf_ref[1 - par, g]) * scale + jnp.concatenate(bias, axis=1)
        m = m_ref[:, 0:1]
        m_new = jnp.maximum(m, jnp.max(s, axis=1, keepdims=True))
        alpha = jnp.exp(m - m_new)
        p = jnp.exp(s - m_new)
        l = l_ref[:, 0:1] * alpha + jnp.sum(p, axis=1, keepdims=True)
        acc = acc_ref[...] * alpha + _dot(p.astype(BF16), vbuf_ref[...])
        m_ref[...] = jnp.broadcast_to(m_new, m_ref.shape)
        l_ref[...] = jnp.broadcast_to(l, l_ref.shape)
        acc_ref[...] = acc

        @pl.when(g == n_steps - 1)
        def _():
            o_ref[0] = acc / l

    @pl.when(bp < n_batch)
    def _():
        low = lax.broadcasted_iota(jnp.int32, (SUBLANES, HEAD_DIM), 0) < MOBA_HEADS

        def block_mean(n):
            tot = None
            for r in range(PAGES_PER_BLOCK):
                slot = n * PAGES_PER_BLOCK + r
                page = k_refs[slot][0]
                kbuf_ref[par, g, slot * PAGE_ROWS:(slot + 1) * PAGE_ROWS, :] = page.astype(BF16)
                part = jnp.sum(page.reshape(PAGE_ROWS // SUBLANES, SUBLANES, HEAD_DIM), axis=0)
                tot = part if tot is None else tot + part
            return (tot + pltpu.roll(tot, MOBA_HEADS, 0)) * (1.0 / MOBA_BLOCK)

        tiles = [jnp.where(low, block_mean(2 * n), block_mean(2 * n + 1))
                 for n in range(BLOCKS_PER_STEP // 2)]
        kmean_ref[pl.ds(pl.multiple_of(g * step_lanes, step_lanes), step_lanes), :] = (
            jnp.concatenate(tiles, axis=0))


def _moba_sample(page_flat, q2, kn2, vn2, cache_k, cache_v, n_pages):
    db, rows, _ = q2.shape
    assert n_pages % PAGES_PER_STEP == 0
    n_steps = n_pages // PAGES_PER_STEP
    n_cand = n_steps * BLOCKS_PER_STEP * MOBA_HEADS
    assert rows <= LANES and n_cand <= LANES
    stage1 = lambda bp: jnp.minimum(bp, db - 1)
    stage2 = lambda bp: jnp.maximum(bp - 1, 0)
    tok = pl.BlockSpec((1, rows, HEAD_DIM), lambda bp, g, pt: (stage2(bp), 0, 0))
    return pl.pallas_call(
        _moba_sample_kernel,
        grid_spec=pltpu.PrefetchScalarGridSpec(
            num_scalar_prefetch=1,
            grid=(db + 1, n_steps),
            in_specs=[tok, tok, tok] + _page_specs(n_pages, stage1) + _page_specs(n_pages, stage2),
            out_specs=tok,
            scratch_shapes=[pltpu.VMEM((2, n_steps, STEP_ROWS, HEAD_DIM), BF16),
                            pltpu.VMEM((STEP_ROWS, HEAD_DIM), BF16),
                            pltpu.VMEM((LANES, HEAD_DIM), F32),
                            pltpu.VMEM((rows, LANES), F32),
                            pltpu.VMEM((rows, LANES), F32), pltpu.VMEM((rows, LANES), F32),
                            pltpu.VMEM((rows, HEAD_DIM), F32)],
        ),
        out_shape=jax.ShapeDtypeStruct((db, rows, HEAD_DIM), F32),
        compiler_params=pltpu.CompilerParams(dimension_semantics=("arbitrary", "arbitrary"),
                                             vmem_limit_bytes=VMEM_LIMIT),
        name="moba_sample",
    )(page_flat, q2, kn2, vn2, *([cache_k] * PAGES_PER_STEP), *([cache_v] * PAGES_PER_STEP))


def _merge_kernel(og_ref, za_ref, om_ref, zb_ref, x_ref, gw_ref, wo_ref, nf_ref, y_ref):
    og = og_ref[...]
    parts = []
    for h in range(GDN_HEADS):
        oh = og[:, h * HEAD_DIM:(h + 1) * HEAD_DIM]
        parts.append(oh * lax.rsqrt(jnp.mean(oh * oh, axis=-1, keepdims=True) + NORM_EPS) * gw_ref[...])
    ga = jnp.concatenate(parts, axis=1) * _silu(za_ref[...])
    gb = om_ref[...] * _silu(zb_ref[...])
    cat = jnp.concatenate([ga, gb], axis=1).astype(BF16)
    hp = x_ref[...] + _dot(cat, wo_ref[...])
    y_ref[...] = hp * lax.rsqrt(jnp.mean(hp * hp, axis=-1, keepdims=True) + NORM_EPS) * nf_ref[...]


def _merge(og, za, om, zb, x2d, gdn_norm_w, w_out16, norm_f_w):
    m, d = x2d.shape
    tm = ROW_TILE
    assert m % tm == 0
    row = lambda i: (i, 0)
    full = lambda i: (0, 0)
    half = pl.BlockSpec((tm, GDN_WIDTH), row)
    return pl.pallas_call(
        _merge_kernel,
        grid=(m // tm,),
        in_specs=[half, half, half, half, pl.BlockSpec((tm, d), row),
                  pl.BlockSpec((1, HEAD_DIM), full), pl.BlockSpec(w_out16.shape, full),
                  pl.BlockSpec((1, d), full)],
        out_specs=pl.BlockSpec((tm, d), row),
        out_shape=jax.ShapeDtypeStruct((m, d), F32),
        compiler_params=pltpu.CompilerParams(dimension_semantics=("parallel",),
                                             vmem_limit_bytes=VMEM_LIMIT),
        name="merge_out",
    )(og, za, om, zb, x2d, gdn_norm_w, w_out16, norm_f_w)


def _rope_tables(pos):
    half = HEAD_DIM // 2
    inv_freq = ROPE_THETA ** (-jnp.arange(half, dtype=F32) / half)
    ang = pos.astype(F32)[:, None] * inv_freq[None, :]
    cos, sin = jnp.cos(ang), jnp.sin(ang)
    return jnp.concatenate([cos, cos], axis=1), jnp.concatenate([-sin, sin], axis=1)


def _relayout_w_in(w):
    o = 0
    parts = {}
    for name, n in (("qkv", GDN_CONV_CH), ("za", GDN_WIDTH), ("b", GDN_HEADS), ("a", GDN_HEADS),
                    ("qb", MOBA_WIDTH), ("kb", MOBA_WIDTH), ("vb", MOBA_WIDTH), ("zb", MOBA_WIDTH)):
        parts[name] = w[:, o:o + n]
        o += n
    pad = jnp.zeros((w.shape[0], LANES - 2 * GDN_HEADS), w.dtype)
    return jnp.concatenate([parts["qkv"], parts["za"], parts["qb"], parts["kb"], parts["vb"],
                            parts["zb"], parts["b"], parts["a"], pad], axis=1).astype(BF16)


def kernel(x_prompt, x_sample, cache_k, cache_v, state_gdn, state_conv, page_table, norm_in_w, w_in,
           conv_w, a_log, dt_bias, gdn_norm_w, w_out, norm_f_w):
    b, s, d = x_prompt.shape
    db, t_len, _ = x_sample.shape
    depth = w_in.shape[0]
    assert depth == 1
    n_pages = page_table.shape[1]
    past_len = n_pages * PAGE_SIZE
    assert past_len % MOBA_BLOCK == 0
    assert t_len <= PAGE_SIZE and t_len >= GDN_CONV_W - 1 and (db * t_len) % ROW_TILE == 0
    assert ROW_TILE % t_len == 0

    w_p = _relayout_w_in(w_in[0])
    w_o = w_out[0].astype(BF16)
    nw = norm_in_w[0][None, :]
    cw = conv_w[0]
    zeros4 = jnp.zeros((GDN_HEADS,), F32)
    lane_pad = jnp.zeros((LANES - 2 * GDN_HEADS,), F32)
    arow = jnp.stack([jnp.concatenate([zeros4, a_log[0], lane_pad]),
                      jnp.concatenate([zeros4, dt_bias[0], lane_pad])])
    acol = jnp.stack([jnp.concatenate([zeros4, a_log[0]]),
                      jnp.concatenate([zeros4, dt_bias[0]])])[:, :, None]
    gw = gdn_norm_w[0][None, :]
    nf = norm_f_w[None, :]

    cos_p, sin_p = _rope_tables(jnp.arange(s, dtype=jnp.int32))
    (qkv_p, za_p, q_p, k_p, v_p, zb_p, ba_p, kbf_p, vbf_p, kmean_p) = _inproj(
        x_prompt.reshape(b * s, d), nw, w_p, cos_p, sin_p, prompt=True)
    qkv_p3 = qkv_p.reshape(b, s, GDN_CONV_CH)
    ba_p3 = ba_p.reshape(b, s, LANES)
    bat_p = jnp.transpose(ba_p3[:, :, :SUBLANES], (0, 2, 1))
    n_m, intra, rhs, qdec, kdec, glast = _gdn_prep(qkv_p3, ba_p3, bat_p, cw, arow, acol)
    bh, nc, c, _ = n_m.shape
    t_m = _tri_inv(jnp.transpose(n_m.reshape(bh * nc, c, c), (1, 2, 0)))
    t_m = jnp.transpose(t_m, (2, 0, 1)).reshape(bh, nc, c, c)
    o_gdn_p, s_gdn_p = _gdn_scan(t_m, rhs, qdec, kdec, intra, glast, b)
    o_moba_p = _moba_prompt(q_p.reshape(b, s, MOBA_WIDTH), kbf_p.reshape(b, s, MOBA_WIDTH),
                            vbf_p.reshape(b, s, MOBA_WIDTH),
                            kmean_p.reshape(b, s // MOBA_BLOCK, MOBA_WIDTH))
    y_p = _merge(o_gdn_p.reshape(b * s, GDN_WIDTH), za_p, o_moba_p.reshape(b * s, MOBA_WIDTH), zb_p,
                 x_prompt.reshape(b * s, d), gw, w_o, nf)

    cos_s, sin_s = _rope_tables(past_len + jnp.arange(t_len, dtype=jnp.int32))
    reps = ROW_TILE // t_len
    (qkv_s, za_s, q_s, k_s, v_s, zb_s, ba_s) = _inproj(
        x_sample.reshape(db * t_len, d), nw, w_p, jnp.tile(cos_s, (reps, 1)), jnp.tile(sin_s, (reps, 1)),
        prompt=False)
    qkv_s3 = qkv_s.reshape(db, t_len, GDN_CONV_CH)
    tb = lambda a: jnp.transpose(a, (1, 0, 2))
    qg, kg, vg, beta_t, dec_t = _gdn_sample_prep(
        tb(qkv_s3), tb(state_conv[0]), tb(ba_s.reshape(db, t_len, LANES)), cw, arow)
    to_cols = lambda a: jnp.transpose(a.reshape(t_len, db, GDN_HEADS, HEAD_DIM), (1, 2, 3, 0))
    o_gdn_s, s_gdn_s = _gdn_sample_rec(state_gdn[0], to_cols(qg), to_cols(kg), tb(vg), tb(beta_t),
                                       tb(dec_t))

    page_flat = page_table.reshape(-1).astype(jnp.int32)
    ck = cache_k.reshape(cache_k.shape[1], PAGE_ROWS, HEAD_DIM)
    cv = cache_v.reshape(cache_v.shape[1], PAGE_ROWS, HEAD_DIM)
    rows_s = t_len * MOBA_HEADS
    q2 = jnp.transpose(q_s.reshape(db, t_len, MOBA_HEADS, HEAD_DIM), (0, 2, 1, 3))
    o2 = _moba_sample(page_flat, q2.reshape(db, rows_s, HEAD_DIM), k_s.reshape(db, rows_s, HEAD_DIM),
                      v_s.reshape(db, rows_s, HEAD_DIM), ck, cv, n_pages)
    o_moba_s = jnp.transpose(o2.reshape(db, MOBA_HEADS, t_len, HEAD_DIM), (0, 2, 1, 3))
    y_s = _merge(o_gdn_s.reshape(db * t_len, GDN_WIDTH), za_s, o_moba_s.reshape(db * t_len, MOBA_WIDTH),
                 zb_s, x_sample.reshape(db * t_len, d), gw, w_o, nf)

    ctx = GDN_CONV_W - 1
    heads = lambda a, n, t: a.reshape(1, n, t, MOBA_HEADS, HEAD_DIM)
    return (y_p.reshape(b, s, d), y_s.reshape(db, t_len, d),
            heads(k_p, b, s), heads(v_p, b, s),
            s_gdn_p.reshape(1, b, GDN_HEADS, HEAD_DIM, HEAD_DIM),
            qkv_p3[:, s - ctx:, :][None],
            heads(k_s, db, t_len), heads(v_s, db, t_len),
            s_gdn_s[None],
            qkv_s3[:, t_len - ctx:, :][None])
```

```python
import functools
import math

import jax
import jax.numpy as jnp
from jax import lax
from jax.experimental import pallas as pl
from jax.experimental.pallas import tpu as pltpu

F32 = jnp.float32
BF16 = jnp.bfloat16

HEAD_DIM = 128
GDN_HEADS = 4
MOBA_HEADS = 4
GDN_QK = GDN_HEADS * HEAD_DIM
GDN_WIDTH = GDN_HEADS * HEAD_DIM
GDN_CONV_W = 4
GDN_CONV_CH = 2 * GDN_QK + GDN_WIDTH
GDN_CHUNK = 64
MOBA_WIDTH = MOBA_HEADS * HEAD_DIM
MOBA_BLOCK = 256
MOBA_TOPK = 3
PAGE_SIZE = 128
ROPE_THETA = 10000.0
NORM_EPS = 1e-6
LANES = 128
SUBLANES = 8
MASK_BIAS = -1e30

C_QKV = 0
C_ZA = GDN_CONV_CH
C_QB = C_ZA + GDN_WIDTH
C_KB = C_QB + MOBA_WIDTH
C_VB = C_KB + MOBA_WIDTH
C_ZB = C_VB + MOBA_WIDTH
C_BA = C_ZB + MOBA_WIDTH
IN_COLS_PAD = C_BA + LANES

ROW_TILE = 256
VMEM_LIMIT = 56 * 1024 * 1024


def _dot(a, b):
    return jnp.dot(a, b, preferred_element_type=F32)


def _dot_nt(a, b):
    return lax.dot_general(a, b, (((1,), (1,)), ((), ())), preferred_element_type=F32)


def _dot_tn(a, b):
    return lax.dot_general(a, b, (((0,), (0,)), ((), ())), preferred_element_type=F32)


def _dot_nt_f32(a, b):
    return lax.dot_general(a, b, (((1,), (1,)), ((), ())), preferred_element_type=F32,
                           precision=lax.Precision.HIGHEST)


def _dot_f32(a, b):
    return jnp.dot(a, b, preferred_element_type=F32, precision=lax.Precision.HIGHEST)


def _split_bf16(a):
    hi = a.astype(BF16)
    lo = (a - hi.astype(F32)).astype(BF16)
    return hi, lo


def _dot3(a, b):
    ah, al = _split_bf16(a)
    bh, bl = _split_bf16(b)
    return _dot(ah, bh) + (_dot(ah, bl) + _dot(al, bh))


def _silu(x):
    return x * jax.nn.sigmoid(x)


def _softplus(x):
    return jnp.maximum(x, 0.0) + jnp.log1p(jnp.exp(-jnp.abs(x)))


def _top3_members(gate, valid):
    lane = lax.broadcasted_iota(jnp.int32, gate.shape, 1).astype(F32)
    g = jnp.where(valid, gate, -jnp.inf)
    member = jnp.zeros(gate.shape, dtype=jnp.bool_)
    for _ in range(MOBA_TOPK):
        m = jnp.max(g, axis=1, keepdims=True)
        idx = jnp.min(jnp.where(g == m, lane, float(LANES)), axis=1, keepdims=True)
        pick = lane == idx
        member = jnp.logical_or(member, pick)
        g = jnp.where(pick, -jnp.inf, g)
    return jnp.logical_and(member, valid)


def _inproj_kernel(x_ref, nw_ref, w_ref, cos_ref, sin_ref,
                   qkv_ref, za_ref, q_ref, k_ref, v_ref, zb_ref, ba_ref, *prompt_refs):
    x = x_ref[...]
    ms = jnp.mean(x * x, axis=-1, keepdims=True)
    hb = (x * lax.rsqrt(ms + NORM_EPS) * nw_ref[...]).astype(BF16)

    def proj(c0, n):
        return _dot(hb, w_ref[:, c0:c0 + n])

    def rope(t):
        parts = []
        for h in range(MOBA_HEADS):
            th = t[:, h * HEAD_DIM:(h + 1) * HEAD_DIM]
            parts.append(th * cos_ref[...] + pltpu.roll(th, HEAD_DIM // 2, 1) * sin_ref[...])
        return jnp.concatenate(parts, axis=1)

    for j in range(GDN_CONV_CH // GDN_QK):
        qkv_ref[:, j * GDN_QK:(j + 1) * GDN_QK] = proj(C_QKV + j * GDN_QK, GDN_QK)
    za_ref[...] = proj(C_ZA, GDN_WIDTH)
    q_ref[...] = rope(proj(C_QB, MOBA_WIDTH))
    k = rope(proj(C_KB, MOBA_WIDTH))
    k_ref[...] = k
    v = proj(C_VB, MOBA_WIDTH)
    v_ref[...] = v
    zb_ref[...] = proj(C_ZB, MOBA_WIDTH)
    ba_ref[...] = proj(C_BA, LANES)
    if prompt_refs:
        kbf_ref, vbf_ref, kmean_ref = prompt_refs
        kbf_ref[...] = k.astype(BF16)
        vbf_ref[...] = v.astype(BF16)
        kmean_ref[0] = jnp.mean(k, axis=0, keepdims=True)


def _inproj(x2d, norm_w, w_p, cosf, sinf, *, prompt):
    m, d = x2d.shape
    tm = ROW_TILE
    assert m % tm == 0 and cosf.shape[0] % tm == 0
    n_pos = cosf.shape[0] // tm
    row = lambda i: (i, 0)
    full = lambda i: (0, 0)
    pos = (lambda i: (i % n_pos, 0)) if n_pos > 1 else full
    out_shape = [jax.ShapeDtypeStruct((m, GDN_CONV_CH), F32)]
    out_specs = [pl.BlockSpec((tm, GDN_CONV_CH), row)]
    for _ in range(5):
        out_shape.append(jax.ShapeDtypeStruct((m, MOBA_WIDTH), F32))
        out_specs.append(pl.BlockSpec((tm, MOBA_WIDTH), row))
    out_shape.append(jax.ShapeDtypeStruct((m, LANES), F32))
    out_specs.append(pl.BlockSpec((tm, LANES), row))
    if prompt:
        assert tm == MOBA_BLOCK
        for _ in range(2):
            out_shape.append(jax.ShapeDtypeStruct((m, MOBA_WIDTH), BF16))
            out_specs.append(pl.BlockSpec((tm, MOBA_WIDTH), row))
        out_shape.append(jax.ShapeDtypeStruct((m // tm, 1, MOBA_WIDTH), F32))
        out_specs.append(pl.BlockSpec((1, 1, MOBA_WIDTH), lambda i: (i, 0, 0)))
    return pl.pallas_call(
        _inproj_kernel,
        grid=(m // tm,),
        in_specs=[pl.BlockSpec((tm, d), row), pl.BlockSpec((1, d), full),
                  pl.BlockSpec((d, IN_COLS_PAD), full),
                  pl.BlockSpec((tm, HEAD_DIM), pos), pl.BlockSpec((tm, HEAD_DIM), pos)],
        out_specs=out_specs,
        out_shape=out_shape,
        compiler_params=pltpu.CompilerParams(dimension_semantics=("parallel",),
                                             vmem_limit_bytes=VMEM_LIMIT),
        name="inproj_prompt" if prompt else "inproj_sample",
    )(x2d, norm_w, w_p, cosf, sinf)


def _conv_silu_norm(c):
    c = _silu(c)
    qs, ks = [], []
    for h in range(GDN_HEADS):
        qh = c[:, h * HEAD_DIM:(h + 1) * HEAD_DIM]
        kh = c[:, GDN_QK + h * HEAD_DIM:GDN_QK + (h + 1) * HEAD_DIM]
        qs.append(qh * (lax.rsqrt(jnp.sum(qh * qh, axis=-1, keepdims=True) + NORM_EPS)
                        * (HEAD_DIM ** -0.5)))
        ks.append(kh * lax.rsqrt(jnp.sum(kh * kh, axis=-1, keepdims=True) + NORM_EPS))
    return qs, ks, c[:, 2 * GDN_QK:]


def _gdn_prep_kernel(x_ref, halo_ref, ba_ref, bat_ref, cw_ref, arow_ref, acol_ref,
                     n_ref, intra_ref, rhs_ref, qdec_ref, kdec_ref, glast_ref, buf_ref):
    i = pl.program_id(1)
    tt = x_ref.shape[1]
    c = GDN_CHUNK
    halo = jnp.where(i > 0, halo_ref[0], 0.0)
    buf_ref[0:SUBLANES, :] = halo
    buf_ref[SUBLANES:SUBLANES + tt, :] = x_ref[0]
    conv = buf_ref[SUBLANES - 3:SUBLANES - 3 + tt, :] * cw_ref[0:1, :]
    for w in range(1, GDN_CONV_W):
        conv = conv + buf_ref[SUBLANES - 3 + w:SUBLANES - 3 + w + tt, :] * cw_ref[w:w + 1, :]
    qs, ks, v = _conv_silu_norm(conv)

    ba = ba_ref[0]
    beta_full = jax.nn.sigmoid(ba)
    g_full = -jnp.exp(arow_ref[0:1, :]) * _softplus(ba + arow_ref[1:2, :])
    bat = bat_ref[0]
    gt_full = -jnp.exp(acol_ref[0]) * _softplus(bat + acol_ref[1])

    ri = lax.broadcasted_iota(jnp.int32, (c, c), 0)
    ci = lax.broadcasted_iota(jnp.int32, (c, c), 1)
    tril = ri >= ci
    strict = ri > ci
    ltri = jnp.where(tril, 1.0, 0.0).astype(F32)
    utri = jnp.where(ri <= ci, 1.0, 0.0).astype(F32)

    for cc in range(tt // c):
        rows = slice(cc * c, (cc + 1) * c)
        gcol_all = _dot_f32(ltri, g_full[rows, :])
        grow_all = _dot_f32(gt_full[:, rows], utri)
        for h in range(GDN_HEADS):
            gc_c = gcol_all[:, GDN_HEADS + h:GDN_HEADS + h + 1]
            gc_r = grow_all[GDN_HEADS + h:GDN_HEADS + h + 1, :]
            gamma = jnp.where(tril, jnp.exp(jnp.where(tril, gc_c - gc_r, 0.0)), 0.0)
            beta = beta_full[rows, h:h + 1]
            qh, kh = qs[h][rows, :], ks[h][rows, :]
            vh = v[rows, h * HEAD_DIM:(h + 1) * HEAD_DIM]
            kb = kh * beta
            k16 = kh.astype(BF16)
            n_ref[h, cc] = jnp.where(strict, _dot_nt(kb.astype(BF16), k16) * gamma, 0.0)
            intra_ref[h, cc] = jnp.where(tril, _dot_nt(qh.astype(BF16), k16) * gamma, 0.0)
            egc = jnp.exp(gc_c)
            rhs_ref[h, cc, :, 0:HEAD_DIM] = vh * beta
            rhs_ref[h, cc, :, HEAD_DIM:2 * HEAD_DIM] = kb * egc
            qdec_ref[h, cc] = qh * egc
            gl = gc_c[c - 1:c, :]
            kdec_ref[h, cc] = kh * jnp.exp(gl - gc_c)
            glast_ref[h, cc] = jnp.broadcast_to(jnp.exp(gl), (SUBLANES, LANES))


def _gdn_prep(qkv, ba, bat, conv_w, arow, acol):
    b, s, _ = qkv.shape
    tt = ROW_TILE
    c = GDN_CHUNK
    cpt = tt // c
    nc = s // c
    hb = tt // SUBLANES
    bh = b * GDN_HEADS

    def o(x):
        return (jax.ShapeDtypeStruct((bh, nc, c, x), F32),
                pl.BlockSpec((GDN_HEADS, cpt, c, x), lambda bi, i: (bi, i, 0, 0)))

    outs = [o(c), o(c), o(2 * HEAD_DIM), o(HEAD_DIM), o(HEAD_DIM)]
    outs.append((jax.ShapeDtypeStruct((bh, nc, SUBLANES, LANES), F32),
                 pl.BlockSpec((GDN_HEADS, cpt, SUBLANES, LANES), lambda bi, i: (bi, i, 0, 0))))
    return pl.pallas_call(
        _gdn_prep_kernel,
        grid=(b, s // tt),
        in_specs=[
            pl.BlockSpec((1, tt, GDN_CONV_CH), lambda bi, i: (bi, i, 0)),
            pl.BlockSpec((1, SUBLANES, GDN_CONV_CH), lambda bi, i: (bi, jnp.maximum(i * hb - 1, 0), 0)),
            pl.BlockSpec((1, tt, LANES), lambda bi, i: (bi, i, 0)),
            pl.BlockSpec((1, SUBLANES, tt), lambda bi, i: (bi, 0, i)),
            pl.BlockSpec((GDN_CONV_W, GDN_CONV_CH), lambda bi, i: (0, 0)),
            pl.BlockSpec((2, LANES), lambda bi, i: (0, 0)),
            pl.BlockSpec((2, SUBLANES, 1), lambda bi, i: (0, 0, 0)),
        ],
        out_specs=[x[1] for x in outs],
        out_shape=[x[0] for x in outs],
        scratch_shapes=[pltpu.VMEM((SUBLANES + tt, GDN_CONV_CH), F32)],
        compiler_params=pltpu.CompilerParams(dimension_semantics=("parallel", "parallel"),
                                             vmem_limit_bytes=VMEM_LIMIT),
        name="gdn_prep",
    )(qkv, qkv, ba, bat, conv_w, arow, acol)


def _tri_inv_kernel(n_ref, t_ref):
    c = n_ref.shape[0]
    col = lax.broadcasted_iota(jnp.int32, (c, LANES), 0)
    t_ref[...] = jnp.zeros(t_ref.shape, F32)

    def outer(i, carry):
        def inner(jg, acc):
            j0 = pl.multiple_of(jg * SUBLANES, SUBLANES)
            n_rows = n_ref[i, pl.ds(j0, SUBLANES), :]
            for r in range(SUBLANES):
                acc = acc - n_rows[r:r + 1, :] * t_ref[j0 + r]
            return acc
        groups = lax.shift_right_logical(i + (SUBLANES - 1), SUBLANES.bit_length() - 1)
        acc = lax.fori_loop(0, groups, inner, jnp.where(col == i, 1.0, 0.0).astype(F32))
        t_ref[i] = acc
        return carry

    lax.fori_loop(0, c, outer, 0)


def _tri_inv(nt):
    c, _, nmat = nt.shape
    assert nmat % LANES == 0
    spec = pl.BlockSpec((c, c, LANES), lambda g: (0, 0, g))
    return pl.pallas_call(
        _tri_inv_kernel,
        grid=(nmat // LANES,),
        in_specs=[spec],
        out_specs=spec,
        out_shape=jax.ShapeDtypeStruct(nt.shape, F32),
        compiler_params=pltpu.CompilerParams(dimension_semantics=("parallel",)),
        name="gdn_tri_inv",
    )(nt)


def _gdn_scan_kernel(t_ref, rhs_ref, qdec_ref, kdec_ref, intra_ref, glast_ref, o_ref, s_ref):
    i = pl.program_id(0)
    bh, ct, c, _ = t_ref.shape

    @pl.when(i == 0)
    def _():
        s_ref[...] = jnp.zeros(s_ref.shape, F32)

    heads = range(bh)
    for cc in range(ct):
        sol = [_dot3(t_ref[n, cc], rhs_ref[n, cc]) for n in heads]
        wqs = [_dot(jnp.concatenate([sol[n][:, HEAD_DIM:], qdec_ref[n, cc]], axis=0).astype(BF16),
                    s_ref[n].astype(BF16)) for n in heads]
        vn16 = [(sol[n][:, :HEAD_DIM] - wqs[n][:c]).astype(BF16) for n in heads]
        for n in heads:
            s_ref[n] = (s_ref[n] * glast_ref[n, cc, 0:1, :]
                        + _dot_tn(kdec_ref[n, cc].astype(BF16), vn16[n]))
        for n in heads:
            b, h = n // GDN_HEADS, n % GDN_HEADS
            o_ref[b, cc * c:(cc + 1) * c, h * HEAD_DIM:(h + 1) * HEAD_DIM] = (
                wqs[n][c:] + _dot(intra_ref[n, cc].astype(BF16), vn16[n]))


def _gdn_scan(t, rhs, qdec, kdec, intra, glast, b):
    bh, nc, c, _ = t.shape
    ct = 4
    assert nc % ct == 0

    def spec(x):
        return pl.BlockSpec((bh, ct, x.shape[2], x.shape[3]), lambda i: (0, i, 0, 0))

    return pl.pallas_call(
        _gdn_scan_kernel,
        grid=(nc // ct,),
        in_specs=[spec(t), spec(rhs), spec(qdec), spec(kdec), spec(intra), spec(glast)],
        out_specs=[pl.BlockSpec((b, ct * c, GDN_WIDTH), lambda i: (0, i, 0)),
                   pl.BlockSpec((bh, HEAD_DIM, HEAD_DIM), lambda i: (0, 0, 0))],
        out_shape=[jax.ShapeDtypeStruct((b, nc * c, GDN_WIDTH), F32),
                   jax.ShapeDtypeStruct((bh, HEAD_DIM, HEAD_DIM), F32)],
        compiler_params=pltpu.CompilerParams(dimension_semantics=("arbitrary",),
                                             vmem_limit_bytes=VMEM_LIMIT),
        name="gdn_scan",
    )(t, rhs, qdec, kdec, intra, glast)


MOBA_GROUP = 4
MOBA_SPLIT = 2
MOBA_QTILE = MOBA_BLOCK


def _moba_prompt_kernel(q_ref, k_ref, v_ref, kmean_ref, o_ref):
    i = pl.program_id(2)
    tq = q_ref.shape[1]
    blk = MOBA_BLOCK
    scale = HEAD_DIM ** -0.5
    q = q_ref[0]
    q16 = q.astype(BF16)
    kmean = jnp.concatenate(
        [kmean_ref[0], jnp.zeros((LANES - kmean_ref.shape[1], HEAD_DIM), F32)], axis=0)
    bpt = tq // blk
    blk_shift = blk.bit_length() - 1
    row_blk = i * bpt + lax.shift_right_logical(
        lax.broadcasted_iota(jnp.int32, (tq, LANES), 0), blk_shift)
    past = lax.broadcasted_iota(jnp.int32, (tq, LANES), 1) < row_blk
    member = _top3_members(_dot_nt_f32(q, kmean), past)
    q_aug = jnp.concatenate([q16, jnp.where(member, 0.0, MASK_BIAS).astype(BF16)], axis=1)
    grp = MOBA_GROUP * blk
    c_exp = scale * math.log2(math.e)
    tile_rows = 2 * SUBLANES
    tile_lane = lax.broadcasted_iota(jnp.int32, (tile_rows, LANES), 1)

    rq = lax.broadcasted_iota(jnp.int32, (blk, blk), 0)
    ck = lax.broadcasted_iota(jnp.int32, (blk, blk), 1)
    ms, ls, accs = [], [], []
    for r in range(bpt):
        start = pl.multiple_of((i * bpt + r) * blk, blk)
        s = _dot_nt(q16[r * blk:(r + 1) * blk, :], k_ref[0, pl.ds(start, blk), :])
        s = jnp.where(ck <= rq, s, -jnp.inf)
        m_r = jnp.max(s, axis=1, keepdims=True)
        p = jnp.exp2((s - m_r) * c_exp)
        ms.append(m_r)
        ls.append(jnp.sum(p, axis=1, keepdims=True))
        accs.append(_dot(p.astype(BF16), v_ref[0, pl.ds(start, blk), :]))
    m, l, acc = (jnp.concatenate(x, axis=0) for x in (ms, ls, accs))

    sub = grp // MOBA_SPLIT

    def body(j, carry):
        m, l, acc = carry
        st = pl.multiple_of(j * grp, grp)
        onehot = []
        for n in range(MOBA_GROUP):
            tile = jnp.where(tile_lane == j * MOBA_GROUP + n, 1.0, 0.0).astype(BF16)
            onehot += [tile] * (blk // tile_rows)
        k_aug = jnp.concatenate([k_ref[0, pl.ds(st, grp), :], jnp.concatenate(onehot, axis=0)], axis=1)
        raw = [_dot_nt(q_aug, k_aug[u * sub:(u + 1) * sub, :]) for u in range(MOBA_SPLIT)]
        for u in range(MOBA_SPLIT):
            m_new = jnp.maximum(m, jnp.max(raw[u], axis=1, keepdims=True))
            alpha = jnp.exp2((m - m_new) * c_exp)
            p = jnp.exp2((raw[u] - m_new) * c_exp)
            l = l * alpha + jnp.sum(p, axis=1, keepdims=True)
            acc = acc * alpha + _dot(p.astype(BF16), v_ref[0, pl.ds(st + u * sub, sub), :])
            m = m_new
        return m, l, acc

    n_groups = lax.shift_right_logical(i * bpt + (bpt - 1) + (MOBA_GROUP - 1),
                                       MOBA_GROUP.bit_length() - 1)
    m, l, acc = lax.fori_loop(0, n_groups, body, (m, l, acc))
    o_ref[0] = acc / l


def _moba_prompt(q, kbf, vbf, kmean):
    b, s, _ = q.shape
    tq = MOBA_QTILE
    nb = s // MOBA_BLOCK
    assert s % tq == 0 and tq % MOBA_BLOCK == 0 and nb <= LANES and nb % MOBA_GROUP == 0
    return pl.pallas_call(
        _moba_prompt_kernel,
        grid=(b, MOBA_HEADS, s // tq),
        in_specs=[pl.BlockSpec((1, tq, HEAD_DIM), lambda bi, h, i: (bi, i, h)),
                  pl.BlockSpec((1, s, HEAD_DIM), lambda bi, h, i: (bi, 0, h)),
                  pl.BlockSpec((1, s, HEAD_DIM), lambda bi, h, i: (bi, 0, h)),
                  pl.BlockSpec((1, nb, HEAD_DIM), lambda bi, h, i: (bi, 0, h))],
        out_specs=pl.BlockSpec((1, tq, HEAD_DIM), lambda bi, h, i: (bi, i, h)),
        out_shape=jax.ShapeDtypeStruct((b, s, MOBA_WIDTH), F32),
        compiler_params=pltpu.CompilerParams(
            dimension_semantics=("parallel", "parallel", "arbitrary"),
            vmem_limit_bytes=VMEM_LIMIT),
        name="moba_prompt",
    )(q, kbf, vbf, kmean)


def _gdn_sample_prep_kernel(x_ref, st_ref, ba_ref, cw_ref, arow_ref,
                            q_ref, k_ref, v_ref, beta_ref, dec_ref):
    t_len = x_ref.shape[0]
    ctx = GDN_CONV_W - 1

    def src(tt):
        return st_ref[tt + ctx] if tt < 0 else x_ref[tt]

    for t in range(t_len):
        conv = src(t - ctx) * cw_ref[0:1, :]
        for w in range(1, GDN_CONV_W):
            conv = conv + src(t - ctx + w) * cw_ref[w:w + 1, :]
        qs, ks, v = _conv_silu_norm(conv)
        q_ref[t] = jnp.concatenate(qs, axis=1)
        k_ref[t] = jnp.concatenate(ks, axis=1)
        v_ref[t] = v
        ba = ba_ref[t]
        beta_full = jax.nn.sigmoid(ba)
        dec_full = jnp.exp(-jnp.exp(arow_ref[0:1, :]) * _softplus(ba + arow_ref[1:2, :]))
        rows = ba.shape[0]
        beta_ref[t] = jnp.concatenate(
            [jnp.broadcast_to(beta_full[:, h:h + 1], (rows, HEAD_DIM)) for h in range(GDN_HEADS)], axis=1)
        dec_ref[t] = jnp.concatenate(
            [jnp.broadcast_to(dec_full[:, GDN_HEADS + h:GDN_HEADS + h + 1], (rows, HEAD_DIM))
             for h in range(GDN_HEADS)], axis=1)


def _gdn_sample_prep(x_t, st_t, ba_t, conv_w, arow):
    t_len, db, _ = x_t.shape
    bt = min(db, 32)
    assert db % bt == 0

    def spec(t, w):
        return pl.BlockSpec((t, bt, w), lambda i: (0, i, 0))

    out = jax.ShapeDtypeStruct((t_len, db, GDN_WIDTH), F32)
    return pl.pallas_call(
        _gdn_sample_prep_kernel,
        grid=(db // bt,),
        in_specs=[spec(t_len, GDN_CONV_CH), spec(GDN_CONV_W - 1, GDN_CONV_CH), spec(t_len, LANES),
                  pl.BlockSpec((GDN_CONV_W, GDN_CONV_CH), lambda i: (0, 0)),
                  pl.BlockSpec((2, LANES), lambda i: (0, 0))],
        out_specs=[spec(t_len, GDN_WIDTH)] * 5,
        out_shape=[out] * 5,
        compiler_params=pltpu.CompilerParams(dimension_semantics=("parallel",),
                                             vmem_limit_bytes=VMEM_LIMIT),
        name="gdn_sample_prep",
    )(x_t, st_t, ba_t, conv_w, arow)


def _gdn_sample_rec_kernel(s0_ref, qt_ref, kt_ref, v_ref, beta_ref, dec_ref, o_ref, s_ref):
    bt = s0_ref.shape[0]
    t_len = v_ref.shape[1]

    def per_batch(bi, carry):
        v, beta, dec = v_ref[bi], beta_ref[bi], dec_ref[bi]
        outs = []
        for h in range(GDN_HEADS):
            cols = slice(h * HEAD_DIM, (h + 1) * HEAD_DIM)
            s = s0_ref[bi, h]
            kt = kt_ref[bi, h]
            qt = qt_ref[bi, h]
            rows = []
            for t in range(t_len):
                s = s * dec[t:t + 1, cols]
                kcol = kt[:, t:t + 1]
                kv = jnp.sum(kcol * s, axis=0, keepdims=True)
                upd = (v[t:t + 1, cols] - kv) * beta[t:t + 1, cols]
                s = s + kcol * upd
                rows.append(jnp.sum(qt[:, t:t + 1] * s, axis=0, keepdims=True))
            s_ref[bi, h] = s
            outs.append(jnp.concatenate(rows, axis=0))
        o_ref[bi] = jnp.concatenate(outs, axis=1)
        return carry

    lax.fori_loop(0, bt, per_batch, 0)


def _gdn_sample_rec(s0, qt, kt, v, beta, dec):
    db, t_len, _ = v.shape
    bt = min(db, 8)
    assert db % bt == 0
    st_spec = pl.BlockSpec((bt, GDN_HEADS, HEAD_DIM, HEAD_DIM), lambda i: (i, 0, 0, 0))
    tr_spec = pl.BlockSpec((bt, GDN_HEADS, HEAD_DIM, t_len), lambda i: (i, 0, 0, 0))
    tok_spec = pl.BlockSpec((bt, t_len, GDN_WIDTH), lambda i: (i, 0, 0))
    return pl.pallas_call(
        _gdn_sample_rec_kernel,
        grid=(db // bt,),
        in_specs=[st_spec, tr_spec, tr_spec, tok_spec, tok_spec, tok_spec],
        out_specs=[tok_spec, st_spec],
        out_shape=[jax.ShapeDtypeStruct((db, t_len, GDN_WIDTH), F32),
                   jax.ShapeDtypeStruct(s0.shape, F32)],
        compiler_params=pltpu.CompilerParams(dimension_semantics=("parallel",),
                                             vmem_limit_bytes=VMEM_LIMIT),
        name="gdn_sample_rec",
    )(s0, qt, kt, v, beta, dec)


PAGES_PER_STEP = 16
PAGE_ROWS = PAGE_SIZE * MOBA_HEADS
PAGES_PER_BLOCK = MOBA_BLOCK // PAGE_SIZE
BLOCKS_PER_STEP = PAGES_PER_STEP // PAGES_PER_BLOCK
STEP_ROWS = PAGES_PER_STEP * PAGE_ROWS
HEAD_SHIFT = MOBA_HEADS.bit_length() - 1
assert 1 << HEAD_SHIFT == MOBA_HEADS and SUBLANES == 2 * MOBA_HEADS and PAGES_PER_BLOCK == 2


PAGE_RING = 3


def _moba_sample_kernel(pt_ref, q_ref, kn_ref, vn_ref, ck_ref, cv_ref, o_ref,
                        kwin_ref, vwin_ref, ksem, vsem,
                        kbuf_ref, vbuf_ref, kmean_ref, mem_ref, m_ref, l_ref, acc_ref):
    bp, g = pl.program_id(0), pl.program_id(1)
    n_batch = pl.num_programs(0) - 1
    n_steps = pl.num_programs(1)
    n_pages = n_steps * PAGES_PER_STEP
    lin = bp * n_steps + g

    def page_copies(step, ring_slot, stage):
        sb, sg = lax.div(step, n_steps), lax.rem(step, n_steps)
        if stage == 1:
            needed, row, cache, win, sem = sb < n_batch, sb, ck_ref, kwin_ref, ksem
        else:
            needed, row, cache, win, sem = sb > 0, sb - 1, cv_ref, vwin_ref, vsem
        base = jnp.where(needed, row, 0) * n_pages + sg * PAGES_PER_STEP
        return needed, [pltpu.make_async_copy(cache.at[pt_ref[base + pg]], win.at[ring_slot, pg],
                                              sem.at[ring_slot]) for pg in range(PAGES_PER_STEP)]

    def request(step, ring_slot):
        for stage in (1, 2):
            needed, copies = page_copies(step, ring_slot, stage)

            @pl.when(needed)
            def _():
                for c in copies:
                    c.start()

    ring_slot = lax.rem(lin, PAGE_RING)

    @pl.when(lin == 0)
    def _():
        for step in range(PAGE_RING - 1):
            request(jnp.int32(step), step)

    ahead = lin + (PAGE_RING - 1)

    @pl.when(ahead < (n_batch + 1) * n_steps)
    def _():
        request(ahead, lax.rem(ahead, PAGE_RING))

    for stage in (1, 2):
        needed, copies = page_copies(lin, ring_slot, stage)

        @pl.when(needed)
        def _():
            for c in copies:
                c.wait()

    rows = q_ref.shape[1]
    t_len = rows // MOBA_HEADS
    scale = HEAD_DIM ** -0.5
    t_shift = t_len.bit_length() - 1
    assert 1 << t_shift == t_len and kn_ref.shape[1] == rows
    step_lanes = BLOCKS_PER_STEP * MOBA_HEADS
    par = jnp.bitwise_and(bp, 1)
    row_i = lax.broadcasted_iota(jnp.int32, (rows, LANES), 0)
    lane_i = lax.broadcasted_iota(jnp.int32, (rows, LANES), 1)
    row_head = lax.shift_right_logical(row_i, t_shift)
    lane_head = jnp.bitwise_and(lane_i, MOBA_HEADS - 1)

    @pl.when(jnp.logical_and(bp == 0, g == 0))
    def _():
        kmean_ref[...] = jnp.zeros(kmean_ref.shape, F32)

    @pl.when(jnp.logical_and(bp > 0, g == 0))
    def _():
        q = q_ref[0]
        q16 = q.astype(BF16)
        n_cand = n_steps * step_lanes
        valid = jnp.logical_and(lane_head == row_head, lane_i < n_cand)
        member = _top3_members(_dot_nt_f32(q, kmean_ref[...]), valid)
        mem_ref[...] = jnp.where(member, 1.0, 0.0)
        pad = jnp.zeros((LANES - rows, HEAD_DIM), F32)
        kn = jnp.concatenate([kn_ref[0], pad], axis=0).astype(BF16)
        vn = jnp.concatenate([vn_ref[0], pad], axis=0).astype(BF16)
        s = _dot_nt(q16, kn) * scale
        ok = jnp.logical_and(lane_head == row_head, lane_i < rows)
        ok = jnp.logical_and(ok, lax.shift_right_logical(lane_i, HEAD_SHIFT)
                             <= jnp.bitwise_and(row_i, t_len - 1))
        s = jnp.where(ok, s, -jnp.inf)
        m = jnp.max(s, axis=1, keepdims=True)
        p = jnp.exp(s - m)
        m_ref[...] = jnp.broadcast_to(m, m_ref.shape)
        l_ref[...] = jnp.broadcast_to(jnp.sum(p, axis=1, keepdims=True), l_ref.shape)
        acc_ref[...] = _dot(p.astype(BF16), vn)

    @pl.when(bp > 0)
    def _():
        for pg in range(PAGES_PER_STEP):
            vbuf_ref[pg * PAGE_ROWS:(pg + 1) * PAGE_ROWS, :] = vwin_ref[ring_slot, pg].astype(BF16)
        member = mem_ref[...]
        blk_rows = PAGES_PER_BLOCK * PAGE_ROWS
        col_head = jnp.bitwise_and(lax.broadcasted_iota(jnp.int32, (rows, blk_rows), 1), MOBA_HEADS - 1)
        own_head = col_head == lax.shift_right_logical(
            lax.broadcasted_iota(jnp.int32, (rows, blk_rows), 0), t_shift)
        cand_blk = lax.shift_right_logical(lane_i, HEAD_SHIFT)
        bias = []
        for n in range(BLOCKS_PER_STEP):
            picked = jnp.sum(jnp.where(cand_blk == g * BLOCKS_PER_STEP + n, member, 0.0),
                             axis=1, keepdims=True) > 0.0
            bias.append(jnp.where(jnp.logical_and(own_head, picked), 0.0, -jnp.inf))
        s = _dot_nt(q_ref[0].astype(BF16), kbuf_ref[1 - par, g]) * scale + jnp.concatenate(bias, axis=1)
        m = m_ref[:, 0:1]
        m_new = jnp.maximum(m, jnp.max(s, axis=1, keepdims=True))
        alpha = jnp.exp(m - m_new)
        p = jnp.exp(s - m_new)
        l = l_ref[:, 0:1] * alpha + jnp.sum(p, axis=1, keepdims=True)
        acc = acc_ref[...] * alpha + _dot(p.astype(BF16), vbuf_ref[...])
        m_ref[...] = jnp.broadcast_to(m_new, m_ref.shape)
        l_ref[...] = jnp.broadcast_to(l, l_ref.shape)
        acc_ref[...] = acc

        @pl.when(g == n_steps - 1)
        def _():
            o_ref[0] = acc / l

    @pl.when(bp < n_batch)
    def _():
        low = lax.broadcasted_iota(jnp.int32, (SUBLANES, HEAD_DIM), 0) < MOBA_HEADS

        def block_mean(n):
            tot = None
            for r in range(PAGES_PER_BLOCK):
                pg = n * PAGES_PER_BLOCK + r
                page = kwin_ref[ring_slot, pg]
                kbuf_ref[par, g, pg * PAGE_ROWS:(pg + 1) * PAGE_ROWS, :] = page.astype(BF16)
                part = jnp.sum(page.reshape(PAGE_ROWS // SUBLANES, SUBLANES, HEAD_DIM), axis=0)
                tot = part if tot is None else tot + part
            return (tot + pltpu.roll(tot, MOBA_HEADS, 0)) * (1.0 / MOBA_BLOCK)

        tiles = [jnp.where(low, block_mean(2 * n), block_mean(2 * n + 1))
                 for n in range(BLOCKS_PER_STEP // 2)]
        kmean_ref[pl.ds(pl.multiple_of(g * step_lanes, step_lanes), step_lanes), :] = (
            jnp.concatenate(tiles, axis=0))


def _moba_sample(page_flat, q2, kn2, vn2, cache_k, cache_v, n_pages):
    db, rows, _ = q2.shape
    assert n_pages % PAGES_PER_STEP == 0
    n_steps = n_pages // PAGES_PER_STEP
    n_cand = n_steps * BLOCKS_PER_STEP * MOBA_HEADS
    assert rows <= LANES and n_cand <= LANES
    tok = pl.BlockSpec((1, rows, HEAD_DIM), lambda bp, g, pt: (jnp.maximum(bp - 1, 0), 0, 0))
    window = pltpu.VMEM((PAGE_RING, PAGES_PER_STEP, PAGE_ROWS, HEAD_DIM), F32)
    return pl.pallas_call(
        _moba_sample_kernel,
        grid_spec=pltpu.PrefetchScalarGridSpec(
            num_scalar_prefetch=1,
            grid=(db + 1, n_steps),
            in_specs=[tok, tok, tok, pl.BlockSpec(memory_space=pl.ANY),
                      pl.BlockSpec(memory_space=pl.ANY)],
            out_specs=tok,
            scratch_shapes=[window, window,
                            pltpu.SemaphoreType.DMA((PAGE_RING,)), pltpu.SemaphoreType.DMA((PAGE_RING,)),
                            pltpu.VMEM((2, n_steps, STEP_ROWS, HEAD_DIM), BF16),
                            pltpu.VMEM((STEP_ROWS, HEAD_DIM), BF16),
                            pltpu.VMEM((LANES, HEAD_DIM), F32),
                            pltpu.VMEM((rows, LANES), F32),
                            pltpu.VMEM((rows, LANES), F32), pltpu.VMEM((rows, LANES), F32),
                            pltpu.VMEM((rows, HEAD_DIM), F32)],
        ),
        out_shape=jax.ShapeDtypeStruct((db, rows, HEAD_DIM), F32),
        compiler_params=pltpu.CompilerParams(dimension_semantics=("arbitrary", "arbitrary"),
                                             vmem_limit_bytes=VMEM_LIMIT),
        name="moba_sample",
    )(page_flat, q2, kn2, vn2, cache_k, cache_v)


def _merge_kernel(og_ref, za_ref, om_ref, zb_ref, x_ref, gw_ref, wo_ref, nf_ref, y_ref):
    og = og_ref[...]
    parts = []
    for h in range(GDN_HEADS):
        oh = og[:, h * HEAD_DIM:(h + 1) * HEAD_DIM]
        parts.append(oh * lax.rsqrt(jnp.mean(oh * oh, axis=-1, keepdims=True) + NORM_EPS) * gw_ref[...])
    ga = jnp.concatenate(parts, axis=1) * _silu(za_ref[...])
    gb = om_ref[...] * _silu(zb_ref[...])
    cat = jnp.concatenate([ga, gb], axis=1).astype(BF16)
    hp = x_ref[...] + _dot(cat, wo_ref[...])
    y_ref[...] = hp * lax.rsqrt(jnp.mean(hp * hp, axis=-1, keepdims=True) + NORM_EPS) * nf_ref[...]


def _merge(og, za, om, zb, x2d, gdn_norm_w, w_out16, norm_f_w):
    m, d = x2d.shape
    tm = ROW_TILE
    assert m % tm == 0
    row = lambda i: (i, 0)
    full = lambda i: (0, 0)
    half = pl.BlockSpec((tm, GDN_WIDTH), row)
    return pl.pallas_call(
        _merge_kernel,
        grid=(m // tm,),
        in_specs=[half, half, half, half, pl.BlockSpec((tm, d), row),
                  pl.BlockSpec((1, HEAD_DIM), full), pl.BlockSpec(w_out16.shape, full),
                  pl.BlockSpec((1, d), full)],
        out_specs=pl.BlockSpec((tm, d), row),
        out_shape=jax.ShapeDtypeStruct((m, d), F32),
        compiler_params=pltpu.CompilerParams(dimension_semantics=("parallel",),
                                             vmem_limit_bytes=VMEM_LIMIT),
        name="merge_out",
    )(og, za, om, zb, x2d, gdn_norm_w, w_out16, norm_f_w)


def _rope_tables(pos):
    half = HEAD_DIM // 2
    inv_freq = ROPE_THETA ** (-jnp.arange(half, dtype=F32) / half)
    ang = pos.astype(F32)[:, None] * inv_freq[None, :]
    cos, sin = jnp.cos(ang), jnp.sin(ang)
    return jnp.concatenate([cos, cos], axis=1), jnp.concatenate([-sin, sin], axis=1)


def _relayout_w_in(w):
    o = 0
    parts = {}
    for name, n in (("qkv", GDN_CONV_CH), ("za", GDN_WIDTH), ("b", GDN_HEADS), ("a", GDN_HEADS),
                    ("qb", MOBA_WIDTH), ("kb", MOBA_WIDTH), ("vb", MOBA_WIDTH), ("zb", MOBA_WIDTH)):
        parts[name] = w[:, o:o + n]
        o += n
    pad = jnp.zeros((w.shape[0], LANES - 2 * GDN_HEADS), w.dtype)
    return jnp.concatenate([parts["qkv"], parts["za"], parts["qb"], parts["kb"], parts["vb"],
                            parts["zb"], parts["b"], parts["a"], pad], axis=1).astype(BF16)


def kernel(x_prompt, x_sample, cache_k, cache_v, state_gdn, state_conv, page_table, norm_in_w, w_in,
           conv_w, a_log, dt_bias, gdn_norm_w, w_out, norm_f_w):
    b, s, d = x_prompt.shape
    db, t_len, _ = x_sample.shape
    depth = w_in.shape[0]
    assert depth == 1
    n_pages = page_table.shape[1]
    past_len = n_pages * PAGE_SIZE
    assert past_len % MOBA_BLOCK == 0
    assert t_len <= PAGE_SIZE and t_len >= GDN_CONV_W - 1 and (db * t_len) % ROW_TILE == 0
    assert ROW_TILE % t_len == 0

    w_p = _relayout_w_in(w_in[0])
    w_o = w_out[0].astype(BF16)
    nw = norm_in_w[0][None, :]
    cw = conv_w[0]
    zeros4 = jnp.zeros((GDN_HEADS,), F32)
    lane_pad = jnp.zeros((LANES - 2 * GDN_HEADS,), F32)
    arow = jnp.stack([jnp.concatenate([zeros4, a_log[0], lane_pad]),
                      jnp.concatenate([zeros4, dt_bias[0], lane_pad])])
    acol = jnp.stack([jnp.concatenate([zeros4, a_log[0]]),
                      jnp.concatenate([zeros4, dt_bias[0]])])[:, :, None]
    gw = gdn_norm_w[0][None, :]
    nf = norm_f_w[None, :]

    cos_p, sin_p = _rope_tables(jnp.arange(s, dtype=jnp.int32))
    (qkv_p, za_p, q_p, k_p, v_p, zb_p, ba_p, kbf_p, vbf_p, kmean_p) = _inproj(
        x_prompt.reshape(b * s, d), nw, w_p, cos_p, sin_p, prompt=True)
    qkv_p3 = qkv_p.reshape(b, s, GDN_CONV_CH)
    ba_p3 = ba_p.reshape(b, s, LANES)
    bat_p = jnp.transpose(ba_p3[:, :, :SUBLANES], (0, 2, 1))
    n_m, intra, rhs, qdec, kdec, glast = _gdn_prep(qkv_p3, ba_p3, bat_p, cw, arow, acol)
    bh, nc, c, _ = n_m.shape
    t_m = _tri_inv(jnp.transpose(n_m.reshape(bh * nc, c, c), (1, 2, 0)))
    t_m = jnp.transpose(t_m, (2, 0, 1)).reshape(bh, nc, c, c)
    o_gdn_p, s_gdn_p = _gdn_scan(t_m, rhs, qdec, kdec, intra, glast, b)
    o_moba_p = _moba_prompt(q_p.reshape(b, s, MOBA_WIDTH), kbf_p.reshape(b, s, MOBA_WIDTH),
                            vbf_p.reshape(b, s, MOBA_WIDTH),
                            kmean_p.reshape(b, s // MOBA_BLOCK, MOBA_WIDTH))
    y_p = _merge(o_gdn_p.reshape(b * s, GDN_WIDTH), za_p, o_moba_p.reshape(b * s, MOBA_WIDTH), zb_p,
                 x_prompt.reshape(b * s, d), gw, w_o, nf)

    cos_s, sin_s = _rope_tables(past_len + jnp.arange(t_len, dtype=jnp.int32))
    reps = ROW_TILE // t_len
    (qkv_s, za_s, q_s, k_s, v_s, zb_s, ba_s) = _inproj(
        x_sample.reshape(db * t_len, d), nw, w_p, jnp.tile(cos_s, (reps, 1)), jnp.tile(sin_s, (reps, 1)),
        prompt=False)
    qkv_s3 = qkv_s.reshape(db, t_len, GDN_CONV_CH)
    tb = lambda a: jnp.transpose(a, (1, 0, 2))
    qg, kg, vg, beta_t, dec_t = _gdn_sample_prep(
        tb(qkv_s3), tb(state_conv[0]), tb(ba_s.reshape(db, t_len, LANES)), cw, arow)
    to_cols = lambda a: jnp.transpose(a.reshape(t_len, db, GDN_HEADS, HEAD_DIM), (1, 2, 3, 0))
    o_gdn_s, s_gdn_s = _gdn_sample_rec(state_gdn[0], to_cols(qg), to_cols(kg), tb(vg), tb(beta_t),
                                       tb(dec_t))

    page_flat = page_table.reshape(-1).astype(jnp.int32)
    ck = cache_k.reshape(cache_k.shape[1], PAGE_ROWS, HEAD_DIM)
    cv = cache_v.reshape(cache_v.shape[1], PAGE_ROWS, HEAD_DIM)
    rows_s = t_len * MOBA_HEADS
    q2 = jnp.transpose(q_s.reshape(db, t_len, MOBA_HEADS, HEAD_DIM), (0, 2, 1, 3))
    o2 = _moba_sample(page_flat, q2.reshape(db, rows_s, HEAD_DIM), k_s.reshape(db, rows_s, HEAD_DIM),
                      v_s.reshape(db, rows_s, HEAD_DIM), ck, cv, n_pages)
    o_moba_s = jnp.transpose(o2.reshape(db, MOBA_HEADS, t_len, HEAD_DIM), (0, 2, 1, 3))
    y_s = _merge(o_gdn_s.reshape(db * t_len, GDN_WIDTH), za_s, o_moba_s.reshape(db * t_len, MOBA_WIDTH),
                 zb_s, x_sample.reshape(db * t_len, d), gw, w_o, nf)

    ctx = GDN_CONV_W - 1
    heads = lambda a, n, t: a.reshape(1, n, t, MOBA_HEADS, HEAD_DIM)
    return (y_p.reshape(b, s, d), y_s.reshape(db, t_len, d),
            heads(k_p, b, s), heads(v_p, b, s),
            s_gdn_p.reshape(1, b, GDN_HEADS, HEAD_DIM, HEAD_DIM),
            qkv_p3[:, s - ctx:, :][None],
            heads(k_s, db, t_len), heads(v_s, db, t_len),
            s_gdn_s[None],
            qkv_s3[:, t_len - ctx:, :][None])
```

```python
import functools
import math

import jax
import jax.numpy as jnp
from jax import lax
from jax.experimental import pallas as pl
from jax.experimental.pallas import tpu as pltpu

F32 = jnp.float32
BF16 = jnp.bfloat16

HEAD_DIM = 128
GDN_HEADS = 4
MOBA_HEADS = 4
GDN_QK = GDN_HEADS * HEAD_DIM
GDN_WIDTH = GDN_HEADS * HEAD_DIM
GDN_CONV_W = 4
GDN_CONV_CH = 2 * GDN_QK + GDN_WIDTH
GDN_CHUNK = 64
MOBA_WIDTH = MOBA_HEADS * HEAD_DIM
MOBA_BLOCK = 256
MOBA_TOPK = 3
PAGE_SIZE = 128
ROPE_THETA = 10000.0
NORM_EPS = 1e-6
LANES = 128
SUBLANES = 8
MASK_BIAS = -1e30

C_QKV = 0
C_ZA = GDN_CONV_CH
C_QB = C_ZA + GDN_WIDTH
C_KB = C_QB + MOBA_WIDTH
C_VB = C_KB + MOBA_WIDTH
C_ZB = C_VB + MOBA_WIDTH
C_BA = C_ZB + MOBA_WIDTH
IN_COLS_PAD = C_BA + LANES

ROW_TILE = 256
VMEM_LIMIT = 56 * 1024 * 1024


def _dot(a, b):
    return jnp.dot(a, b, preferred_element_type=F32)


def _dot_nt(a, b):
    return lax.dot_general(a, b, (((1,), (1,)), ((), ())), preferred_element_type=F32)


def _dot_tn(a, b):
    return lax.dot_general(a, b, (((0,), (0,)), ((), ())), preferred_element_type=F32)


def _dot_nt_f32(a, b):
    return lax.dot_general(a, b, (((1,), (1,)), ((), ())), preferred_element_type=F32,
                           precision=lax.Precision.HIGHEST)


def _dot_f32(a, b):
    return jnp.dot(a, b, preferred_element_type=F32, precision=lax.Precision.HIGHEST)


def _split_bf16(a):
    hi = a.astype(BF16)
    lo = (a - hi.astype(F32)).astype(BF16)
    return hi, lo


def _dot3(a, b):
    ah, al = _split_bf16(a)
    bh, bl = _split_bf16(b)
    return _dot(ah, bh) + (_dot(ah, bl) + _dot(al, bh))


def _dot3_nt(a, b):
    ah, al = _split_bf16(a)
    bh, bl = _split_bf16(b)
    return _dot_nt(ah, bh) + (_dot_nt(ah, bl) + _dot_nt(al, bh))


def _silu(x):
    return x * jax.nn.sigmoid(x)


def _softplus(x):
    return jnp.maximum(x, 0.0) + jnp.log1p(jnp.exp(-jnp.abs(x)))


def _top3_members(gate, valid, axis=1):
    cand = lax.broadcasted_iota(jnp.int32, gate.shape, axis).astype(F32)
    g = jnp.where(valid, gate, -jnp.inf)
    member = jnp.zeros(gate.shape, dtype=jnp.bool_)
    for _ in range(MOBA_TOPK):
        m = jnp.max(g, axis=axis, keepdims=True)
        idx = jnp.min(jnp.where(g == m, cand, float(LANES)), axis=axis, keepdims=True)
        pick = cand == idx
        member = jnp.logical_or(member, pick)
        g = jnp.where(pick, -jnp.inf, g)
    return jnp.logical_and(member, valid)


def _inproj_kernel(x_ref, nw_ref, w_ref, cos_ref, sin_ref,
                   qkv_ref, za_ref, q_ref, k_ref, v_ref, zb_ref, ba_ref, *prompt_refs):
    x = x_ref[...]
    ms = jnp.mean(x * x, axis=-1, keepdims=True)
    hb = (x * lax.rsqrt(ms + NORM_EPS) * nw_ref[...]).astype(BF16)

    def proj(c0, n):
        return _dot(hb, w_ref[:, c0:c0 + n])

    def rope(t):
        parts = []
        for h in range(MOBA_HEADS):
            th = t[:, h * HEAD_DIM:(h + 1) * HEAD_DIM]
            parts.append(th * cos_ref[...] + pltpu.roll(th, HEAD_DIM // 2, 1) * sin_ref[...])
        return jnp.concatenate(parts, axis=1)

    for j in range(GDN_CONV_CH // GDN_QK):
        qkv_ref[:, j * GDN_QK:(j + 1) * GDN_QK] = proj(C_QKV + j * GDN_QK, GDN_QK)
    za_ref[...] = proj(C_ZA, GDN_WIDTH).astype(za_ref.dtype)
    q_ref[...] = rope(proj(C_QB, MOBA_WIDTH))
    k = rope(proj(C_KB, MOBA_WIDTH))
    k_ref[...] = k
    v = proj(C_VB, MOBA_WIDTH)
    v_ref[...] = v
    zb_ref[...] = proj(C_ZB, MOBA_WIDTH).astype(zb_ref.dtype)
    ba_ref[...] = proj(C_BA, LANES)
    if prompt_refs:
        kbf_ref, vbf_ref, kmean_ref = prompt_refs
        kbf_ref[...] = k.astype(BF16)
        vbf_ref[...] = v.astype(BF16)
        kmean_ref[0] = jnp.mean(k, axis=0, keepdims=True)


def _inproj(x2d, norm_w, w_p, cosf, sinf, *, prompt):
    m, d = x2d.shape
    tm = ROW_TILE
    assert m % tm == 0 and cosf.shape[0] % tm == 0
    n_pos = cosf.shape[0] // tm
    row = lambda i: (i, 0)
    full = lambda i: (0, 0)
    pos = (lambda i: (i % n_pos, 0)) if n_pos > 1 else full
    out_shape = [jax.ShapeDtypeStruct((m, GDN_CONV_CH), F32)]
    out_specs = [pl.BlockSpec((tm, GDN_CONV_CH), row)]
    for dt in (BF16, F32, F32, F32, BF16):
        out_shape.append(jax.ShapeDtypeStruct((m, MOBA_WIDTH), dt))
        out_specs.append(pl.BlockSpec((tm, MOBA_WIDTH), row))
    out_shape.append(jax.ShapeDtypeStruct((m, LANES), F32))
    out_specs.append(pl.BlockSpec((tm, LANES), row))
    if prompt:
        assert tm == MOBA_BLOCK
        for _ in range(2):
            out_shape.append(jax.ShapeDtypeStruct((m, MOBA_WIDTH), BF16))
            out_specs.append(pl.BlockSpec((tm, MOBA_WIDTH), row))
        out_shape.append(jax.ShapeDtypeStruct((m // tm, 1, MOBA_WIDTH), F32))
        out_specs.append(pl.BlockSpec((1, 1, MOBA_WIDTH), lambda i: (i, 0, 0)))
    return pl.pallas_call(
        _inproj_kernel,
        grid=(m // tm,),
        in_specs=[pl.BlockSpec((tm, d), row), pl.BlockSpec((1, d), full),
                  pl.BlockSpec((d, IN_COLS_PAD), full),
                  pl.BlockSpec((tm, HEAD_DIM), pos), pl.BlockSpec((tm, HEAD_DIM), pos)],
        out_specs=out_specs,
        out_shape=out_shape,
        compiler_params=pltpu.CompilerParams(dimension_semantics=("parallel",),
                                             vmem_limit_bytes=VMEM_LIMIT),
        name="inproj_prompt" if prompt else "inproj_sample",
    )(x2d, norm_w, w_p, cosf, sinf)


def _conv_silu_norm(c):
    c = _silu(c)
    qs, ks = [], []
    for h in range(GDN_HEADS):
        qh = c[:, h * HEAD_DIM:(h + 1) * HEAD_DIM]
        kh = c[:, GDN_QK + h * HEAD_DIM:GDN_QK + (h + 1) * HEAD_DIM]
        qs.append(qh * (lax.rsqrt(jnp.sum(qh * qh, axis=-1, keepdims=True) + NORM_EPS)
                        * (HEAD_DIM ** -0.5)))
        ks.append(kh * lax.rsqrt(jnp.sum(kh * kh, axis=-1, keepdims=True) + NORM_EPS))
    return qs, ks, c[:, 2 * GDN_QK:]


def _gdn_prep_kernel(x_ref, halo_ref, ba_ref, bat_ref, cw_ref, arow_ref, acol_ref,
                     n_ref, intra_ref, rhs_ref, qdec_ref, kdec_ref, glast_ref, buf_ref):
    i = pl.program_id(1)
    tt = x_ref.shape[1]
    c = GDN_CHUNK
    halo = jnp.where(i > 0, halo_ref[0], 0.0)
    buf_ref[0:SUBLANES, :] = halo
    buf_ref[SUBLANES:SUBLANES + tt, :] = x_ref[0]
    conv = buf_ref[SUBLANES - 3:SUBLANES - 3 + tt, :] * cw_ref[0:1, :]
    for w in range(1, GDN_CONV_W):
        conv = conv + buf_ref[SUBLANES - 3 + w:SUBLANES - 3 + w + tt, :] * cw_ref[w:w + 1, :]
    qs, ks, v = _conv_silu_norm(conv)

    ba = ba_ref[0]
    beta_full = jax.nn.sigmoid(ba)
    g_full = -jnp.exp(arow_ref[0:1, :]) * _softplus(ba + arow_ref[1:2, :])
    bat = bat_ref[0]
    gt_full = -jnp.exp(acol_ref[0]) * _softplus(bat + acol_ref[1])

    ri = lax.broadcasted_iota(jnp.int32, (c, c), 0)
    ci = lax.broadcasted_iota(jnp.int32, (c, c), 1)
    tril = ri >= ci
    strict = ri > ci
    ltri = jnp.where(tril, 1.0, 0.0).astype(F32)
    utri = jnp.where(ri <= ci, 1.0, 0.0).astype(F32)

    for cc in range(tt // c):
        rows = slice(cc * c, (cc + 1) * c)
        gcol_all = _dot_f32(ltri, g_full[rows, :])
        grow_all = _dot_f32(gt_full[:, rows], utri)
        for h in range(GDN_HEADS):
            gc_c = gcol_all[:, GDN_HEADS + h:GDN_HEADS + h + 1]
            gc_r = grow_all[GDN_HEADS + h:GDN_HEADS + h + 1, :]
            gamma = jnp.where(tril, jnp.exp(jnp.where(tril, gc_c - gc_r, 0.0)), 0.0)
            beta = beta_full[rows, h:h + 1]
            qh, kh = qs[h][rows, :], ks[h][rows, :]
            vh = v[rows, h * HEAD_DIM:(h + 1) * HEAD_DIM]
            kb = kh * beta
            k16 = kh.astype(BF16)
            n_ref[h, cc] = jnp.where(strict, _dot_nt(kb.astype(BF16), k16) * gamma, 0.0)
            intra_ref[h, cc] = jnp.where(tril, _dot_nt(qh.astype(BF16), k16) * gamma, 0.0)
            egc = jnp.exp(gc_c)
            rhs_ref[h, cc, :, 0:HEAD_DIM] = vh * beta
            rhs_ref[h, cc, :, HEAD_DIM:2 * HEAD_DIM] = kb * egc
            qdec_ref[h, cc] = qh * egc
            gl = gc_c[c - 1:c, :]
            kdec_ref[h, cc] = kh * jnp.exp(gl - gc_c)
            glast_ref[h, cc] = jnp.broadcast_to(jnp.exp(gl), (SUBLANES, LANES))


def _gdn_prep(qkv, ba, bat, conv_w, arow, acol):
    b, s, _ = qkv.shape
    tt = ROW_TILE
    c = GDN_CHUNK
    cpt = tt // c
    nc = s // c
    hb = tt // SUBLANES
    bh = b * GDN_HEADS

    def o(x):
        return (jax.ShapeDtypeStruct((bh, nc, c, x), F32),
                pl.BlockSpec((GDN_HEADS, cpt, c, x), lambda bi, i: (bi, i, 0, 0)))

    outs = [o(c), o(c), o(2 * HEAD_DIM), o(HEAD_DIM), o(HEAD_DIM)]
    outs.append((jax.ShapeDtypeStruct((bh, nc, SUBLANES, LANES), F32),
                 pl.BlockSpec((GDN_HEADS, cpt, SUBLANES, LANES), lambda bi, i: (bi, i, 0, 0))))
    return pl.pallas_call(
        _gdn_prep_kernel,
        grid=(b, s // tt),
        in_specs=[
            pl.BlockSpec((1, tt, GDN_CONV_CH), lambda bi, i: (bi, i, 0)),
            pl.BlockSpec((1, SUBLANES, GDN_CONV_CH), lambda bi, i: (bi, jnp.maximum(i * hb - 1, 0), 0)),
            pl.BlockSpec((1, tt, LANES), lambda bi, i: (bi, i, 0)),
            pl.BlockSpec((1, SUBLANES, tt), lambda bi, i: (bi, 0, i)),
            pl.BlockSpec((GDN_CONV_W, GDN_CONV_CH), lambda bi, i: (0, 0)),
            pl.BlockSpec((2, LANES), lambda bi, i: (0, 0)),
            pl.BlockSpec((2, SUBLANES, 1), lambda bi, i: (0, 0, 0)),
        ],
        out_specs=[x[1] for x in outs],
        out_shape=[x[0] for x in outs],
        scratch_shapes=[pltpu.VMEM((SUBLANES + tt, GDN_CONV_CH), F32)],
        compiler_params=pltpu.CompilerParams(dimension_semantics=("parallel", "parallel"),
                                             vmem_limit_bytes=VMEM_LIMIT),
        name="gdn_prep",
    )(qkv, qkv, ba, bat, conv_w, arow, acol)


def _tri_inv_kernel(n_ref, t_ref):
    c = n_ref.shape[0]
    col = lax.broadcasted_iota(jnp.int32, (c, LANES), 0)
    t_ref[...] = jnp.zeros(t_ref.shape, F32)

    def outer(i, carry):
        def inner(jg, acc):
            j0 = pl.multiple_of(jg * SUBLANES, SUBLANES)
            n_rows = n_ref[i, pl.ds(j0, SUBLANES), :]
            for r in range(SUBLANES):
                acc = acc - n_rows[r:r + 1, :] * t_ref[j0 + r]
            return acc
        groups = lax.shift_right_logical(i + (SUBLANES - 1), SUBLANES.bit_length() - 1)
        acc = lax.fori_loop(0, groups, inner, jnp.where(col == i, 1.0, 0.0).astype(F32))
        t_ref[i] = acc
        return carry

    lax.fori_loop(0, c, outer, 0)


def _tri_inv(nt):
    c, _, nmat = nt.shape
    assert nmat % LANES == 0
    spec = pl.BlockSpec((c, c, LANES), lambda g: (0, 0, g))
    return pl.pallas_call(
        _tri_inv_kernel,
        grid=(nmat // LANES,),
        in_specs=[spec],
        out_specs=spec,
        out_shape=jax.ShapeDtypeStruct(nt.shape, F32),
        compiler_params=pltpu.CompilerParams(dimension_semantics=("parallel",)),
        name="gdn_tri_inv",
    )(nt)


def _gdn_scan_kernel(t_ref, rhs_ref, qdec_ref, kdec_ref, intra_ref, glast_ref, o_ref, s_ref):
    i = pl.program_id(0)
    bh, ct, c, _ = t_ref.shape

    @pl.when(i == 0)
    def _():
        s_ref[...] = jnp.zeros(s_ref.shape, F32)

    heads = range(bh)
    for cc in range(ct):
        sol = [_dot3(t_ref[n, cc], rhs_ref[n, cc]) for n in heads]
        wqs = [_dot(jnp.concatenate([sol[n][:, HEAD_DIM:], qdec_ref[n, cc]], axis=0).astype(BF16),
                    s_ref[n].astype(BF16)) for n in heads]
        vn16 = [(sol[n][:, :HEAD_DIM] - wqs[n][:c]).astype(BF16) for n in heads]
        for n in heads:
            s_ref[n] = (s_ref[n] * glast_ref[n, cc, 0:1, :]
                        + _dot_tn(kdec_ref[n, cc].astype(BF16), vn16[n]))
        for n in heads:
            b, h = n // GDN_HEADS, n % GDN_HEADS
            o_ref[b, cc * c:(cc + 1) * c, h * HEAD_DIM:(h + 1) * HEAD_DIM] = (
                wqs[n][c:] + _dot(intra_ref[n, cc].astype(BF16), vn16[n])).astype(o_ref.dtype)


def _gdn_scan(t, rhs, qdec, kdec, intra, glast, b):
    bh, nc, c, _ = t.shape
    ct = 4
    assert nc % ct == 0

    def spec(x):
        return pl.BlockSpec((bh, ct, x.shape[2], x.shape[3]), lambda i: (0, i, 0, 0))

    return pl.pallas_call(
        _gdn_scan_kernel,
        grid=(nc // ct,),
        in_specs=[spec(t), spec(rhs), spec(qdec), spec(kdec), spec(intra), spec(glast)],
        out_specs=[pl.BlockSpec((b, ct * c, GDN_WIDTH), lambda i: (0, i, 0)),
                   pl.BlockSpec((bh, HEAD_DIM, HEAD_DIM), lambda i: (0, 0, 0))],
        out_shape=[jax.ShapeDtypeStruct((b, nc * c, GDN_WIDTH), BF16),
                   jax.ShapeDtypeStruct((bh, HEAD_DIM, HEAD_DIM), F32)],
        compiler_params=pltpu.CompilerParams(dimension_semantics=("arbitrary",),
                                             vmem_limit_bytes=VMEM_LIMIT),
        name="gdn_scan",
    )(t, rhs, qdec, kdec, intra, glast)


MOBA_GROUP = 4
MOBA_SPLIT = 2
MOBA_QTILE = MOBA_BLOCK


def _moba_prompt_kernel(q_ref, k_ref, v_ref, kmean_ref, o_ref):
    i = pl.program_id(2)
    tq = q_ref.shape[1]
    blk = MOBA_BLOCK
    scale = HEAD_DIM ** -0.5
    q = q_ref[0]
    q16 = q.astype(BF16)
    kmean = jnp.concatenate(
        [kmean_ref[0], jnp.zeros((LANES - kmean_ref.shape[1], HEAD_DIM), F32)], axis=0)
    bpt = tq // blk
    blk_shift = blk.bit_length() - 1
    own_blk = i * bpt + lax.shift_right_logical(
        lax.broadcasted_iota(jnp.int32, (LANES, tq), 1), blk_shift)
    past = lax.broadcasted_iota(jnp.int32, (LANES, tq), 0) < own_blk
    member = _top3_members(_dot3_nt(kmean, q), past, axis=0)
    bias = jnp.transpose(jnp.where(member, 0.0, MASK_BIAS))
    q_aug = jnp.concatenate([q16, bias.astype(BF16)], axis=1)
    grp = MOBA_GROUP * blk
    c_exp = scale * math.log2(math.e)
    tile_rows = 2 * SUBLANES
    tile_lane = lax.broadcasted_iota(jnp.int32, (tile_rows, LANES), 1)

    rq = lax.broadcasted_iota(jnp.int32, (blk, blk), 0)
    ck = lax.broadcasted_iota(jnp.int32, (blk, blk), 1)
    ms, ls, accs = [], [], []
    for r in range(bpt):
        start = pl.multiple_of((i * bpt + r) * blk, blk)
        s = _dot_nt(q16[r * blk:(r + 1) * blk, :], k_ref[0, pl.ds(start, blk), :])
        s = jnp.where(ck <= rq, s, -jnp.inf)
        m_r = jnp.max(s, axis=1, keepdims=True)
        p = jnp.exp2((s - m_r) * c_exp)
        ms.append(m_r)
        ls.append(jnp.sum(p, axis=1, keepdims=True))
        accs.append(_dot(p.astype(BF16), v_ref[0, pl.ds(start, blk), :]))
    m, l, acc = (jnp.concatenate(x, axis=0) for x in (ms, ls, accs))

    sub = grp // MOBA_SPLIT

    def body(j, carry):
        m, l, acc = carry
        st = pl.multiple_of(j * grp, grp)
        onehot = []
        for n in range(MOBA_GROUP):
            tile = jnp.where(tile_lane == j * MOBA_GROUP + n, 1.0, 0.0).astype(BF16)
            onehot += [tile] * (blk // tile_rows)
        k_aug = jnp.concatenate([k_ref[0, pl.ds(st, grp), :], jnp.concatenate(onehot, axis=0)], axis=1)
        raw = [_dot_nt(q_aug, k_aug[u * sub:(u + 1) * sub, :]) for u in range(MOBA_SPLIT)]
        for u in range(MOBA_SPLIT):
            m_new = jnp.maximum(m, jnp.max(raw[u], axis=1, keepdims=True))
            alpha = jnp.exp2((m - m_new) * c_exp)
            p = jnp.exp2((raw[u] - m_new) * c_exp)
            l = l * alpha + jnp.sum(p, axis=1, keepdims=True)
            acc = acc * alpha + _dot(p.astype(BF16), v_ref[0, pl.ds(st + u * sub, sub), :])
            m = m_new
        return m, l, acc

    n_groups = lax.shift_right_logical(i * bpt + (bpt - 1) + (MOBA_GROUP - 1),
                                       MOBA_GROUP.bit_length() - 1)
    m, l, acc = lax.fori_loop(0, n_groups, body, (m, l, acc))
    o_ref[0] = (acc / l).astype(o_ref.dtype)


def _moba_prompt(q, kbf, vbf, kmean):
    b, s, _ = q.shape
    tq = MOBA_QTILE
    nb = s // MOBA_BLOCK
    assert s % tq == 0 and tq % MOBA_BLOCK == 0 and nb <= LANES and nb % MOBA_GROUP == 0
    return pl.pallas_call(
        _moba_prompt_kernel,
        grid=(b, MOBA_HEADS, s // tq),
        in_specs=[pl.BlockSpec((1, tq, HEAD_DIM), lambda bi, h, i: (bi, i, h)),
                  pl.BlockSpec((1, s, HEAD_DIM), lambda bi, h, i: (bi, 0, h)),
                  pl.BlockSpec((1, s, HEAD_DIM), lambda bi, h, i: (bi, 0, h)),
                  pl.BlockSpec((1, nb, HEAD_DIM), lambda bi, h, i: (bi, 0, h))],
        out_specs=pl.BlockSpec((1, tq, HEAD_DIM), lambda bi, h, i: (bi, i, h)),
        out_shape=jax.ShapeDtypeStruct((b, s, MOBA_WIDTH), BF16),
        compiler_params=pltpu.CompilerParams(
            dimension_semantics=("parallel", "parallel", "arbitrary"),
            vmem_limit_bytes=VMEM_LIMIT),
        name="moba_prompt",
    )(q, kbf, vbf, kmean)


def _gdn_sample_prep_kernel(x_ref, st_ref, ba_ref, cw_ref, arow_ref,
                            q_ref, k_ref, v_ref, beta_ref, dec_ref):
    t_len = x_ref.shape[0]
    ctx = GDN_CONV_W - 1

    def src(tt):
        return st_ref[tt + ctx] if tt < 0 else x_ref[tt]

    for t in range(t_len):
        conv = src(t - ctx) * cw_ref[0:1, :]
        for w in range(1, GDN_CONV_W):
            conv = conv + src(t - ctx + w) * cw_ref[w:w + 1, :]
        qs, ks, v = _conv_silu_norm(conv)
        q_ref[t] = jnp.concatenate(qs, axis=1)
        k_ref[t] = jnp.concatenate(ks, axis=1)
        v_ref[t] = v
        ba = ba_ref[t]
        beta_full = jax.nn.sigmoid(ba)
        dec_full = jnp.exp(-jnp.exp(arow_ref[0:1, :]) * _softplus(ba + arow_ref[1:2, :]))
        rows = ba.shape[0]
        beta_ref[t] = jnp.concatenate(
            [jnp.broadcast_to(beta_full[:, h:h + 1], (rows, HEAD_DIM)) for h in range(GDN_HEADS)], axis=1)
        dec_ref[t] = jnp.concatenate(
            [jnp.broadcast_to(dec_full[:, GDN_HEADS + h:GDN_HEADS + h + 1], (rows, HEAD_DIM))
             for h in range(GDN_HEADS)], axis=1)


def _gdn_sample_prep(x_t, st_t, ba_t, conv_w, arow):
    t_len, db, _ = x_t.shape
    bt = min(db, 32)
    assert db % bt == 0

    def spec(t, w):
        return pl.BlockSpec((t, bt, w), lambda i: (0, i, 0))

    out = jax.ShapeDtypeStruct((t_len, db, GDN_WIDTH), F32)
    return pl.pallas_call(
        _gdn_sample_prep_kernel,
        grid=(db // bt,),
        in_specs=[spec(t_len, GDN_CONV_CH), spec(GDN_CONV_W - 1, GDN_CONV_CH), spec(t_len, LANES),
                  pl.BlockSpec((GDN_CONV_W, GDN_CONV_CH), lambda i: (0, 0)),
                  pl.BlockSpec((2, LANES), lambda i: (0, 0))],
        out_specs=[spec(t_len, GDN_WIDTH)] * 5,
        out_shape=[out] * 5,
        compiler_params=pltpu.CompilerParams(dimension_semantics=("parallel",),
                                             vmem_limit_bytes=VMEM_LIMIT),
        name="gdn_sample_prep",
    )(x_t, st_t, ba_t, conv_w, arow)


def _gdn_sample_rec_kernel(s0_ref, qt_ref, kt_ref, v_ref, beta_ref, dec_ref, o_ref, s_ref):
    bt = s0_ref.shape[0]
    t_len = v_ref.shape[1]

    def per_batch(bi, carry):
        v, beta, dec = v_ref[bi], beta_ref[bi], dec_ref[bi]
        outs = []
        for h in range(GDN_HEADS):
            cols = slice(h * HEAD_DIM, (h + 1) * HEAD_DIM)
            s = s0_ref[bi, h]
            kt = kt_ref[bi, h]
            qt = qt_ref[bi, h]
            rows = []
            for t in range(t_len):
                s = s * dec[t:t + 1, cols]
                kcol = kt[:, t:t + 1]
                kv = jnp.sum(kcol * s, axis=0, keepdims=True)
                upd = (v[t:t + 1, cols] - kv) * beta[t:t + 1, cols]
                s = s + kcol * upd
                rows.append(jnp.sum(qt[:, t:t + 1] * s, axis=0, keepdims=True))
            s_ref[bi, h] = s
            outs.append(jnp.concatenate(rows, axis=0))
        o_ref[bi] = jnp.concatenate(outs, axis=1)
        return carry

    lax.fori_loop(0, bt, per_batch, 0)


def _gdn_sample_rec(s0, qt, kt, v, beta, dec):
    db, t_len, _ = v.shape
    bt = min(db, 8)
    assert db % bt == 0
    st_spec = pl.BlockSpec((bt, GDN_HEADS, HEAD_DIM, HEAD_DIM), lambda i: (i, 0, 0, 0))
    tr_spec = pl.BlockSpec((bt, GDN_HEADS, HEAD_DIM, t_len), lambda i: (i, 0, 0, 0))
    tok_spec = pl.BlockSpec((bt, t_len, GDN_WIDTH), lambda i: (i, 0, 0))
    return pl.pallas_call(
        _gdn_sample_rec_kernel,
        grid=(db // bt,),
        in_specs=[st_spec, tr_spec, tr_spec, tok_spec, tok_spec, tok_spec],
        out_specs=[tok_spec, st_spec],
        out_shape=[jax.ShapeDtypeStruct((db, t_len, GDN_WIDTH), F32),
                   jax.ShapeDtypeStruct(s0.shape, F32)],
        compiler_params=pltpu.CompilerParams(dimension_semantics=("parallel",),
                                             vmem_limit_bytes=VMEM_LIMIT),
        name="gdn_sample_rec",
    )(s0, qt, kt, v, beta, dec)


PAGES_PER_STEP = 16
PAGE_ROWS = PAGE_SIZE * MOBA_HEADS
PAGES_PER_BLOCK = MOBA_BLOCK // PAGE_SIZE
BLOCKS_PER_STEP = PAGES_PER_STEP // PAGES_PER_BLOCK
STEP_ROWS = PAGES_PER_STEP * PAGE_ROWS
HEAD_SHIFT = MOBA_HEADS.bit_length() - 1
assert 1 << HEAD_SHIFT == MOBA_HEADS and SUBLANES == 2 * MOBA_HEADS and PAGES_PER_BLOCK == 2


PAGE_RING = 3


def _moba_sample_kernel(pt_ref, q_ref, kn_ref, vn_ref, ck_ref, cv_ref, o_ref,
                        kwin_ref, vwin_ref, ksem, vsem,
                        kbuf_ref, vbuf_ref, kmean_ref, mem_ref, m_ref, l_ref, acc_ref):
    bp, g = pl.program_id(0), pl.program_id(1)
    n_batch = pl.num_programs(0) - 1
    n_steps = pl.num_programs(1)
    n_pages = n_steps * PAGES_PER_STEP
    lin = bp * n_steps + g

    def page_copies(step, ring_slot, stage):
        sb, sg = lax.div(step, n_steps), lax.rem(step, n_steps)
        if stage == 1:
            needed, row, cache, win, sem = sb < n_batch, sb, ck_ref, kwin_ref, ksem
        else:
            needed, row, cache, win, sem = sb > 0, sb - 1, cv_ref, vwin_ref, vsem
        base = jnp.where(needed, row, 0) * n_pages + sg * PAGES_PER_STEP
        return needed, [pltpu.make_async_copy(cache.at[pt_ref[base + pg]], win.at[ring_slot, pg],
                                              sem.at[ring_slot]) for pg in range(PAGES_PER_STEP)]

    def request(step, ring_slot):
        for stage in (1, 2):
            needed, copies = page_copies(step, ring_slot, stage)

            @pl.when(needed)
            def _():
                for c in copies:
                    c.start()

    ring_slot = lax.rem(lin, PAGE_RING)

    @pl.when(lin == 0)
    def _():
        for step in range(PAGE_RING - 1):
            request(jnp.int32(step), step)

    ahead = lin + (PAGE_RING - 1)

    @pl.when(ahead < (n_batch + 1) * n_steps)
    def _():
        request(ahead, lax.rem(ahead, PAGE_RING))

    for stage in (1, 2):
        needed, copies = page_copies(lin, ring_slot, stage)

        @pl.when(needed)
        def _():
            for c in copies:
                c.wait()

    rows = q_ref.shape[1]
    t_len = rows // MOBA_HEADS
    scale = HEAD_DIM ** -0.5
    t_shift = t_len.bit_length() - 1
    assert 1 << t_shift == t_len and kn_ref.shape[1] == rows
    step_lanes = BLOCKS_PER_STEP * MOBA_HEADS
    par = jnp.bitwise_and(bp, 1)
    row_i = lax.broadcasted_iota(jnp.int32, (rows, LANES), 0)
    lane_i = lax.broadcasted_iota(jnp.int32, (rows, LANES), 1)
    row_head = lax.shift_right_logical(row_i, t_shift)
    lane_head = jnp.bitwise_and(lane_i, MOBA_HEADS - 1)

    @pl.when(jnp.logical_and(bp == 0, g == 0))
    def _():
        kmean_ref[...] = jnp.zeros(kmean_ref.shape, F32)

    @pl.when(jnp.logical_and(bp > 0, g == 0))
    def _():
        q = q_ref[0]
        q16 = q.astype(BF16)
        n_cand = n_steps * step_lanes
        valid = jnp.logical_and(lane_head == row_head, lane_i < n_cand)
        member = _top3_members(_dot_nt_f32(q, kmean_ref[...]), valid)
        mem_ref[...] = jnp.where(member, 1.0, 0.0)
        pad = jnp.zeros((LANES - rows, HEAD_DIM), F32)
        kn = jnp.concatenate([kn_ref[0], pad], axis=0).astype(BF16)
        vn = jnp.concatenate([vn_ref[0], pad], axis=0).astype(BF16)
        s = _dot_nt(q16, kn) * scale
        ok = jnp.logical_and(lane_head == row_head, lane_i < rows)
        ok = jnp.logical_and(ok, lax.shift_right_logical(lane_i, HEAD_SHIFT)
                             <= jnp.bitwise_and(row_i, t_len - 1))
        s = jnp.where(ok, s, -jnp.inf)
        m = jnp.max(s, axis=1, keepdims=True)
        p = jnp.exp(s - m)
        m_ref[...] = jnp.broadcast_to(m, m_ref.shape)
        l_ref[...] = jnp.broadcast_to(jnp.sum(p, axis=1, keepdims=True), l_ref.shape)
        acc_ref[...] = _dot(p.astype(BF16), vn)

    @pl.when(bp > 0)
    def _():
        for pg in range(PAGES_PER_STEP):
            vbuf_ref[pg * PAGE_ROWS:(pg + 1) * PAGE_ROWS, :] = vwin_ref[ring_slot, pg].astype(BF16)
        member = mem_ref[...]
        blk_rows = PAGES_PER_BLOCK * PAGE_ROWS
        col_head = jnp.bitwise_and(lax.broadcasted_iota(jnp.int32, (rows, blk_rows), 1), MOBA_HEADS - 1)
        own_head = col_head == lax.shift_right_logical(
            lax.broadcasted_iota(jnp.int32, (rows, blk_rows), 0), t_shift)
        cand_blk = lax.shift_right_logical(lane_i, HEAD_SHIFT)
        bias = []
        for n in range(BLOCKS_PER_STEP):
            picked = jnp.sum(jnp.where(cand_blk == g * BLOCKS_PER_STEP + n, member, 0.0),
                             axis=1, keepdims=True) > 0.0
            bias.append(jnp.where(jnp.logical_and(own_head, picked), 0.0, -jnp.inf))
        s = _dot_nt(q_ref[0].astype(BF16), kbuf_ref[1 - par, g]) * scale + jnp.concatenate(bias, axis=1)
        m = m_ref[:, 0:1]
        m_new = jnp.maximum(m, jnp.max(s, axis=1, keepdims=True))
        alpha = jnp.exp(m - m_new)
        p = jnp.exp(s - m_new)
        l = l_ref[:, 0:1] * alpha + jnp.sum(p, axis=1, keepdims=True)
        acc = acc_ref[...] * alpha + _dot(p.astype(BF16), vbuf_ref[...])
        m_ref[...] = jnp.broadcast_to(m_new, m_ref.shape)
        l_ref[...] = jnp.broadcast_to(l, l_ref.shape)
        acc_ref[...] = acc

        @pl.when(g == n_steps - 1)
        def _():
            o_ref[0] = acc / l

    @pl.when(bp < n_batch)
    def _():
        low = lax.broadcasted_iota(jnp.int32, (SUBLANES, HEAD_DIM), 0) < MOBA_HEADS

        def block_mean(n):
            tot = None
            for r in range(PAGES_PER_BLOCK):
                pg = n * PAGES_PER_BLOCK + r
                page = kwin_ref[ring_slot, pg]
                kbuf_ref[par, g, pg * PAGE_ROWS:(pg + 1) * PAGE_ROWS, :] = page.astype(BF16)
                part = jnp.sum(page.reshape(PAGE_ROWS // SUBLANES, SUBLANES, HEAD_DIM), axis=0)
                tot = part if tot is None else tot + part
            return (tot + pltpu.roll(tot, MOBA_HEADS, 0)) * (1.0 / MOBA_BLOCK)

        tiles = [jnp.where(low, block_mean(2 * n), block_mean(2 * n + 1))
                 for n in range(BLOCKS_PER_STEP // 2)]
        kmean_ref[pl.ds(pl.multiple_of(g * step_lanes, step_lanes), step_lanes), :] = (
            jnp.concatenate(tiles, axis=0))


def _moba_sample(page_flat, q2, kn2, vn2, cache_k, cache_v, n_pages):
    db, rows, _ = q2.shape
    assert n_pages % PAGES_PER_STEP == 0
    n_steps = n_pages // PAGES_PER_STEP
    n_cand = n_steps * BLOCKS_PER_STEP * MOBA_HEADS
    assert rows <= LANES and n_cand <= LANES
    tok = pl.BlockSpec((1, rows, HEAD_DIM), lambda bp, g, pt: (jnp.maximum(bp - 1, 0), 0, 0))
    window = pltpu.VMEM((PAGE_RING, PAGES_PER_STEP, PAGE_ROWS, HEAD_DIM), F32)
    return pl.pallas_call(
        _moba_sample_kernel,
        grid_spec=pltpu.PrefetchScalarGridSpec(
            num_scalar_prefetch=1,
            grid=(db + 1, n_steps),
            in_specs=[tok, tok, tok, pl.BlockSpec(memory_space=pl.ANY),
                      pl.BlockSpec(memory_space=pl.ANY)],
            out_specs=tok,
            scratch_shapes=[window, window,
                            pltpu.SemaphoreType.DMA((PAGE_RING,)), pltpu.SemaphoreType.DMA((PAGE_RING,)),
                            pltpu.VMEM((2, n_steps, STEP_ROWS, HEAD_DIM), BF16),
                            pltpu.VMEM((STEP_ROWS, HEAD_DIM), BF16),
                            pltpu.VMEM((LANES, HEAD_DIM), F32),
                            pltpu.VMEM((rows, LANES), F32),
                            pltpu.VMEM((rows, LANES), F32), pltpu.VMEM((rows, LANES), F32),
                            pltpu.VMEM((rows, HEAD_DIM), F32)],
        ),
        out_shape=jax.ShapeDtypeStruct((db, rows, HEAD_DIM), F32),
        compiler_params=pltpu.CompilerParams(dimension_semantics=("arbitrary", "arbitrary"),
                                             vmem_limit_bytes=VMEM_LIMIT),
        name="moba_sample",
    )(page_flat, q2, kn2, vn2, cache_k, cache_v)


def _merge_kernel(og_ref, za_ref, om_ref, zb_ref, x_ref, gw_ref, wo_ref, nf_ref, y_ref):
    og = og_ref[...].astype(F32)
    parts = []
    for h in range(GDN_HEADS):
        oh = og[:, h * HEAD_DIM:(h + 1) * HEAD_DIM]
        parts.append(oh * lax.rsqrt(jnp.mean(oh * oh, axis=-1, keepdims=True) + NORM_EPS) * gw_ref[...])
    ga = jnp.concatenate(parts, axis=1) * _silu(za_ref[...].astype(F32))
    gb = om_ref[...].astype(F32) * _silu(zb_ref[...].astype(F32))
    cat = jnp.concatenate([ga, gb], axis=1).astype(BF16)
    hp = x_ref[...] + _dot(cat, wo_ref[...])
    y_ref[...] = hp * lax.rsqrt(jnp.mean(hp * hp, axis=-1, keepdims=True) + NORM_EPS) * nf_ref[...]


def _merge(og, za, om, zb, x2d, gdn_norm_w, w_out16, norm_f_w):
    m, d = x2d.shape
    tm = ROW_TILE
    assert m % tm == 0
    row = lambda i: (i, 0)
    full = lambda i: (0, 0)
    half = pl.BlockSpec((tm, GDN_WIDTH), row)
    return pl.pallas_call(
        _merge_kernel,
        grid=(m // tm,),
        in_specs=[half, half, half, half, pl.BlockSpec((tm, d), row),
                  pl.BlockSpec((1, HEAD_DIM), full), pl.BlockSpec(w_out16.shape, full),
                  pl.BlockSpec((1, d), full)],
        out_specs=pl.BlockSpec((tm, d), row),
        out_shape=jax.ShapeDtypeStruct((m, d), F32),
        compiler_params=pltpu.CompilerParams(dimension_semantics=("parallel",),
                                             vmem_limit_bytes=VMEM_LIMIT),
        name="merge_out",
    )(og, za, om, zb, x2d, gdn_norm_w, w_out16, norm_f_w)


def _rope_tables(pos):
    half = HEAD_DIM // 2
    inv_freq = ROPE_THETA ** (-jnp.arange(half, dtype=F32) / half)
    ang = pos.astype(F32)[:, None] * inv_freq[None, :]
    cos, sin = jnp.cos(ang), jnp.sin(ang)
    return jnp.concatenate([cos, cos], axis=1), jnp.concatenate([-sin, sin], axis=1)


def _relayout_w_in(w):
    o = 0
    parts = {}
    for name, n in (("qkv", GDN_CONV_CH), ("za", GDN_WIDTH), ("b", GDN_HEADS), ("a", GDN_HEADS),
                    ("qb", MOBA_WIDTH), ("kb", MOBA_WIDTH), ("vb", MOBA_WIDTH), ("zb", MOBA_WIDTH)):
        parts[name] = w[:, o:o + n]
        o += n
    pad = jnp.zeros((w.shape[0], LANES - 2 * GDN_HEADS), w.dtype)
    return jnp.concatenate([parts["qkv"], parts["za"], parts["qb"], parts["kb"], parts["vb"],
                            parts["zb"], parts["b"], parts["a"], pad], axis=1).astype(BF16)


def kernel(x_prompt, x_sample, cache_k, cache_v, state_gdn, state_conv, page_table, norm_in_w, w_in,
           conv_w, a_log, dt_bias, gdn_norm_w, w_out, norm_f_w):
    b, s, d = x_prompt.shape
    db, t_len, _ = x_sample.shape
    depth = w_in.shape[0]
    assert depth == 1
    n_pages = page_table.shape[1]
    past_len = n_pages * PAGE_SIZE
    assert past_len % MOBA_BLOCK == 0
    assert t_len <= PAGE_SIZE and t_len >= GDN_CONV_W - 1 and (db * t_len) % ROW_TILE == 0
    assert ROW_TILE % t_len == 0

    w_p = _relayout_w_in(w_in[0])
    w_o = w_out[0].astype(BF16)
    nw = norm_in_w[0][None, :]
    cw = conv_w[0]
    zeros4 = jnp.zeros((GDN_HEADS,), F32)
    lane_pad = jnp.zeros((LANES - 2 * GDN_HEADS,), F32)
    arow = jnp.stack([jnp.concatenate([zeros4, a_log[0], lane_pad]),
                      jnp.concatenate([zeros4, dt_bias[0], lane_pad])])
    acol = jnp.stack([jnp.concatenate([zeros4, a_log[0]]),
                      jnp.concatenate([zeros4, dt_bias[0]])])[:, :, None]
    gw = gdn_norm_w[0][None, :]
    nf = norm_f_w[None, :]

    cos_p, sin_p = _rope_tables(jnp.arange(s, dtype=jnp.int32))
    (qkv_p, za_p, q_p, k_p, v_p, zb_p, ba_p, kbf_p, vbf_p, kmean_p) = _inproj(
        x_prompt.reshape(b * s, d), nw, w_p, cos_p, sin_p, prompt=True)
    qkv_p3 = qkv_p.reshape(b, s, GDN_CONV_CH)
    ba_p3 = ba_p.reshape(b, s, LANES)
    bat_p = jnp.transpose(ba_p3[:, :, :SUBLANES], (0, 2, 1))
    n_m, intra, rhs, qdec, kdec, glast = _gdn_prep(qkv_p3, ba_p3, bat_p, cw, arow, acol)
    bh, nc, c, _ = n_m.shape
    t_m = _tri_inv(jnp.transpose(n_m.reshape(bh * nc, c, c), (1, 2, 0)))
    t_m = jnp.transpose(t_m, (2, 0, 1)).reshape(bh, nc, c, c)
    o_gdn_p, s_gdn_p = _gdn_scan(t_m, rhs, qdec, kdec, intra, glast, b)
    o_moba_p = _moba_prompt(q_p.reshape(b, s, MOBA_WIDTH), kbf_p.reshape(b, s, MOBA_WIDTH),
                            vbf_p.reshape(b, s, MOBA_WIDTH),
                            kmean_p.reshape(b, s // MOBA_BLOCK, MOBA_WIDTH))
    y_p = _merge(o_gdn_p.reshape(b * s, GDN_WIDTH), za_p, o_moba_p.reshape(b * s, MOBA_WIDTH), zb_p,
                 x_prompt.reshape(b * s, d), gw, w_o, nf)

    cos_s, sin_s = _rope_tables(past_len + jnp.arange(t_len, dtype=jnp.int32))
    reps = ROW_TILE // t_len
    (qkv_s, za_s, q_s, k_s, v_s, zb_s, ba_s) = _inproj(
        x_sample.reshape(db * t_len, d), nw, w_p, jnp.tile(cos_s, (reps, 1)), jnp.tile(sin_s, (reps, 1)),
        prompt=False)
    qkv_s3 = qkv_s.reshape(db, t_len, GDN_CONV_CH)
    tb = lambda a: jnp.transpose(a, (1, 0, 2))
    qg, kg, vg, beta_t, dec_t = _gdn_sample_prep(
        tb(qkv_s3), tb(state_conv[0]), tb(ba_s.reshape(db, t_len, LANES)), cw, arow)
    to_cols = lambda a: jnp.transpose(a.reshape(t_len, db, GDN_HEADS, HEAD_DIM), (1, 2, 3, 0))
    o_gdn_s, s_gdn_s = _gdn_sample_rec(state_gdn[0], to_cols(qg), to_cols(kg), tb(vg), tb(beta_t),
                                       tb(dec_t))

    page_flat = page_table.reshape(-1).astype(jnp.int32)
    ck = cache_k.reshape(cache_k.shape[1], PAGE_ROWS, HEAD_DIM)
    cv = cache_v.reshape(cache_v.shape[1], PAGE_ROWS, HEAD_DIM)
    rows_s = t_len * MOBA_HEADS
    q2 = jnp.transpose(q_s.reshape(db, t_len, MOBA_HEADS, HEAD_DIM), (0, 2, 1, 3))
    o2 = _moba_sample(page_flat, q2.reshape(db, rows_s, HEAD_DIM), k_s.reshape(db, rows_s, HEAD_DIM),
                      v_s.reshape(db, rows_s, HEAD_DIM), ck, cv, n_pages)
    o_moba_s = jnp.transpose(o2.reshape(db, MOBA_HEADS, t_len, HEAD_DIM), (0, 2, 1, 3))
    y_s = _merge(o_gdn_s.reshape(db * t_len, GDN_WIDTH), za_s, o_moba_s.reshape(db * t_len, MOBA_WIDTH),
                 zb_s, x_sample.reshape(db * t_len, d), gw, w_o, nf)

    ctx = GDN_CONV_W - 1
    heads = lambda a, n, t: a.reshape(1, n, t, MOBA_HEADS, HEAD_DIM)
    return (y_p.reshape(b, s, d), y_s.reshape(db, t_len, d),
            heads(k_p, b, s), heads(v_p, b, s),
            s_gdn_p.reshape(1, b, GDN_HEADS, HEAD_DIM, HEAD_DIM),
            qkv_p3[:, s - ctx:, :][None],
            heads(k_s, db, t_len), heads(v_s, db, t_len),
            s_gdn_s[None],
            qkv_s3[:, t_len - ctx:, :][None])
```

```python
import functools
import math

import jax
import jax.numpy as jnp
from jax import lax
from jax.experimental import pallas as pl
from jax.experimental.pallas import tpu as pltpu

F32 = jnp.float32
BF16 = jnp.bfloat16

HEAD_DIM = 128
GDN_HEADS = 4
MOBA_HEADS = 4
GDN_QK = GDN_HEADS * HEAD_DIM
GDN_WIDTH = GDN_HEADS * HEAD_DIM
GDN_CONV_W = 4
GDN_CONV_CH = 2 * GDN_QK + GDN_WIDTH
GDN_CHUNK = 64
MOBA_WIDTH = MOBA_HEADS * HEAD_DIM
MOBA_BLOCK = 256
MOBA_TOPK = 3
PAGE_SIZE = 128
ROPE_THETA = 10000.0
NORM_EPS = 1e-6
LANES = 128
SUBLANES = 8
MASK_BIAS = -1e30

C_QKV = 0
C_ZA = GDN_CONV_CH
C_QB = C_ZA + GDN_WIDTH
C_KB = C_QB + MOBA_WIDTH
C_VB = C_KB + MOBA_WIDTH
C_ZB = C_VB + MOBA_WIDTH
C_BA = C_ZB + MOBA_WIDTH
IN_COLS_PAD = C_BA + LANES

ROW_TILE = 256
MERGE_TILE = 512
VMEM_LIMIT = 56 * 1024 * 1024


def _dot(a, b):
    return jnp.dot(a, b, preferred_element_type=F32)


def _dot_nt(a, b):
    return lax.dot_general(a, b, (((1,), (1,)), ((), ())), preferred_element_type=F32)


def _dot_tn(a, b):
    return lax.dot_general(a, b, (((0,), (0,)), ((), ())), preferred_element_type=F32)


def _dot_nt_f32(a, b):
    return lax.dot_general(a, b, (((1,), (1,)), ((), ())), preferred_element_type=F32,
                           precision=lax.Precision.HIGHEST)


def _dot_f32(a, b):
    return jnp.dot(a, b, preferred_element_type=F32, precision=lax.Precision.HIGHEST)


def _split_bf16(a):
    hi = a.astype(BF16)
    lo = (a - hi.astype(F32)).astype(BF16)
    return hi, lo


def _dot3(a, b):
    ah, al = _split_bf16(a)
    bh, bl = _split_bf16(b)
    return _dot(ah, bh) + (_dot(ah, bl) + _dot(al, bh))


def _dot3_nt(a, b):
    ah, al = _split_bf16(a)
    bh, bl = _split_bf16(b)
    return _dot_nt(ah, bh) + (_dot_nt(ah, bl) + _dot_nt(al, bh))


def _silu(x):
    return x * jax.nn.sigmoid(x)


def _softplus(x):
    return jnp.maximum(x, 0.0) + jnp.log1p(jnp.exp(-jnp.abs(x)))


def _top3_members(gate, valid, axis=1):
    cand = lax.broadcasted_iota(jnp.int32, gate.shape, axis).astype(F32)
    g = jnp.where(valid, gate, -jnp.inf)
    member = jnp.zeros(gate.shape, dtype=jnp.bool_)
    for _ in range(MOBA_TOPK):
        m = jnp.max(g, axis=axis, keepdims=True)
        idx = jnp.min(jnp.where(g == m, cand, float(LANES)), axis=axis, keepdims=True)
        pick = cand == idx
        member = jnp.logical_or(member, pick)
        g = jnp.where(pick, -jnp.inf, g)
    return jnp.logical_and(member, valid)


def _inproj_kernel(x_ref, nw_ref, w_ref, cos_ref, sin_ref,
                   qkv_ref, za_ref, q_ref, k_ref, v_ref, zb_ref, ba_ref, *prompt_refs):
    x = x_ref[...]
    ms = jnp.mean(x * x, axis=-1, keepdims=True)
    hb = (x * lax.rsqrt(ms + NORM_EPS) * nw_ref[...]).astype(BF16)

    def proj(c0, n):
        return _dot(hb, w_ref[:, c0:c0 + n])

    def rope(t):
        parts = []
        for h in range(MOBA_HEADS):
            th = t[:, h * HEAD_DIM:(h + 1) * HEAD_DIM]
            parts.append(th * cos_ref[...] + pltpu.roll(th, HEAD_DIM // 2, 1) * sin_ref[...])
        return jnp.concatenate(parts, axis=1)

    for j in range(GDN_CONV_CH // GDN_QK):
        qkv_ref[:, j * GDN_QK:(j + 1) * GDN_QK] = proj(C_QKV + j * GDN_QK, GDN_QK)
    za_ref[...] = proj(C_ZA, GDN_WIDTH).astype(za_ref.dtype)
    q_ref[...] = rope(proj(C_QB, MOBA_WIDTH))
    k = rope(proj(C_KB, MOBA_WIDTH))
    k_ref[...] = k
    v = proj(C_VB, MOBA_WIDTH)
    v_ref[...] = v
    zb_ref[...] = proj(C_ZB, MOBA_WIDTH).astype(zb_ref.dtype)
    ba_ref[...] = proj(C_BA, LANES)
    if prompt_refs:
        kbf_ref, vbf_ref, kmean_ref = prompt_refs
        kbf_ref[...] = k.astype(BF16)
        vbf_ref[...] = v.astype(BF16)
        kmean_ref[0] = jnp.mean(k, axis=0, keepdims=True)


def _inproj(x2d, norm_w, w_p, cosf, sinf, *, prompt):
    m, d = x2d.shape
    tm = ROW_TILE
    assert m % tm == 0 and cosf.shape[0] % tm == 0
    n_pos = cosf.shape[0] // tm
    row = lambda i: (i, 0)
    full = lambda i: (0, 0)
    pos = (lambda i: (i % n_pos, 0)) if n_pos > 1 else full
    out_shape = [jax.ShapeDtypeStruct((m, GDN_CONV_CH), F32)]
    out_specs = [pl.BlockSpec((tm, GDN_CONV_CH), row)]
    for dt in (BF16, F32, F32, F32, BF16):
        out_shape.append(jax.ShapeDtypeStruct((m, MOBA_WIDTH), dt))
        out_specs.append(pl.BlockSpec((tm, MOBA_WIDTH), row))
    out_shape.append(jax.ShapeDtypeStruct((m, LANES), F32))
    out_specs.append(pl.BlockSpec((tm, LANES), row))
    if prompt:
        assert tm == MOBA_BLOCK
        for _ in range(2):
            out_shape.append(jax.ShapeDtypeStruct((m, MOBA_WIDTH), BF16))
            out_specs.append(pl.BlockSpec((tm, MOBA_WIDTH), row))
        out_shape.append(jax.ShapeDtypeStruct((m // tm, 1, MOBA_WIDTH), F32))
        out_specs.append(pl.BlockSpec((1, 1, MOBA_WIDTH), lambda i: (i, 0, 0)))
    return pl.pallas_call(
        _inproj_kernel,
        grid=(m // tm,),
        in_specs=[pl.BlockSpec((tm, d), row), pl.BlockSpec((1, d), full),
                  pl.BlockSpec((d, IN_COLS_PAD), full),
                  pl.BlockSpec((tm, HEAD_DIM), pos), pl.BlockSpec((tm, HEAD_DIM), pos)],
        out_specs=out_specs,
        out_shape=out_shape,
        compiler_params=pltpu.CompilerParams(dimension_semantics=("parallel",),
                                             vmem_limit_bytes=VMEM_LIMIT),
        name="inproj_prompt" if prompt else "inproj_sample",
    )(x2d, norm_w, w_p, cosf, sinf)


def _conv_silu_norm(c):
    c = _silu(c)
    qs, ks = [], []
    for h in range(GDN_HEADS):
        qh = c[:, h * HEAD_DIM:(h + 1) * HEAD_DIM]
        kh = c[:, GDN_QK + h * HEAD_DIM:GDN_QK + (h + 1) * HEAD_DIM]
        qs.append(qh * (lax.rsqrt(jnp.sum(qh * qh, axis=-1, keepdims=True) + NORM_EPS)
                        * (HEAD_DIM ** -0.5)))
        ks.append(kh * lax.rsqrt(jnp.sum(kh * kh, axis=-1, keepdims=True) + NORM_EPS))
    return qs, ks, c[:, 2 * GDN_QK:]


def _gdn_prep_kernel(x_ref, halo_ref, ba_ref, bat_ref, cw_ref, arow_ref, acol_ref,
                     n_ref, intra_ref, rhs_ref, qdec_ref, kdec_ref, glast_ref, buf_ref):
    i = pl.program_id(1)
    tt = x_ref.shape[1]
    c = GDN_CHUNK
    halo = jnp.where(i > 0, halo_ref[0], 0.0)
    buf_ref[0:SUBLANES, :] = halo
    buf_ref[SUBLANES:SUBLANES + tt, :] = x_ref[0]
    full = buf_ref[...]
    conv = full[SUBLANES:, :] * cw_ref[GDN_CONV_W - 1:GDN_CONV_W, :]
    for back in range(1, GDN_CONV_W):
        w = GDN_CONV_W - 1 - back
        conv = conv + pltpu.roll(full, back, 0)[SUBLANES:, :] * cw_ref[w:w + 1, :]
    qs, ks, v = _conv_silu_norm(conv)

    ba = ba_ref[0]
    beta_full = jax.nn.sigmoid(ba)
    g_full = -jnp.exp(arow_ref[0:1, :]) * _softplus(ba + arow_ref[1:2, :])
    bat = bat_ref[0]
    gt_full = -jnp.exp(acol_ref[0]) * _softplus(bat + acol_ref[1])

    ri = lax.broadcasted_iota(jnp.int32, (c, c), 0)
    ci = lax.broadcasted_iota(jnp.int32, (c, c), 1)
    tril = ri >= ci
    strict = ri > ci
    ltri = jnp.where(tril, 1.0, 0.0).astype(F32)
    utri = jnp.where(ri <= ci, 1.0, 0.0).astype(F32)

    for cc in range(tt // c):
        rows = slice(cc * c, (cc + 1) * c)
        gcol_all = _dot_f32(ltri, g_full[rows, :])
        grow_all = _dot_f32(gt_full[:, rows], utri)
        for h in range(GDN_HEADS):
            gc_c = gcol_all[:, GDN_HEADS + h:GDN_HEADS + h + 1]
            gc_r = grow_all[GDN_HEADS + h:GDN_HEADS + h + 1, :]
            gamma = jnp.where(tril, jnp.exp(jnp.where(tril, gc_c - gc_r, 0.0)), 0.0)
            beta = beta_full[rows, h:h + 1]
            qh, kh = qs[h][rows, :], ks[h][rows, :]
            vh = v[rows, h * HEAD_DIM:(h + 1) * HEAD_DIM]
            kb = kh * beta
            k16 = kh.astype(BF16)
            n_ref[h, cc] = jnp.where(strict, _dot_nt(kb.astype(BF16), k16) * gamma, 0.0)
            intra_ref[h, cc] = jnp.where(tril, _dot_nt(qh.astype(BF16), k16) * gamma, 0.0)
            egc = jnp.exp(gc_c)
            rhs_ref[h, cc, :, 0:HEAD_DIM] = vh * beta
            rhs_ref[h, cc, :, HEAD_DIM:2 * HEAD_DIM] = kb * egc
            qdec_ref[h, cc] = qh * egc
            gl = gc_c[c - 1:c, :]
            kdec_ref[h, cc] = kh * jnp.exp(gl - gc_c)
            glast_ref[h, cc] = jnp.broadcast_to(jnp.exp(gl), (SUBLANES, LANES))


def _gdn_prep(qkv, ba, bat, conv_w, arow, acol):
    b, s, _ = qkv.shape
    tt = ROW_TILE
    c = GDN_CHUNK
    cpt = tt // c
    nc = s // c
    hb = tt // SUBLANES
    bh = b * GDN_HEADS

    def o(x):
        return (jax.ShapeDtypeStruct((bh, nc, c, x), F32),
                pl.BlockSpec((GDN_HEADS, cpt, c, x), lambda bi, i: (bi, i, 0, 0)))

    outs = [o(c), o(c), o(2 * HEAD_DIM), o(HEAD_DIM), o(HEAD_DIM)]
    outs.append((jax.ShapeDtypeStruct((bh, nc, SUBLANES, LANES), F32),
                 pl.BlockSpec((GDN_HEADS, cpt, SUBLANES, LANES), lambda bi, i: (bi, i, 0, 0))))
    return pl.pallas_call(
        _gdn_prep_kernel,
        grid=(b, s // tt),
        in_specs=[
            pl.BlockSpec((1, tt, GDN_CONV_CH), lambda bi, i: (bi, i, 0)),
            pl.BlockSpec((1, SUBLANES, GDN_CONV_CH), lambda bi, i: (bi, jnp.maximum(i * hb - 1, 0), 0)),
            pl.BlockSpec((1, tt, LANES), lambda bi, i: (bi, i, 0)),
            pl.BlockSpec((1, SUBLANES, tt), lambda bi, i: (bi, 0, i)),
            pl.BlockSpec((GDN_CONV_W, GDN_CONV_CH), lambda bi, i: (0, 0)),
            pl.BlockSpec((2, LANES), lambda bi, i: (0, 0)),
            pl.BlockSpec((2, SUBLANES, 1), lambda bi, i: (0, 0, 0)),
        ],
        out_specs=[x[1] for x in outs],
        out_shape=[x[0] for x in outs],
        scratch_shapes=[pltpu.VMEM((SUBLANES + tt, GDN_CONV_CH), F32)],
        compiler_params=pltpu.CompilerParams(dimension_semantics=("parallel", "parallel"),
                                             vmem_limit_bytes=VMEM_LIMIT),
        name="gdn_prep",
    )(qkv, qkv, ba, bat, conv_w, arow, acol)


def _tri_inv_kernel(n_ref, t_ref):
    c = n_ref.shape[0]
    t_ref[...] = jnp.zeros(t_ref.shape, F32)

    for ig in range(c // SUBLANES):
        width = (ig + 1) * SUBLANES
        col = lax.broadcasted_iota(jnp.int32, (width, LANES), 0)

        def outer(i, carry, width=width, col=col):
            def inner(jg, acc):
                j0 = pl.multiple_of(jg * SUBLANES, SUBLANES)
                n_rows = n_ref[i, pl.ds(j0, SUBLANES), :]
                for r in range(SUBLANES):
                    acc = acc - n_rows[r:r + 1, :] * t_ref[j0 + r, 0:width, :]
                return acc
            groups = lax.shift_right_logical(i + (SUBLANES - 1), SUBLANES.bit_length() - 1)
            acc = lax.fori_loop(0, groups, inner, jnp.where(col == i, 1.0, 0.0).astype(F32))
            t_ref[i, 0:width, :] = acc
            return carry

        lax.fori_loop(ig * SUBLANES, (ig + 1) * SUBLANES, outer, 0)


def _tri_inv(nt):
    c, _, nmat = nt.shape
    assert nmat % LANES == 0
    spec = pl.BlockSpec((c, c, LANES), lambda g: (0, 0, g))
    return pl.pallas_call(
        _tri_inv_kernel,
        grid=(nmat // LANES,),
        in_specs=[spec],
        out_specs=spec,
        out_shape=jax.ShapeDtypeStruct(nt.shape, F32),
        compiler_params=pltpu.CompilerParams(dimension_semantics=("parallel",)),
        name="gdn_tri_inv",
    )(nt)


def _gdn_scan_kernel(t_ref, rhs_ref, qdec_ref, kdec_ref, intra_ref, glast_ref, o_ref, s_ref):
    i = pl.program_id(0)
    bh, ct, c, _ = t_ref.shape

    @pl.when(i == 0)
    def _():
        s_ref[...] = jnp.zeros(s_ref.shape, F32)

    heads = range(bh)
    for cc in range(ct):
        sol = [_dot3(t_ref[n, cc], rhs_ref[n, cc]) for n in heads]
        wqs = [_dot(jnp.concatenate([sol[n][:, HEAD_DIM:], qdec_ref[n, cc]], axis=0).astype(BF16),
                    s_ref[n].astype(BF16)) for n in heads]
        vn16 = [(sol[n][:, :HEAD_DIM] - wqs[n][:c]).astype(BF16) for n in heads]
        for n in heads:
            s_ref[n] = (s_ref[n] * glast_ref[n, cc, 0:1, :]
                        + _dot_tn(kdec_ref[n, cc].astype(BF16), vn16[n]))
        for n in heads:
            b, h = n // GDN_HEADS, n % GDN_HEADS
            o_ref[b, cc * c:(cc + 1) * c, h * HEAD_DIM:(h + 1) * HEAD_DIM] = (
                wqs[n][c:] + _dot(intra_ref[n, cc].astype(BF16), vn16[n])).astype(o_ref.dtype)


def _gdn_scan(t, rhs, qdec, kdec, intra, glast, b):
    bh, nc, c, _ = t.shape
    ct = 4
    assert nc % ct == 0

    def spec(x):
        return pl.BlockSpec((bh, ct, x.shape[2], x.shape[3]), lambda i: (0, i, 0, 0))

    return pl.pallas_call(
        _gdn_scan_kernel,
        grid=(nc // ct,),
        in_specs=[spec(t), spec(rhs), spec(qdec), spec(kdec), spec(intra), spec(glast)],
        out_specs=[pl.BlockSpec((b, ct * c, GDN_WIDTH), lambda i: (0, i, 0)),
                   pl.BlockSpec((bh, HEAD_DIM, HEAD_DIM), lambda i: (0, 0, 0))],
        out_shape=[jax.ShapeDtypeStruct((b, nc * c, GDN_WIDTH), BF16),
                   jax.ShapeDtypeStruct((bh, HEAD_DIM, HEAD_DIM), F32)],
        compiler_params=pltpu.CompilerParams(dimension_semantics=("arbitrary",),
                                             vmem_limit_bytes=VMEM_LIMIT),
        name="gdn_scan",
    )(t, rhs, qdec, kdec, intra, glast)


MOBA_GROUP = 4
MOBA_SPLIT = 2
MOBA_QTILE = MOBA_BLOCK


def _moba_prompt_kernel(q_ref, k_ref, v_ref, kmean_ref, o_ref):
    i = pl.program_id(2)
    tq = q_ref.shape[1]
    blk = MOBA_BLOCK
    scale = HEAD_DIM ** -0.5
    q = q_ref[0]
    q16 = q.astype(BF16)
    kmean = jnp.concatenate(
        [kmean_ref[0], jnp.zeros((LANES - kmean_ref.shape[1], HEAD_DIM), F32)], axis=0)
    bpt = tq // blk
    blk_shift = blk.bit_length() - 1
    own_blk = i * bpt + lax.shift_right_logical(
        lax.broadcasted_iota(jnp.int32, (LANES, tq), 1), blk_shift)
    past = lax.broadcasted_iota(jnp.int32, (LANES, tq), 0) < own_blk
    member = _top3_members(_dot3_nt(kmean, q), past, axis=0)
    bias = jnp.transpose(jnp.where(member, 0.0, MASK_BIAS))
    q_aug = jnp.concatenate([q16, bias.astype(BF16)], axis=1)
    grp = MOBA_GROUP * blk
    c_exp = scale * math.log2(math.e)
    tile_rows = 2 * SUBLANES
    tile_lane = lax.broadcasted_iota(jnp.int32, (tile_rows, LANES), 1)

    rq = lax.broadcasted_iota(jnp.int32, (blk, blk), 0)
    ck = lax.broadcasted_iota(jnp.int32, (blk, blk), 1)
    ms, ls, accs = [], [], []
    for r in range(bpt):
        start = pl.multiple_of((i * bpt + r) * blk, blk)
        s = _dot_nt(q16[r * blk:(r + 1) * blk, :], k_ref[0, pl.ds(start, blk), :])
        s = jnp.where(ck <= rq, s, -jnp.inf)
        m_r = jnp.max(s, axis=1, keepdims=True)
        p = jnp.exp2((s - m_r) * c_exp)
        ms.append(m_r)
        ls.append(jnp.sum(p, axis=1, keepdims=True))
        accs.append(_dot(p.astype(BF16), v_ref[0, pl.ds(start, blk), :]))
    m, l, acc = (jnp.concatenate(x, axis=0) for x in (ms, ls, accs))

    sub = grp // MOBA_SPLIT

    def body(j, carry):
        m, l, acc = carry
        st = pl.multiple_of(j * grp, grp)
        onehot = []
        for n in range(MOBA_GROUP):
            tile = jnp.where(tile_lane == j * MOBA_GROUP + n, 1.0, 0.0).astype(BF16)
            onehot += [tile] * (blk // tile_rows)
        k_aug = jnp.concatenate([k_ref[0, pl.ds(st, grp), :], jnp.concatenate(onehot, axis=0)], axis=1)
        raw = [_dot_nt(q_aug, k_aug[u * sub:(u + 1) * sub, :]) for u in range(MOBA_SPLIT)]
        for u in range(MOBA_SPLIT):
            m_new = jnp.maximum(m, jnp.max(raw[u], axis=1, keepdims=True))
            alpha = jnp.exp2((m - m_new) * c_exp)
            p = jnp.exp2((raw[u] - m_new) * c_exp)
            l = l * alpha + jnp.sum(p, axis=1, keepdims=True)
            acc = acc * alpha + _dot(p.astype(BF16), v_ref[0, pl.ds(st + u * sub, sub), :])
            m = m_new
        return m, l, acc

    n_groups = lax.shift_right_logical(i * bpt + (bpt - 1) + (MOBA_GROUP - 1),
                                       MOBA_GROUP.bit_length() - 1)
    m, l, acc = lax.fori_loop(0, n_groups, body, (m, l, acc))
    o_ref[0] = (acc / l).astype(o_ref.dtype)


def _moba_prompt(q, kbf, vbf, kmean):
    b, s, _ = q.shape
    tq = MOBA_QTILE
    nb = s // MOBA_BLOCK
    assert s % tq == 0 and tq % MOBA_BLOCK == 0 and nb <= LANES and nb % MOBA_GROUP == 0
    return pl.pallas_call(
        _moba_prompt_kernel,
        grid=(b, MOBA_HEADS, s // tq),
        in_specs=[pl.BlockSpec((1, tq, HEAD_DIM), lambda bi, h, i: (bi, i, h)),
                  pl.BlockSpec((1, s, HEAD_DIM), lambda bi, h, i: (bi, 0, h)),
                  pl.BlockSpec((1, s, HEAD_DIM), lambda bi, h, i: (bi, 0, h)),
                  pl.BlockSpec((1, nb, HEAD_DIM), lambda bi, h, i: (bi, 0, h))],
        out_specs=pl.BlockSpec((1, tq, HEAD_DIM), lambda bi, h, i: (bi, i, h)),
        out_shape=jax.ShapeDtypeStruct((b, s, MOBA_WIDTH), BF16),
        compiler_params=pltpu.CompilerParams(
            dimension_semantics=("parallel", "parallel", "arbitrary"),
            vmem_limit_bytes=VMEM_LIMIT),
        name="moba_prompt",
    )(q, kbf, vbf, kmean)


def _gdn_sample_prep_kernel(x_ref, st_ref, ba_ref, cw_ref, arow_ref,
                            q_ref, k_ref, v_ref, beta_ref, dec_ref):
    t_len = x_ref.shape[0]
    ctx = GDN_CONV_W - 1

    def src(tt):
        return st_ref[tt + ctx] if tt < 0 else x_ref[tt]

    for t in range(t_len):
        conv = src(t - ctx) * cw_ref[0:1, :]
        for w in range(1, GDN_CONV_W):
            conv = conv + src(t - ctx + w) * cw_ref[w:w + 1, :]
        qs, ks, v = _conv_silu_norm(conv)
        q_ref[t] = jnp.concatenate(qs, axis=1)
        k_ref[t] = jnp.concatenate(ks, axis=1)
        v_ref[t] = v
        ba = ba_ref[t]
        beta_full = jax.nn.sigmoid(ba)
        dec_full = jnp.exp(-jnp.exp(arow_ref[0:1, :]) * _softplus(ba + arow_ref[1:2, :]))
        rows = ba.shape[0]
        beta_ref[t] = jnp.concatenate(
            [jnp.broadcast_to(beta_full[:, h:h + 1], (rows, HEAD_DIM)) for h in range(GDN_HEADS)], axis=1)
        dec_ref[t] = jnp.concatenate(
            [jnp.broadcast_to(dec_full[:, GDN_HEADS + h:GDN_HEADS + h + 1], (rows, HEAD_DIM))
             for h in range(GDN_HEADS)], axis=1)


def _gdn_sample_prep(x_t, st_t, ba_t, conv_w, arow):
    t_len, db, _ = x_t.shape
    bt = min(db, 32)
    assert db % bt == 0

    def spec(t, w):
        return pl.BlockSpec((t, bt, w), lambda i: (0, i, 0))

    out = jax.ShapeDtypeStruct((t_len, db, GDN_WIDTH), F32)
    return pl.pallas_call(
        _gdn_sample_prep_kernel,
        grid=(db // bt,),
        in_specs=[spec(t_len, GDN_CONV_CH), spec(GDN_CONV_W - 1, GDN_CONV_CH), spec(t_len, LANES),
                  pl.BlockSpec((GDN_CONV_W, GDN_CONV_CH), lambda i: (0, 0)),
                  pl.BlockSpec((2, LANES), lambda i: (0, 0))],
        out_specs=[spec(t_len, GDN_WIDTH)] * 5,
        out_shape=[out] * 5,
        compiler_params=pltpu.CompilerParams(dimension_semantics=("parallel",),
                                             vmem_limit_bytes=VMEM_LIMIT),
        name="gdn_sample_prep",
    )(x_t, st_t, ba_t, conv_w, arow)


def _gdn_sample_rec_kernel(s0_ref, qt_ref, kt_ref, v_ref, beta_ref, dec_ref, o_ref, s_ref):
    bt = s0_ref.shape[0]
    t_len = v_ref.shape[1]

    def per_batch(bi, carry):
        v, beta, dec = v_ref[bi], beta_ref[bi], dec_ref[bi]
        outs = []
        for h in range(GDN_HEADS):
            cols = slice(h * HEAD_DIM, (h + 1) * HEAD_DIM)
            s = s0_ref[bi, h]
            kt = kt_ref[bi, h]
            qt = qt_ref[bi, h]
            rows = []
            for t in range(t_len):
                s = s * dec[t:t + 1, cols]
                kcol = kt[:, t:t + 1]
                kv = jnp.sum(kcol * s, axis=0, keepdims=True)
                upd = (v[t:t + 1, cols] - kv) * beta[t:t + 1, cols]
                s = s + kcol * upd
                rows.append(jnp.sum(qt[:, t:t + 1] * s, axis=0, keepdims=True))
            s_ref[bi, h] = s
            outs.append(jnp.concatenate(rows, axis=0))
        o_ref[bi] = jnp.concatenate(outs, axis=1)
        return carry

    lax.fori_loop(0, bt, per_batch, 0)


def _gdn_sample_rec(s0, qt, kt, v, beta, dec):
    db, t_len, _ = v.shape
    bt = min(db, 8)
    assert db % bt == 0
    st_spec = pl.BlockSpec((bt, GDN_HEADS, HEAD_DIM, HEAD_DIM), lambda i: (i, 0, 0, 0))
    tr_spec = pl.BlockSpec((bt, GDN_HEADS, HEAD_DIM, t_len), lambda i: (i, 0, 0, 0))
    tok_spec = pl.BlockSpec((bt, t_len, GDN_WIDTH), lambda i: (i, 0, 0))
    return pl.pallas_call(
        _gdn_sample_rec_kernel,
        grid=(db // bt,),
        in_specs=[st_spec, tr_spec, tr_spec, tok_spec, tok_spec, tok_spec],
        out_specs=[tok_spec, st_spec],
        out_shape=[jax.ShapeDtypeStruct((db, t_len, GDN_WIDTH), F32),
                   jax.ShapeDtypeStruct(s0.shape, F32)],
        compiler_params=pltpu.CompilerParams(dimension_semantics=("parallel",),
                                             vmem_limit_bytes=VMEM_LIMIT),
        name="gdn_sample_rec",
    )(s0, qt, kt, v, beta, dec)


PAGES_PER_STEP = 16
PAGE_ROWS = PAGE_SIZE * MOBA_HEADS
PAGES_PER_BLOCK = MOBA_BLOCK // PAGE_SIZE
BLOCKS_PER_STEP = PAGES_PER_STEP // PAGES_PER_BLOCK
STEP_ROWS = PAGES_PER_STEP * PAGE_ROWS
HEAD_SHIFT = MOBA_HEADS.bit_length() - 1
assert 1 << HEAD_SHIFT == MOBA_HEADS and SUBLANES == 2 * MOBA_HEADS and PAGES_PER_BLOCK == 2


PAGE_RING = 3


def _moba_sample_kernel(pt_ref, q_ref, kn_ref, vn_ref, ck_ref, cv_ref, o_ref,
                        kwin_ref, vwin_ref, ksem, vsem,
                        kbuf_ref, vbuf_ref, kmean_ref, mem_ref, m_ref, l_ref, acc_ref):
    bp, g = pl.program_id(0), pl.program_id(1)
    n_batch = pl.num_programs(0) - 1
    n_steps = pl.num_programs(1)
    n_pages = n_steps * PAGES_PER_STEP
    lin = bp * n_steps + g

    def page_copies(step, ring_slot, stage):
        sb, sg = lax.div(step, n_steps), lax.rem(step, n_steps)
        if stage == 1:
            needed, row, cache, win, sem = sb < n_batch, sb, ck_ref, kwin_ref, ksem
        else:
            needed, row, cache, win, sem = sb > 0, sb - 1, cv_ref, vwin_ref, vsem
        base = jnp.where(needed, row, 0) * n_pages + sg * PAGES_PER_STEP
        return needed, [pltpu.make_async_copy(cache.at[pt_ref[base + pg]], win.at[ring_slot, pg],
                                              sem.at[ring_slot]) for pg in range(PAGES_PER_STEP)]

    def request(step, ring_slot):
        for stage in (1, 2):
            needed, copies = page_copies(step, ring_slot, stage)

            @pl.when(needed)
            def _():
                for c in copies:
                    c.start()

    ring_slot = lax.rem(lin, PAGE_RING)

    @pl.when(lin == 0)
    def _():
        for step in range(PAGE_RING - 1):
            request(jnp.int32(step), step)

    ahead = lin + (PAGE_RING - 1)

    @pl.when(ahead < (n_batch + 1) * n_steps)
    def _():
        request(ahead, lax.rem(ahead, PAGE_RING))

    for stage in (1, 2):
        needed, copies = page_copies(lin, ring_slot, stage)

        @pl.when(needed)
        def _():
            for c in copies:
                c.wait()

    rows = q_ref.shape[1]
    t_len = rows // MOBA_HEADS
    scale = HEAD_DIM ** -0.5
    t_shift = t_len.bit_length() - 1
    assert 1 << t_shift == t_len and kn_ref.shape[1] == rows
    step_lanes = BLOCKS_PER_STEP * MOBA_HEADS
    par = jnp.bitwise_and(bp, 1)
    row_i = lax.broadcasted_iota(jnp.int32, (rows, LANES), 0)
    lane_i = lax.broadcasted_iota(jnp.int32, (rows, LANES), 1)
    row_head = lax.shift_right_logical(row_i, t_shift)
    lane_head = jnp.bitwise_and(lane_i, MOBA_HEADS - 1)

    @pl.when(jnp.logical_and(bp == 0, g == 0))
    def _():
        kmean_ref[...] = jnp.zeros(kmean_ref.shape, F32)

    @pl.when(jnp.logical_and(bp > 0, g == 0))
    def _():
        q = q_ref[0]
        q16 = q.astype(BF16)
        n_cand = n_steps * step_lanes
        valid = jnp.logical_and(lane_head == row_head, lane_i < n_cand)
        member = _top3_members(_dot_nt_f32(q, kmean_ref[...]), valid)
        mem_ref[...] = jnp.where(member, 1.0, 0.0)
        pad = jnp.zeros((LANES - rows, HEAD_DIM), F32)
        kn = jnp.concatenate([kn_ref[0], pad], axis=0).astype(BF16)
        vn = jnp.concatenate([vn_ref[0], pad], axis=0).astype(BF16)
        s = _dot_nt(q16, kn) * scale
        ok = jnp.logical_and(lane_head == row_head, lane_i < rows)
        ok = jnp.logical_and(ok, lax.shift_right_logical(lane_i, HEAD_SHIFT)
                             <= jnp.bitwise_and(row_i, t_len - 1))
        s = jnp.where(ok, s, -jnp.inf)
        m = jnp.max(s, axis=1, keepdims=True)
        p = jnp.exp(s - m)
        m_ref[...] = jnp.broadcast_to(m, m_ref.shape)
        l_ref[...] = jnp.broadcast_to(jnp.sum(p, axis=1, keepdims=True), l_ref.shape)
        acc_ref[...] = _dot(p.astype(BF16), vn)

    @pl.when(bp > 0)
    def _():
        for pg in range(PAGES_PER_STEP):
            vbuf_ref[pg * PAGE_ROWS:(pg + 1) * PAGE_ROWS, :] = vwin_ref[ring_slot, pg].astype(BF16)
        member = mem_ref[...]
        blk_rows = PAGES_PER_BLOCK * PAGE_ROWS
        col_head = jnp.bitwise_and(lax.broadcasted_iota(jnp.int32, (rows, blk_rows), 1), MOBA_HEADS - 1)
        own_head = col_head == lax.shift_right_logical(
            lax.broadcasted_iota(jnp.int32, (rows, blk_rows), 0), t_shift)
        cand_blk = lax.shift_right_logical(lane_i, HEAD_SHIFT)
        bias = []
        for n in range(BLOCKS_PER_STEP):
            picked = jnp.sum(jnp.where(cand_blk == g * BLOCKS_PER_STEP + n, member, 0.0),
                             axis=1, keepdims=True) > 0.0
            bias.append(jnp.where(jnp.logical_and(own_head, picked), 0.0, -jnp.inf))
        s = _dot_nt(q_ref[0].astype(BF16), kbuf_ref[1 - par, g]) * scale + jnp.concatenate(bias, axis=1)
        m = m_ref[:, 0:1]
        m_new = jnp.maximum(m, jnp.max(s, axis=1, keepdims=True))
        alpha = jnp.exp(m - m_new)
        p = jnp.exp(s - m_new)
        l = l_ref[:, 0:1] * alpha + jnp.sum(p, axis=1, keepdims=True)
        acc = acc_ref[...] * alpha + _dot(p.astype(BF16), vbuf_ref[...])
        m_ref[...] = jnp.broadcast_to(m_new, m_ref.shape)
        l_ref[...] = jnp.broadcast_to(l, l_ref.shape)
        acc_ref[...] = acc

        @pl.when(g == n_steps - 1)
        def _():
            o_ref[0] = acc / l

    @pl.when(bp < n_batch)
    def _():
        low = lax.broadcasted_iota(jnp.int32, (SUBLANES, HEAD_DIM), 0) < MOBA_HEADS

        def block_mean(n):
            tot = None
            for r in range(PAGES_PER_BLOCK):
                pg = n * PAGES_PER_BLOCK + r
                page = kwin_ref[ring_slot, pg]
                kbuf_ref[par, g, pg * PAGE_ROWS:(pg + 1) * PAGE_ROWS, :] = page.astype(BF16)
                part = jnp.sum(page.reshape(PAGE_ROWS // SUBLANES, SUBLANES, HEAD_DIM), axis=0)
                tot = part if tot is None else tot + part
            return (tot + pltpu.roll(tot, MOBA_HEADS, 0)) * (1.0 / MOBA_BLOCK)

        tiles = [jnp.where(low, block_mean(2 * n), block_mean(2 * n + 1))
                 for n in range(BLOCKS_PER_STEP // 2)]
        kmean_ref[pl.ds(pl.multiple_of(g * step_lanes, step_lanes), step_lanes), :] = (
            jnp.concatenate(tiles, axis=0))


def _moba_sample(page_flat, q2, kn2, vn2, cache_k, cache_v, n_pages):
    db, rows, _ = q2.shape
    assert n_pages % PAGES_PER_STEP == 0
    n_steps = n_pages // PAGES_PER_STEP
    n_cand = n_steps * BLOCKS_PER_STEP * MOBA_HEADS
    assert rows <= LANES and n_cand <= LANES
    tok = pl.BlockSpec((1, rows, HEAD_DIM), lambda bp, g, pt: (jnp.maximum(bp - 1, 0), 0, 0))
    window = pltpu.VMEM((PAGE_RING, PAGES_PER_STEP, PAGE_ROWS, HEAD_DIM), F32)
    return pl.pallas_call(
        _moba_sample_kernel,
        grid_spec=pltpu.PrefetchScalarGridSpec(
            num_scalar_prefetch=1,
            grid=(db + 1, n_steps),
            in_specs=[tok, tok, tok, pl.BlockSpec(memory_space=pl.ANY),
                      pl.BlockSpec(memory_space=pl.ANY)],
            out_specs=tok,
            scratch_shapes=[window, window,
                            pltpu.SemaphoreType.DMA((PAGE_RING,)), pltpu.SemaphoreType.DMA((PAGE_RING,)),
                            pltpu.VMEM((2, n_steps, STEP_ROWS, HEAD_DIM), BF16),
                            pltpu.VMEM((STEP_ROWS, HEAD_DIM), BF16),
                            pltpu.VMEM((LANES, HEAD_DIM), F32),
                            pltpu.VMEM((rows, LANES), F32),
                            pltpu.VMEM((rows, LANES), F32), pltpu.VMEM((rows, LANES), F32),
                            pltpu.VMEM((rows, HEAD_DIM), F32)],
        ),
        out_shape=jax.ShapeDtypeStruct((db, rows, HEAD_DIM), F32),
        compiler_params=pltpu.CompilerParams(dimension_semantics=("arbitrary", "arbitrary"),
                                             vmem_limit_bytes=VMEM_LIMIT),
        name="moba_sample",
    )(page_flat, q2, kn2, vn2, cache_k, cache_v)


def _merge_kernel(og_ref, za_ref, om_ref, zb_ref, x_ref, gw_ref, wo_ref, nf_ref, y_ref):
    og = og_ref[...].astype(F32)
    parts = []
    for h in range(GDN_HEADS):
        oh = og[:, h * HEAD_DIM:(h + 1) * HEAD_DIM]
        parts.append(oh * lax.rsqrt(jnp.mean(oh * oh, axis=-1, keepdims=True) + NORM_EPS) * gw_ref[...])
    ga = jnp.concatenate(parts, axis=1) * _silu(za_ref[...].astype(F32))
    gb = om_ref[...].astype(F32) * _silu(zb_ref[...].astype(F32))
    cat = jnp.concatenate([ga, gb], axis=1).astype(BF16)
    hp = x_ref[...] + _dot(cat, wo_ref[...])
    y_ref[...] = hp * lax.rsqrt(jnp.mean(hp * hp, axis=-1, keepdims=True) + NORM_EPS) * nf_ref[...]


def _merge(og, za, om, zb, x2d, gdn_norm_w, w_out16, norm_f_w):
    m, d = x2d.shape
    tm = MERGE_TILE if m % MERGE_TILE == 0 else ROW_TILE
    assert m % tm == 0
    row = lambda i: (i, 0)
    full = lambda i: (0, 0)
    half = pl.BlockSpec((tm, GDN_WIDTH), row)
    return pl.pallas_call(
        _merge_kernel,
        grid=(m // tm,),
        in_specs=[half, half, half, half, pl.BlockSpec((tm, d), row),
                  pl.BlockSpec((1, HEAD_DIM), full), pl.BlockSpec(w_out16.shape, full),
                  pl.BlockSpec((1, d), full)],
        out_specs=pl.BlockSpec((tm, d), row),
        out_shape=jax.ShapeDtypeStruct((m, d), F32),
        compiler_params=pltpu.CompilerParams(dimension_semantics=("parallel",),
                                             vmem_limit_bytes=VMEM_LIMIT),
        name="merge_out",
    )(og, za, om, zb, x2d, gdn_norm_w, w_out16, norm_f_w)


def _rope_tables(pos):
    half = HEAD_DIM // 2
    inv_freq = ROPE_THETA ** (-jnp.arange(half, dtype=F32) / half)
    ang = pos.astype(F32)[:, None] * inv_freq[None, :]
    cos, sin = jnp.cos(ang), jnp.sin(ang)
    return jnp.concatenate([cos, cos], axis=1), jnp.concatenate([-sin, sin], axis=1)


def _relayout_w_in(w):
    o = 0
    parts = {}
    for name, n in (("qkv", GDN_CONV_CH), ("za", GDN_WIDTH), ("b", GDN_HEADS), ("a", GDN_HEADS),
                    ("qb", MOBA_WIDTH), ("kb", MOBA_WIDTH), ("vb", MOBA_WIDTH), ("zb", MOBA_WIDTH)):
        parts[name] = w[:, o:o + n]
        o += n
    pad = jnp.zeros((w.shape[0], LANES - 2 * GDN_HEADS), w.dtype)
    return jnp.concatenate([parts["qkv"], parts["za"], parts["qb"], parts["kb"], parts["vb"],
                            parts["zb"], parts["b"], parts["a"], pad], axis=1).astype(BF16)


def kernel(x_prompt, x_sample, cache_k, cache_v, state_gdn, state_conv, page_table, norm_in_w, w_in,
           conv_w, a_log, dt_bias, gdn_norm_w, w_out, norm_f_w):
    b, s, d = x_prompt.shape
    db, t_len, _ = x_sample.shape
    depth = w_in.shape[0]
    assert depth == 1
    n_pages = page_table.shape[1]
    past_len = n_pages * PAGE_SIZE
    assert past_len % MOBA_BLOCK == 0
    assert t_len <= PAGE_SIZE and t_len >= GDN_CONV_W - 1 and (db * t_len) % ROW_TILE == 0
    assert ROW_TILE % t_len == 0

    w_p = _relayout_w_in(w_in[0])
    w_o = w_out[0].astype(BF16)
    nw = norm_in_w[0][None, :]
    cw = conv_w[0]
    zeros4 = jnp.zeros((GDN_HEADS,), F32)
    lane_pad = jnp.zeros((LANES - 2 * GDN_HEADS,), F32)
    arow = jnp.stack([jnp.concatenate([zeros4, a_log[0], lane_pad]),
                      jnp.concatenate([zeros4, dt_bias[0], lane_pad])])
    acol = jnp.stack([jnp.concatenate([zeros4, a_log[0]]),
                      jnp.concatenate([zeros4, dt_bias[0]])])[:, :, None]
    gw = gdn_norm_w[0][None, :]
    nf = norm_f_w[None, :]

    cos_p, sin_p = _rope_tables(jnp.arange(s, dtype=jnp.int32))
    (qkv_p, za_p, q_p, k_p, v_p, zb_p, ba_p, kbf_p, vbf_p, kmean_p) = _inproj(
        x_prompt.reshape(b * s, d), nw, w_p, cos_p, sin_p, prompt=True)
    qkv_p3 = qkv_p.reshape(b, s, GDN_CONV_CH)
    ba_p3 = ba_p.reshape(b, s, LANES)
    bat_p = jnp.transpose(ba_p3[:, :, :SUBLANES], (0, 2, 1))
    n_m, intra, rhs, qdec, kdec, glast = _gdn_prep(qkv_p3, ba_p3, bat_p, cw, arow, acol)
    bh, nc, c, _ = n_m.shape
    t_m = _tri_inv(jnp.transpose(n_m.reshape(bh * nc, c, c), (1, 2, 0)))
    t_m = jnp.transpose(t_m, (2, 0, 1)).reshape(bh, nc, c, c)
    o_gdn_p, s_gdn_p = _gdn_scan(t_m, rhs, qdec, kdec, intra, glast, b)
    o_moba_p = _moba_prompt(q_p.reshape(b, s, MOBA_WIDTH), kbf_p.reshape(b, s, MOBA_WIDTH),
                            vbf_p.reshape(b, s, MOBA_WIDTH),
                            kmean_p.reshape(b, s // MOBA_BLOCK, MOBA_WIDTH))
    y_p = _merge(o_gdn_p.reshape(b * s, GDN_WIDTH), za_p, o_moba_p.reshape(b * s, MOBA_WIDTH), zb_p,
                 x_prompt.reshape(b * s, d), gw, w_o, nf)

    cos_s, sin_s = _rope_tables(past_len + jnp.arange(t_len, dtype=jnp.int32))
    reps = ROW_TILE // t_len
    (qkv_s, za_s, q_s, k_s, v_s, zb_s, ba_s) = _inproj(
        x_sample.reshape(db * t_len, d), nw, w_p, jnp.tile(cos_s, (reps, 1)), jnp.tile(sin_s, (reps, 1)),
        prompt=False)
    qkv_s3 = qkv_s.reshape(db, t_len, GDN_CONV_CH)
    tb = lambda a: jnp.transpose(a, (1, 0, 2))
    qg, kg, vg, beta_t, dec_t = _gdn_sample_prep(
        tb(qkv_s3), tb(state_conv[0]), tb(ba_s.reshape(db, t_len, LANES)), cw, arow)
    to_cols = lambda a: jnp.transpose(a.reshape(t_len, db, GDN_HEADS, HEAD_DIM), (1, 2, 3, 0))
    o_gdn_s, s_gdn_s = _gdn_sample_rec(state_gdn[0], to_cols(qg), to_cols(kg), tb(vg), tb(beta_t),
                                       tb(dec_t))

    page_flat = page_table.reshape(-1).astype(jnp.int32)
    ck = cache_k.reshape(cache_k.shape[1], PAGE_ROWS, HEAD_DIM)
    cv = cache_v.reshape(cache_v.shape[1], PAGE_ROWS, HEAD_DIM)
    rows_s = t_len * MOBA_HEADS
    q2 = jnp.transpose(q_s.reshape(db, t_len, MOBA_HEADS, HEAD_DIM), (0, 2, 1, 3))
    o2 = _moba_sample(page_flat, q2.reshape(db, rows_s, HEAD_DIM), k_s.reshape(db, rows_s, HEAD_DIM),
                      v_s.reshape(db, rows_s, HEAD_DIM), ck, cv, n_pages)
    o_moba_s = jnp.transpose(o2.reshape(db, MOBA_HEADS, t_len, HEAD_DIM), (0, 2, 1, 3))
    y_s = _merge(o_gdn_s.reshape(db * t_len, GDN_WIDTH), za_s, o_moba_s.reshape(db * t_len, MOBA_WIDTH),
                 zb_s, x_sample.reshape(db * t_len, d), gw, w_o, nf)

    ctx = GDN_CONV_W - 1
    heads = lambda a, n, t: a.reshape(1, n, t, MOBA_HEADS, HEAD_DIM)
    return (y_p.reshape(b, s, d), y_s.reshape(db, t_len, d),
            heads(k_p, b, s), heads(v_p, b, s),
            s_gdn_p.reshape(1, b, GDN_HEADS, HEAD_DIM, HEAD_DIM),
            qkv_p3[:, s - ctx:, :][None],
            heads(k_s, db, t_len), heads(v_s, db, t_len),
            s_gdn_s[None],
            qkv_s3[:, t_len - ctx:, :][None])
```

```python
import functools
import math

import jax
import jax.numpy as jnp
from jax import lax
from jax.experimental import pallas as pl
from jax.experimental.pallas import tpu as pltpu

F32 = jnp.float32
BF16 = jnp.bfloat16

HEAD_DIM = 128
GDN_HEADS = 4
MOBA_HEADS = 4
GDN_QK = GDN_HEADS * HEAD_DIM
GDN_WIDTH = GDN_HEADS * HEAD_DIM
GDN_CONV_W = 4
GDN_CONV_CH = 2 * GDN_QK + GDN_WIDTH
GDN_CHUNK = 64
MOBA_WIDTH = MOBA_HEADS * HEAD_DIM
MOBA_BLOCK = 256
MOBA_TOPK = 3
PAGE_SIZE = 128
ROPE_THETA = 10000.0
NORM_EPS = 1e-6
LANES = 128
SUBLANES = 8
MASK_BIAS = -1e30

C_QKV = 0
C_ZA = GDN_CONV_CH
C_QB = C_ZA + GDN_WIDTH
C_KB = C_QB + MOBA_WIDTH
C_VB = C_KB + MOBA_WIDTH
C_ZB = C_VB + MOBA_WIDTH
C_BA = C_ZB + MOBA_WIDTH
IN_COLS_PAD = C_BA + LANES

ROW_TILE = 256
MERGE_TILE = 1024
VMEM_LIMIT = 56 * 1024 * 1024


def _dot(a, b):
    return jnp.dot(a, b, preferred_element_type=F32)


def _dot_nt(a, b):
    return lax.dot_general(a, b, (((1,), (1,)), ((), ())), preferred_element_type=F32)


def _dot_tn(a, b):
    return lax.dot_general(a, b, (((0,), (0,)), ((), ())), preferred_element_type=F32)


def _dot_nt_f32(a, b):
    return lax.dot_general(a, b, (((1,), (1,)), ((), ())), preferred_element_type=F32,
                           precision=lax.Precision.HIGHEST)


def _dot_f32(a, b):
    return jnp.dot(a, b, preferred_element_type=F32, precision=lax.Precision.HIGHEST)


def _split_bf16(a):
    hi = a.astype(BF16)
    lo = (a - hi.astype(F32)).astype(BF16)
    return hi, lo


def _dot3(a, b):
    ah, al = _split_bf16(a)
    bh, bl = _split_bf16(b)
    return _dot(ah, bh) + (_dot(ah, bl) + _dot(al, bh))


def _dot3_nt(a, b):
    ah, al = _split_bf16(a)
    bh, bl = _split_bf16(b)
    return _dot_nt(ah, bh) + (_dot_nt(ah, bl) + _dot_nt(al, bh))


def _silu(x):
    return x * jax.nn.sigmoid(x)


def _softplus(x):
    return jnp.maximum(x, 0.0) + jnp.log1p(jnp.exp(-jnp.abs(x)))


def _top3_members(gate, valid, axis=1):
    cand = lax.broadcasted_iota(jnp.int32, gate.shape, axis).astype(F32)
    g = jnp.where(valid, gate, -jnp.inf)
    member = jnp.zeros(gate.shape, dtype=jnp.bool_)
    for _ in range(MOBA_TOPK):
        m = jnp.max(g, axis=axis, keepdims=True)
        idx = jnp.min(jnp.where(g == m, cand, float(LANES)), axis=axis, keepdims=True)
        pick = cand == idx
        member = jnp.logical_or(member, pick)
        g = jnp.where(pick, -jnp.inf, g)
    return jnp.logical_and(member, valid)


def _inproj_kernel(x_ref, nw_ref, w_ref, cos_ref, sin_ref,
                   qkv_ref, za_ref, q_ref, k_ref, v_ref, zb_ref, ba_ref, *prompt_refs):
    x = x_ref[...]
    ms = jnp.mean(x * x, axis=-1, keepdims=True)
    hb = (x * lax.rsqrt(ms + NORM_EPS) * nw_ref[...]).astype(BF16)

    def proj(c0, n):
        return _dot(hb, w_ref[:, c0:c0 + n])

    def rope(t):
        parts = []
        for h in range(MOBA_HEADS):
            th = t[:, h * HEAD_DIM:(h + 1) * HEAD_DIM]
            parts.append(th * cos_ref[...] + pltpu.roll(th, HEAD_DIM // 2, 1) * sin_ref[...])
        return jnp.concatenate(parts, axis=1)

    for j in range(GDN_CONV_CH // GDN_QK):
        qkv_ref[:, j * GDN_QK:(j + 1) * GDN_QK] = proj(C_QKV + j * GDN_QK, GDN_QK)
    za_ref[...] = proj(C_ZA, GDN_WIDTH).astype(za_ref.dtype)
    q_ref[...] = rope(proj(C_QB, MOBA_WIDTH))
    k = rope(proj(C_KB, MOBA_WIDTH))
    k_ref[...] = k
    v = proj(C_VB, MOBA_WIDTH)
    v_ref[...] = v
    zb_ref[...] = proj(C_ZB, MOBA_WIDTH).astype(zb_ref.dtype)
    ba_ref[...] = proj(C_BA, LANES)
    if prompt_refs:
        kbf_ref, vbf_ref, kmean_ref = prompt_refs
        kbf_ref[...] = k.astype(BF16)
        vbf_ref[...] = v.astype(BF16)
        kmean_ref[0] = jnp.mean(k, axis=0, keepdims=True)


def _inproj(x2d, norm_w, w_p, cosf, sinf, *, prompt):
    m, d = x2d.shape
    tm = ROW_TILE
    assert m % tm == 0 and cosf.shape[0] % tm == 0
    n_pos = cosf.shape[0] // tm
    row = lambda i: (i, 0)
    full = lambda i: (0, 0)
    pos = (lambda i: (i % n_pos, 0)) if n_pos > 1 else full
    out_shape = [jax.ShapeDtypeStruct((m, GDN_CONV_CH), F32)]
    out_specs = [pl.BlockSpec((tm, GDN_CONV_CH), row)]
    for dt in (BF16, F32, F32, F32, BF16):
        out_shape.append(jax.ShapeDtypeStruct((m, MOBA_WIDTH), dt))
        out_specs.append(pl.BlockSpec((tm, MOBA_WIDTH), row))
    out_shape.append(jax.ShapeDtypeStruct((m, LANES), F32))
    out_specs.append(pl.BlockSpec((tm, LANES), row))
    if prompt:
        assert tm == MOBA_BLOCK
        for _ in range(2):
            out_shape.append(jax.ShapeDtypeStruct((m, MOBA_WIDTH), BF16))
            out_specs.append(pl.BlockSpec((tm, MOBA_WIDTH), row))
        out_shape.append(jax.ShapeDtypeStruct((m // tm, 1, MOBA_WIDTH), F32))
        out_specs.append(pl.BlockSpec((1, 1, MOBA_WIDTH), lambda i: (i, 0, 0)))
    return pl.pallas_call(
        _inproj_kernel,
        grid=(m // tm,),
        in_specs=[pl.BlockSpec((tm, d), row), pl.BlockSpec((1, d), full),
                  pl.BlockSpec((d, IN_COLS_PAD), full),
                  pl.BlockSpec((tm, HEAD_DIM), pos), pl.BlockSpec((tm, HEAD_DIM), pos)],
        out_specs=out_specs,
        out_shape=out_shape,
        compiler_params=pltpu.CompilerParams(dimension_semantics=("parallel",),
                                             vmem_limit_bytes=VMEM_LIMIT),
        name="inproj_prompt" if prompt else "inproj_sample",
    )(x2d, norm_w, w_p, cosf, sinf)


def _conv_silu_norm(c):
    c = _silu(c)
    qs, ks = [], []
    for h in range(GDN_HEADS):
        qh = c[:, h * HEAD_DIM:(h + 1) * HEAD_DIM]
        kh = c[:, GDN_QK + h * HEAD_DIM:GDN_QK + (h + 1) * HEAD_DIM]
        qs.append(qh * (lax.rsqrt(jnp.sum(qh * qh, axis=-1, keepdims=True) + NORM_EPS)
                        * (HEAD_DIM ** -0.5)))
        ks.append(kh * lax.rsqrt(jnp.sum(kh * kh, axis=-1, keepdims=True) + NORM_EPS))
    return qs, ks, c[:, 2 * GDN_QK:]


def _gdn_prep_kernel(x_ref, halo_ref, ba_ref, bat_ref, cw_ref, arow_ref, acol_ref,
                     n_ref, intra_ref, rhs_ref, qdec_ref, kdec_ref, glast_ref, buf_ref):
    i = pl.program_id(1)
    tt = x_ref.shape[1]
    c = GDN_CHUNK
    halo = jnp.where(i > 0, halo_ref[0], 0.0)
    buf_ref[0:SUBLANES, :] = halo
    buf_ref[SUBLANES:SUBLANES + tt, :] = x_ref[0]
    full = buf_ref[...]
    conv = full[SUBLANES:, :] * cw_ref[GDN_CONV_W - 1:GDN_CONV_W, :]
    for back in range(1, GDN_CONV_W):
        w = GDN_CONV_W - 1 - back
        conv = conv + pltpu.roll(full, back, 0)[SUBLANES:, :] * cw_ref[w:w + 1, :]
    qs, ks, v = _conv_silu_norm(conv)

    ba = ba_ref[0]
    beta_full = jax.nn.sigmoid(ba)
    g_full = -jnp.exp(arow_ref[0:1, :]) * _softplus(ba + arow_ref[1:2, :])
    bat = bat_ref[0]
    gt_full = -jnp.exp(acol_ref[0]) * _softplus(bat + acol_ref[1])

    ri = lax.broadcasted_iota(jnp.int32, (c, c), 0)
    ci = lax.broadcasted_iota(jnp.int32, (c, c), 1)
    tril = ri >= ci
    strict = ri > ci
    ltri = jnp.where(tril, 1.0, 0.0).astype(F32)
    utri = jnp.where(ri <= ci, 1.0, 0.0).astype(F32)

    for cc in range(tt // c):
        rows = slice(cc * c, (cc + 1) * c)
        gcol_all = _dot_f32(ltri, g_full[rows, :])
        grow_all = _dot_f32(gt_full[:, rows], utri)
        for h in range(GDN_HEADS):
            gc_c = gcol_all[:, GDN_HEADS + h:GDN_HEADS + h + 1]
            gc_r = grow_all[GDN_HEADS + h:GDN_HEADS + h + 1, :]
            gamma = jnp.where(tril, jnp.exp(jnp.where(tril, gc_c - gc_r, 0.0)), 0.0)
            beta = beta_full[rows, h:h + 1]
            qh, kh = qs[h][rows, :], ks[h][rows, :]
            vh = v[rows, h * HEAD_DIM:(h + 1) * HEAD_DIM]
            kb = kh * beta
            k16 = kh.astype(BF16)
            n_ref[h, cc] = jnp.where(strict, _dot_nt(kb.astype(BF16), k16) * gamma, 0.0)
            intra_ref[h, cc] = jnp.where(tril, _dot_nt(qh.astype(BF16), k16) * gamma, 0.0)
            egc = jnp.exp(gc_c)
            rhs_ref[h, cc, :, 0:HEAD_DIM] = vh * beta
            rhs_ref[h, cc, :, HEAD_DIM:2 * HEAD_DIM] = kb * egc
            qdec_ref[h, cc] = qh * egc
            gl = gc_c[c - 1:c, :]
            kdec_ref[h, cc] = kh * jnp.exp(gl - gc_c)
            glast_ref[h, cc] = jnp.broadcast_to(jnp.exp(gl), (SUBLANES, LANES))


def _gdn_prep(qkv, ba, bat, conv_w, arow, acol):
    b, s, _ = qkv.shape
    tt = ROW_TILE
    c = GDN_CHUNK
    cpt = tt // c
    nc = s // c
    hb = tt // SUBLANES
    bh = b * GDN_HEADS

    def o(x):
        return (jax.ShapeDtypeStruct((bh, nc, c, x), F32),
                pl.BlockSpec((GDN_HEADS, cpt, c, x), lambda bi, i: (bi, i, 0, 0)))

    outs = [o(c), o(c), o(2 * HEAD_DIM), o(HEAD_DIM), o(HEAD_DIM)]
    outs.append((jax.ShapeDtypeStruct((bh, nc, SUBLANES, LANES), F32),
                 pl.BlockSpec((GDN_HEADS, cpt, SUBLANES, LANES), lambda bi, i: (bi, i, 0, 0))))
    return pl.pallas_call(
        _gdn_prep_kernel,
        grid=(b, s // tt),
        in_specs=[
            pl.BlockSpec((1, tt, GDN_CONV_CH), lambda bi, i: (bi, i, 0)),
            pl.BlockSpec((1, SUBLANES, GDN_CONV_CH), lambda bi, i: (bi, jnp.maximum(i * hb - 1, 0), 0)),
            pl.BlockSpec((1, tt, LANES), lambda bi, i: (bi, i, 0)),
            pl.BlockSpec((1, SUBLANES, tt), lambda bi, i: (bi, 0, i)),
            pl.BlockSpec((GDN_CONV_W, GDN_CONV_CH), lambda bi, i: (0, 0)),
            pl.BlockSpec((2, LANES), lambda bi, i: (0, 0)),
            pl.BlockSpec((2, SUBLANES, 1), lambda bi, i: (0, 0, 0)),
        ],
        out_specs=[x[1] for x in outs],
        out_shape=[x[0] for x in outs],
        scratch_shapes=[pltpu.VMEM((SUBLANES + tt, GDN_CONV_CH), F32)],
        compiler_params=pltpu.CompilerParams(dimension_semantics=("parallel", "parallel"),
                                             vmem_limit_bytes=VMEM_LIMIT),
        name="gdn_prep",
    )(qkv, qkv, ba, bat, conv_w, arow, acol)


def _tri_inv_kernel(n_ref, t_ref):
    c = n_ref.shape[0]
    t_ref[...] = jnp.zeros(t_ref.shape, F32)

    for ig in range(c // SUBLANES):
        width = (ig + 1) * SUBLANES
        col = lax.broadcasted_iota(jnp.int32, (width, LANES), 0)

        def outer(i, carry, width=width, col=col):
            def inner(jg, acc):
                j0 = pl.multiple_of(jg * SUBLANES, SUBLANES)
                n_rows = n_ref[i, pl.ds(j0, SUBLANES), :]
                for r in range(SUBLANES):
                    acc = acc - n_rows[r:r + 1, :] * t_ref[j0 + r, 0:width, :]
                return acc
            groups = lax.shift_right_logical(i + (SUBLANES - 1), SUBLANES.bit_length() - 1)
            acc = lax.fori_loop(0, groups, inner, jnp.where(col == i, 1.0, 0.0).astype(F32))
            t_ref[i, 0:width, :] = acc
            return carry

        lax.fori_loop(ig * SUBLANES, (ig + 1) * SUBLANES, outer, 0)


def _tri_inv(nt):
    c, _, nmat = nt.shape
    assert nmat % LANES == 0
    spec = pl.BlockSpec((c, c, LANES), lambda g: (0, 0, g))
    return pl.pallas_call(
        _tri_inv_kernel,
        grid=(nmat // LANES,),
        in_specs=[spec],
        out_specs=spec,
        out_shape=jax.ShapeDtypeStruct(nt.shape, F32),
        compiler_params=pltpu.CompilerParams(dimension_semantics=("parallel",)),
        name="gdn_tri_inv",
    )(nt)


def _gdn_scan_kernel(t_ref, rhs_ref, qdec_ref, kdec_ref, intra_ref, glast_ref, o_ref, s_ref):
    i = pl.program_id(0)
    bh, ct, c, _ = t_ref.shape

    @pl.when(i == 0)
    def _():
        s_ref[...] = jnp.zeros(s_ref.shape, F32)

    heads = range(bh)
    for cc in range(ct):
        sol = [_dot3(t_ref[n, cc], rhs_ref[n, cc]) for n in heads]
        wqs = [_dot(jnp.concatenate([sol[n][:, HEAD_DIM:], qdec_ref[n, cc]], axis=0).astype(BF16),
                    s_ref[n].astype(BF16)) for n in heads]
        vn16 = [(sol[n][:, :HEAD_DIM] - wqs[n][:c]).astype(BF16) for n in heads]
        for n in heads:
            s_ref[n] = (s_ref[n] * glast_ref[n, cc, 0:1, :]
                        + _dot_tn(kdec_ref[n, cc].astype(BF16), vn16[n]))
        for n in heads:
            b, h = n // GDN_HEADS, n % GDN_HEADS
            o_ref[b, cc * c:(cc + 1) * c, h * HEAD_DIM:(h + 1) * HEAD_DIM] = (
                wqs[n][c:] + _dot(intra_ref[n, cc].astype(BF16), vn16[n])).astype(o_ref.dtype)


def _gdn_scan(t, rhs, qdec, kdec, intra, glast, b):
    bh, nc, c, _ = t.shape
    ct = 8
    assert nc % ct == 0

    def spec(x):
        return pl.BlockSpec((bh, ct, x.shape[2], x.shape[3]), lambda i: (0, i, 0, 0))

    return pl.pallas_call(
        _gdn_scan_kernel,
        grid=(nc // ct,),
        in_specs=[spec(t), spec(rhs), spec(qdec), spec(kdec), spec(intra), spec(glast)],
        out_specs=[pl.BlockSpec((b, ct * c, GDN_WIDTH), lambda i: (0, i, 0)),
                   pl.BlockSpec((bh, HEAD_DIM, HEAD_DIM), lambda i: (0, 0, 0))],
        out_shape=[jax.ShapeDtypeStruct((b, nc * c, GDN_WIDTH), BF16),
                   jax.ShapeDtypeStruct((bh, HEAD_DIM, HEAD_DIM), F32)],
        compiler_params=pltpu.CompilerParams(dimension_semantics=("arbitrary",),
                                             vmem_limit_bytes=VMEM_LIMIT),
        name="gdn_scan",
    )(t, rhs, qdec, kdec, intra, glast)


MOBA_GROUP = 8
MOBA_SPLIT = 2
MOBA_QTILE = MOBA_BLOCK


def _moba_prompt_kernel(q_ref, k_ref, v_ref, kmean_ref, o_ref):
    i = pl.program_id(2)
    tq = q_ref.shape[1]
    blk = MOBA_BLOCK
    scale = HEAD_DIM ** -0.5
    q = q_ref[0]
    q16 = q.astype(BF16)
    kmean = jnp.concatenate(
        [kmean_ref[0], jnp.zeros((LANES - kmean_ref.shape[1], HEAD_DIM), F32)], axis=0)
    bpt = tq // blk
    blk_shift = blk.bit_length() - 1
    own_blk = i * bpt + lax.shift_right_logical(
        lax.broadcasted_iota(jnp.int32, (LANES, tq), 1), blk_shift)
    past = lax.broadcasted_iota(jnp.int32, (LANES, tq), 0) < own_blk
    member = _top3_members(_dot3_nt(kmean, q), past, axis=0)
    bias = jnp.transpose(jnp.where(member, 0.0, MASK_BIAS))
    q_aug = jnp.concatenate([q16, bias.astype(BF16)], axis=1)
    grp = MOBA_GROUP * blk
    c_exp = scale * math.log2(math.e)
    tile_rows = 2 * SUBLANES
    tile_lane = lax.broadcasted_iota(jnp.int32, (tile_rows, LANES), 1)

    rq = lax.broadcasted_iota(jnp.int32, (blk, blk), 0)
    ck = lax.broadcasted_iota(jnp.int32, (blk, blk), 1)
    ms, ls, accs = [], [], []
    for r in range(bpt):
        start = pl.multiple_of((i * bpt + r) * blk, blk)
        s = _dot_nt(q16[r * blk:(r + 1) * blk, :], k_ref[0, pl.ds(start, blk), :])
        s = jnp.where(ck <= rq, s, -jnp.inf)
        m_r = jnp.max(s, axis=1, keepdims=True)
        p = jnp.exp2((s - m_r) * c_exp)
        ms.append(m_r)
        ls.append(jnp.sum(p, axis=1, keepdims=True))
        accs.append(_dot(p.astype(BF16), v_ref[0, pl.ds(start, blk), :]))
    m, l, acc = (jnp.concatenate(x, axis=0) for x in (ms, ls, accs))

    sub = grp // MOBA_SPLIT

    def body(j, carry):
        m, l, acc = carry
        st = pl.multiple_of(j * grp, grp)
        onehot = []
        for n in range(MOBA_GROUP):
            tile = jnp.where(tile_lane == j * MOBA_GROUP + n, 1.0, 0.0).astype(BF16)
            onehot += [tile] * (blk // tile_rows)
        k_aug = jnp.concatenate([k_ref[0, pl.ds(st, grp), :], jnp.concatenate(onehot, axis=0)], axis=1)
        raw = [_dot_nt(q_aug, k_aug[u * sub:(u + 1) * sub, :]) for u in range(MOBA_SPLIT)]
        for u in range(MOBA_SPLIT):
            m_new = jnp.maximum(m, jnp.max(raw[u], axis=1, keepdims=True))
            alpha = jnp.exp2((m - m_new) * c_exp)
            p = jnp.exp2((raw[u] - m_new) * c_exp)
            l = l * alpha + jnp.sum(p, axis=1, keepdims=True)
            acc = acc * alpha + _dot(p.astype(BF16), v_ref[0, pl.ds(st + u * sub, sub), :])
            m = m_new
        return m, l, acc

    n_groups = lax.shift_right_logical(i * bpt + (bpt - 1) + (MOBA_GROUP - 1),
                                       MOBA_GROUP.bit_length() - 1)
    m, l, acc = lax.fori_loop(0, n_groups, body, (m, l, acc))
    o_ref[0] = (acc / l).astype(o_ref.dtype)


def _moba_prompt(q, kbf, vbf, kmean):
    b, s, _ = q.shape
    tq = MOBA_QTILE
    nb = s // MOBA_BLOCK
    assert s % tq == 0 and tq % MOBA_BLOCK == 0 and nb <= LANES and nb % MOBA_GROUP == 0
    return pl.pallas_call(
        _moba_prompt_kernel,
        grid=(b, MOBA_HEADS, s // tq),
        in_specs=[pl.BlockSpec((1, tq, HEAD_DIM), lambda bi, h, i: (bi, i, h)),
                  pl.BlockSpec((1, s, HEAD_DIM), lambda bi, h, i: (bi, 0, h)),
                  pl.BlockSpec((1, s, HEAD_DIM), lambda bi, h, i: (bi, 0, h)),
                  pl.BlockSpec((1, nb, HEAD_DIM), lambda bi, h, i: (bi, 0, h))],
        out_specs=pl.BlockSpec((1, tq, HEAD_DIM), lambda bi, h, i: (bi, i, h)),
        out_shape=jax.ShapeDtypeStruct((b, s, MOBA_WIDTH), BF16),
        compiler_params=pltpu.CompilerParams(
            dimension_semantics=("parallel", "parallel", "arbitrary"),
            vmem_limit_bytes=VMEM_LIMIT),
        name="moba_prompt",
    )(q, kbf, vbf, kmean)


def _gdn_sample_prep_kernel(x_ref, st_ref, ba_ref, cw_ref, arow_ref,
                            q_ref, k_ref, v_ref, beta_ref, dec_ref):
    t_len = x_ref.shape[0]
    ctx = GDN_CONV_W - 1

    def src(tt):
        return st_ref[tt + ctx] if tt < 0 else x_ref[tt]

    for t in range(t_len):
        conv = src(t - ctx) * cw_ref[0:1, :]
        for w in range(1, GDN_CONV_W):
            conv = conv + src(t - ctx + w) * cw_ref[w:w + 1, :]
        qs, ks, v = _conv_silu_norm(conv)
        q_ref[t] = jnp.concatenate(qs, axis=1)
        k_ref[t] = jnp.concatenate(ks, axis=1)
        v_ref[t] = v
        ba = ba_ref[t]
        beta_full = jax.nn.sigmoid(ba)
        dec_full = jnp.exp(-jnp.exp(arow_ref[0:1, :]) * _softplus(ba + arow_ref[1:2, :]))
        rows = ba.shape[0]
        beta_ref[t] = jnp.concatenate(
            [jnp.broadcast_to(beta_full[:, h:h + 1], (rows, HEAD_DIM)) for h in range(GDN_HEADS)], axis=1)
        dec_ref[t] = jnp.concatenate(
            [jnp.broadcast_to(dec_full[:, GDN_HEADS + h:GDN_HEADS + h + 1], (rows, HEAD_DIM))
             for h in range(GDN_HEADS)], axis=1)


def _gdn_sample_prep(x_t, st_t, ba_t, conv_w, arow):
    t_len, db, _ = x_t.shape
    bt = min(db, 32)
    assert db % bt == 0

    def spec(t, w):
        return pl.BlockSpec((t, bt, w), lambda i: (0, i, 0))

    out = jax.ShapeDtypeStruct((t_len, db, GDN_WIDTH), F32)
    return pl.pallas_call(
        _gdn_sample_prep_kernel,
        grid=(db // bt,),
        in_specs=[spec(t_len, GDN_CONV_CH), spec(GDN_CONV_W - 1, GDN_CONV_CH), spec(t_len, LANES),
                  pl.BlockSpec((GDN_CONV_W, GDN_CONV_CH), lambda i: (0, 0)),
                  pl.BlockSpec((2, LANES), lambda i: (0, 0))],
        out_specs=[spec(t_len, GDN_WIDTH)] * 5,
        out_shape=[out] * 5,
        compiler_params=pltpu.CompilerParams(dimension_semantics=("parallel",),
                                             vmem_limit_bytes=VMEM_LIMIT),
        name="gdn_sample_prep",
    )(x_t, st_t, ba_t, conv_w, arow)


def _gdn_sample_rec_kernel(s0_ref, qt_ref, kt_ref, v_ref, beta_ref, dec_ref, o_ref, s_ref):
    bt = s0_ref.shape[0]
    t_len = v_ref.shape[1]

    def per_batch(bi, carry):
        v, beta, dec = v_ref[bi], beta_ref[bi], dec_ref[bi]
        outs = []
        for h in range(GDN_HEADS):
            cols = slice(h * HEAD_DIM, (h + 1) * HEAD_DIM)
            s = s0_ref[bi, h]
            kt = kt_ref[bi, h]
            qt = qt_ref[bi, h]
            rows = []
            for t in range(t_len):
                s = s * dec[t:t + 1, cols]
                kcol = kt[:, t:t + 1]
                kv = jnp.sum(kcol * s, axis=0, keepdims=True)
                upd = (v[t:t + 1, cols] - kv) * beta[t:t + 1, cols]
                s = s + kcol * upd
                rows.append(jnp.sum(qt[:, t:t + 1] * s, axis=0, keepdims=True))
            s_ref[bi, h] = s
            outs.append(jnp.concatenate(rows, axis=0))
        o_ref[bi] = jnp.concatenate(outs, axis=1)
        return carry

    lax.fori_loop(0, bt, per_batch, 0)


def _gdn_sample_rec(s0, qt, kt, v, beta, dec):
    db, t_len, _ = v.shape
    bt = min(db, 8)
    assert db % bt == 0
    st_spec = pl.BlockSpec((bt, GDN_HEADS, HEAD_DIM, HEAD_DIM), lambda i: (i, 0, 0, 0))
    tr_spec = pl.BlockSpec((bt, GDN_HEADS, HEAD_DIM, t_len), lambda i: (i, 0, 0, 0))
    tok_spec = pl.BlockSpec((bt, t_len, GDN_WIDTH), lambda i: (i, 0, 0))
    return pl.pallas_call(
        _gdn_sample_rec_kernel,
        grid=(db // bt,),
        in_specs=[st_spec, tr_spec, tr_spec, tok_spec, tok_spec, tok_spec],
        out_specs=[tok_spec, st_spec],
        out_shape=[jax.ShapeDtypeStruct((db, t_len, GDN_WIDTH), F32),
                   jax.ShapeDtypeStruct(s0.shape, F32)],
        compiler_params=pltpu.CompilerParams(dimension_semantics=("parallel",),
                                             vmem_limit_bytes=VMEM_LIMIT),
        name="gdn_sample_rec",
    )(s0, qt, kt, v, beta, dec)


PAGES_PER_STEP = 16
PAGE_ROWS = PAGE_SIZE * MOBA_HEADS
PAGES_PER_BLOCK = MOBA_BLOCK // PAGE_SIZE
BLOCKS_PER_STEP = PAGES_PER_STEP // PAGES_PER_BLOCK
STEP_ROWS = PAGES_PER_STEP * PAGE_ROWS
HEAD_SHIFT = MOBA_HEADS.bit_length() - 1
assert 1 << HEAD_SHIFT == MOBA_HEADS and SUBLANES == 2 * MOBA_HEADS and PAGES_PER_BLOCK == 2


PAGE_RING = 3


def _moba_sample_kernel(pt_ref, q_ref, kn_ref, vn_ref, ck_ref, cv_ref, o_ref,
                        kwin_ref, vwin_ref, ksem, vsem,
                        kbuf_ref, vbuf_ref, kmean_ref, mem_ref, m_ref, l_ref, acc_ref):
    bp, g = pl.program_id(0), pl.program_id(1)
    n_batch = pl.num_programs(0) - 1
    n_steps = pl.num_programs(1)
    n_pages = n_steps * PAGES_PER_STEP
    lin = bp * n_steps + g

    def page_copies(step, ring_slot, stage):
        sb, sg = lax.div(step, n_steps), lax.rem(step, n_steps)
        if stage == 1:
            needed, row, cache, win, sem = sb < n_batch, sb, ck_ref, kwin_ref, ksem
        else:
            needed, row, cache, win, sem = sb > 0, sb - 1, cv_ref, vwin_ref, vsem
        base = jnp.where(needed, row, 0) * n_pages + sg * PAGES_PER_STEP
        return needed, [pltpu.make_async_copy(cache.at[pt_ref[base + pg]], win.at[ring_slot, pg],
                                              sem.at[ring_slot]) for pg in range(PAGES_PER_STEP)]

    def request(step, ring_slot):
        for stage in (1, 2):
            needed, copies = page_copies(step, ring_slot, stage)

            @pl.when(needed)
            def _():
                for c in copies:
                    c.start()

    ring_slot = lax.rem(lin, PAGE_RING)

    @pl.when(lin == 0)
    def _():
        for step in range(PAGE_RING - 1):
            request(jnp.int32(step), step)

    ahead = lin + (PAGE_RING - 1)

    @pl.when(ahead < (n_batch + 1) * n_steps)
    def _():
        request(ahead, lax.rem(ahead, PAGE_RING))

    for stage in (1, 2):
        needed, copies = page_copies(lin, ring_slot, stage)

        @pl.when(needed)
        def _():
            for c in copies:
                c.wait()

    rows = q_ref.shape[1]
    t_len = rows // MOBA_HEADS
    scale = HEAD_DIM ** -0.5
    t_shift = t_len.bit_length() - 1
    assert 1 << t_shift == t_len and kn_ref.shape[1] == rows
    step_lanes = BLOCKS_PER_STEP * MOBA_HEADS
    par = jnp.bitwise_and(bp, 1)
    row_i = lax.broadcasted_iota(jnp.int32, (rows, LANES), 0)
    lane_i = lax.broadcasted_iota(jnp.int32, (rows, LANES), 1)
    row_head = lax.shift_right_logical(row_i, t_shift)
    lane_head = jnp.bitwise_and(lane_i, MOBA_HEADS - 1)

    @pl.when(jnp.logical_and(bp == 0, g == 0))
    def _():
        kmean_ref[...] = jnp.zeros(kmean_ref.shape, F32)

    @pl.when(jnp.logical_and(bp > 0, g == 0))
    def _():
        q = q_ref[0]
        q16 = q.astype(BF16)
        n_cand = n_steps * step_lanes
        valid = jnp.logical_and(lane_head == row_head, lane_i < n_cand)
        member = _top3_members(_dot_nt_f32(q, kmean_ref[...]), valid)
        mem_ref[...] = jnp.where(member, 1.0, 0.0)
        pad = jnp.zeros((LANES - rows, HEAD_DIM), F32)
        kn = jnp.concatenate([kn_ref[0], pad], axis=0).astype(BF16)
        vn = jnp.concatenate([vn_ref[0], pad], axis=0).astype(BF16)
        s = _dot_nt(q16, kn) * scale
        ok = jnp.logical_and(lane_head == row_head, lane_i < rows)
        ok = jnp.logical_and(ok, lax.shift_right_logical(lane_i, HEAD_SHIFT)
                             <= jnp.bitwise_and(row_i, t_len - 1))
        s = jnp.where(ok, s, -jnp.inf)
        m = jnp.max(s, axis=1, keepdims=True)
        p = jnp.exp(s - m)
        m_ref[...] = jnp.broadcast_to(m, m_ref.shape)
        l_ref[...] = jnp.broadcast_to(jnp.sum(p, axis=1, keepdims=True), l_ref.shape)
        acc_ref[...] = _dot(p.astype(BF16), vn)

    @pl.when(bp > 0)
    def _():
        for pg in range(PAGES_PER_STEP):
            vbuf_ref[pg * PAGE_ROWS:(pg + 1) * PAGE_ROWS, :] = vwin_ref[ring_slot, pg].astype(BF16)
        member = mem_ref[...]
        blk_rows = PAGES_PER_BLOCK * PAGE_ROWS
        col_head = jnp.bitwise_and(lax.broadcasted_iota(jnp.int32, (rows, blk_rows), 1), MOBA_HEADS - 1)
        own_head = col_head == lax.shift_right_logical(
            lax.broadcasted_iota(jnp.int32, (rows, blk_rows), 0), t_shift)
        cand_blk = lax.shift_right_logical(lane_i, HEAD_SHIFT)
        bias = []
        for n in range(BLOCKS_PER_STEP):
            picked = jnp.sum(jnp.where(cand_blk == g * BLOCKS_PER_STEP + n, member, 0.0),
                             axis=1, keepdims=True) > 0.0
            bias.append(jnp.where(jnp.logical_and(own_head, picked), 0.0, -jnp.inf))
        s = _dot_nt(q_ref[0].astype(BF16), kbuf_ref[1 - par, g]) * scale + jnp.concatenate(bias, axis=1)
        m = m_ref[:, 0:1]
        m_new = jnp.maximum(m, jnp.max(s, axis=1, keepdims=True))
        alpha = jnp.exp(m - m_new)
        p = jnp.exp(s - m_new)
        l = l_ref[:, 0:1] * alpha + jnp.sum(p, axis=1, keepdims=True)
        acc = acc_ref[...] * alpha + _dot(p.astype(BF16), vbuf_ref[...])
        m_ref[...] = jnp.broadcast_to(m_new, m_ref.shape)
        l_ref[...] = jnp.broadcast_to(l, l_ref.shape)
        acc_ref[...] = acc

        @pl.when(g == n_steps - 1)
        def _():
            o_ref[0] = acc / l

    @pl.when(bp < n_batch)
    def _():
        low = lax.broadcasted_iota(jnp.int32, (SUBLANES, HEAD_DIM), 0) < MOBA_HEADS

        def block_mean(n):
            tot = None
            for r in range(PAGES_PER_BLOCK):
                pg = n * PAGES_PER_BLOCK + r
                page = kwin_ref[ring_slot, pg]
                kbuf_ref[par, g, pg * PAGE_ROWS:(pg + 1) * PAGE_ROWS, :] = page.astype(BF16)
                part = jnp.sum(page.reshape(PAGE_ROWS // SUBLANES, SUBLANES, HEAD_DIM), axis=0)
                tot = part if tot is None else tot + part
            return (tot + pltpu.roll(tot, MOBA_HEADS, 0)) * (1.0 / MOBA_BLOCK)

        tiles = [jnp.where(low, block_mean(2 * n), block_mean(2 * n + 1))
                 for n in range(BLOCKS_PER_STEP // 2)]
        kmean_ref[pl.ds(pl.multiple_of(g * step_lanes, step_lanes), step_lanes), :] = (
            jnp.concatenate(tiles, axis=0))


def _moba_sample(page_flat, q2, kn2, vn2, cache_k, cache_v, n_pages):
    db, rows, _ = q2.shape
    assert n_pages % PAGES_PER_STEP == 0
    n_steps = n_pages // PAGES_PER_STEP
    n_cand = n_steps * BLOCKS_PER_STEP * MOBA_HEADS
    assert rows <= LANES and n_cand <= LANES
    tok = pl.BlockSpec((1, rows, HEAD_DIM), lambda bp, g, pt: (jnp.maximum(bp - 1, 0), 0, 0))
    window = pltpu.VMEM((PAGE_RING, PAGES_PER_STEP, PAGE_ROWS, HEAD_DIM), F32)
    return pl.pallas_call(
        _moba_sample_kernel,
        grid_spec=pltpu.PrefetchScalarGridSpec(
            num_scalar_prefetch=1,
            grid=(db + 1, n_steps),
            in_specs=[tok, tok, tok, pl.BlockSpec(memory_space=pl.ANY),
                      pl.BlockSpec(memory_space=pl.ANY)],
            out_specs=tok,
            scratch_shapes=[window, window,
                            pltpu.SemaphoreType.DMA((PAGE_RING,)), pltpu.SemaphoreType.DMA((PAGE_RING,)),
                            pltpu.VMEM((2, n_steps, STEP_ROWS, HEAD_DIM), BF16),
                            pltpu.VMEM((STEP_ROWS, HEAD_DIM), BF16),
                            pltpu.VMEM((LANES, HEAD_DIM), F32),
                            pltpu.VMEM((rows, LANES), F32),
                            pltpu.VMEM((rows, LANES), F32), pltpu.VMEM((rows, LANES), F32),
                            pltpu.VMEM((rows, HEAD_DIM), F32)],
        ),
        out_shape=jax.ShapeDtypeStruct((db, rows, HEAD_DIM), F32),
        compiler_params=pltpu.CompilerParams(dimension_semantics=("arbitrary", "arbitrary"),
                                             vmem_limit_bytes=VMEM_LIMIT),
        name="moba_sample",
    )(page_flat, q2, kn2, vn2, cache_k, cache_v)


def _merge_kernel(og_ref, za_ref, om_ref, zb_ref, x_ref, gw_ref, wo_ref, nf_ref, y_ref):
    og = og_ref[...].astype(F32)
    parts = []
    for h in range(GDN_HEADS):
        oh = og[:, h * HEAD_DIM:(h + 1) * HEAD_DIM]
        parts.append(oh * lax.rsqrt(jnp.mean(oh * oh, axis=-1, keepdims=True) + NORM_EPS) * gw_ref[...])
    ga = jnp.concatenate(parts, axis=1) * _silu(za_ref[...].astype(F32))
    gb = om_ref[...].astype(F32) * _silu(zb_ref[...].astype(F32))
    cat = jnp.concatenate([ga, gb], axis=1).astype(BF16)
    hp = x_ref[...] + _dot(cat, wo_ref[...])
    y_ref[...] = hp * lax.rsqrt(jnp.mean(hp * hp, axis=-1, keepdims=True) + NORM_EPS) * nf_ref[...]


def _merge(og, za, om, zb, x2d, gdn_norm_w, w_out16, norm_f_w):
    m, d = x2d.shape
    tm = MERGE_TILE if m % MERGE_TILE == 0 else ROW_TILE
    assert m % tm == 0
    row = lambda i: (i, 0)
    full = lambda i: (0, 0)
    half = pl.BlockSpec((tm, GDN_WIDTH), row)
    return pl.pallas_call(
        _merge_kernel,
        grid=(m // tm,),
        in_specs=[half, half, half, half, pl.BlockSpec((tm, d), row),
                  pl.BlockSpec((1, HEAD_DIM), full), pl.BlockSpec(w_out16.shape, full),
                  pl.BlockSpec((1, d), full)],
        out_specs=pl.BlockSpec((tm, d), row),
        out_shape=jax.ShapeDtypeStruct((m, d), F32),
        compiler_params=pltpu.CompilerParams(dimension_semantics=("parallel",),
                                             vmem_limit_bytes=VMEM_LIMIT),
        name="merge_out",
    )(og, za, om, zb, x2d, gdn_norm_w, w_out16, norm_f_w)


def _rope_tables(pos):
    half = HEAD_DIM // 2
    inv_freq = ROPE_THETA ** (-jnp.arange(half, dtype=F32) / half)
    ang = pos.astype(F32)[:, None] * inv_freq[None, :]
    cos, sin = jnp.cos(ang), jnp.sin(ang)
    return jnp.concatenate([cos, cos], axis=1), jnp.concatenate([-sin, sin], axis=1)


def _relayout_w_in(w):
    o = 0
    parts = {}
    for name, n in (("qkv", GDN_CONV_CH), ("za", GDN_WIDTH), ("b", GDN_HEADS), ("a", GDN_HEADS),
                    ("qb", MOBA_WIDTH), ("kb", MOBA_WIDTH), ("vb", MOBA_WIDTH), ("zb", MOBA_WIDTH)):
        parts[name] = w[:, o:o + n]
        o += n
    pad = jnp.zeros((w.shape[0], LANES - 2 * GDN_HEADS), w.dtype)
    return jnp.concatenate([parts["qkv"], parts["za"], parts["qb"], parts["kb"], parts["vb"],
                            parts["zb"], parts["b"], parts["a"], pad], axis=1).astype(BF16)


def kernel(x_prompt, x_sample, cache_k, cache_v, state_gdn, state_conv, page_table, norm_in_w, w_in,
           conv_w, a_log, dt_bias, gdn_norm_w, w_out, norm_f_w):
    b, s, d = x_prompt.shape
    db, t_len, _ = x_sample.shape
    depth = w_in.shape[0]
    assert depth == 1
    n_pages = page_table.shape[1]
    past_len = n_pages * PAGE_SIZE
    assert past_len % MOBA_BLOCK == 0
    assert t_len <= PAGE_SIZE and t_len >= GDN_CONV_W - 1 and (db * t_len) % ROW_TILE == 0
    assert ROW_TILE % t_len == 0

    w_p = _relayout_w_in(w_in[0])
    w_o = w_out[0].astype(BF16)
    nw = norm_in_w[0][None, :]
    cw = conv_w[0]
    zeros4 = jnp.zeros((GDN_HEADS,), F32)
    lane_pad = jnp.zeros((LANES - 2 * GDN_HEADS,), F32)
    arow = jnp.stack([jnp.concatenate([zeros4, a_log[0], lane_pad]),
                      jnp.concatenate([zeros4, dt_bias[0], lane_pad])])
    acol = jnp.stack([jnp.concatenate([zeros4, a_log[0]]),
                      jnp.concatenate([zeros4, dt_bias[0]])])[:, :, None]
    gw = gdn_norm_w[0][None, :]
    nf = norm_f_w[None, :]

    cos_p, sin_p = _rope_tables(jnp.arange(s, dtype=jnp.int32))
    (qkv_p, za_p, q_p, k_p, v_p, zb_p, ba_p, kbf_p, vbf_p, kmean_p) = _inproj(
        x_prompt.reshape(b * s, d), nw, w_p, cos_p, sin_p, prompt=True)
    qkv_p3 = qkv_p.reshape(b, s, GDN_CONV_CH)
    ba_p3 = ba_p.reshape(b, s, LANES)
    bat_p = jnp.transpose(ba_p3[:, :, :SUBLANES], (0, 2, 1))
    n_m, intra, rhs, qdec, kdec, glast = _gdn_prep(qkv_p3, ba_p3, bat_p, cw, arow, acol)
    bh, nc, c, _ = n_m.shape
    t_m = _tri_inv(jnp.transpose(n_m.reshape(bh * nc, c, c), (1, 2, 0)))
    t_m = jnp.transpose(t_m, (2, 0, 1)).reshape(bh, nc, c, c)
    o_gdn_p, s_gdn_p = _gdn_scan(t_m, rhs, qdec, kdec, intra, glast, b)
    o_moba_p = _moba_prompt(q_p.reshape(b, s, MOBA_WIDTH), kbf_p.reshape(b, s, MOBA_WIDTH),
                            vbf_p.reshape(b, s, MOBA_WIDTH),
                            kmean_p.reshape(b, s // MOBA_BLOCK, MOBA_WIDTH))
    y_p = _merge(o_gdn_p.reshape(b * s, GDN_WIDTH), za_p, o_moba_p.reshape(b * s, MOBA_WIDTH), zb_p,
                 x_prompt.reshape(b * s, d), gw, w_o, nf)

    cos_s, sin_s = _rope_tables(past_len + jnp.arange(t_len, dtype=jnp.int32))
    reps = ROW_TILE // t_len
    (qkv_s, za_s, q_s, k_s, v_s, zb_s, ba_s) = _inproj(
        x_sample.reshape(db * t_len, d), nw, w_p, jnp.tile(cos_s, (reps, 1)), jnp.tile(sin_s, (reps, 1)),
        prompt=False)
    qkv_s3 = qkv_s.reshape(db, t_len, GDN_CONV_CH)
    tb = lambda a: jnp.transpose(a, (1, 0, 2))
    qg, kg, vg, beta_t, dec_t = _gdn_sample_prep(
        tb(qkv_s3), tb(state_conv[0]), tb(ba_s.reshape(db, t_len, LANES)), cw, arow)
    to_cols = lambda a: jnp.transpose(a.reshape(t_len, db, GDN_HEADS, HEAD_DIM), (1, 2, 3, 0))
    o_gdn_s, s_gdn_s = _gdn_sample_rec(state_gdn[0], to_cols(qg), to_cols(kg), tb(vg), tb(beta_t),
                                       tb(dec_t))

    page_flat = page_table.reshape(-1).astype(jnp.int32)
    ck = cache_k.reshape(cache_k.shape[1], PAGE_ROWS, HEAD_DIM)
    cv = cache_v.reshape(cache_v.shape[1], PAGE_ROWS, HEAD_DIM)
    rows_s = t_len * MOBA_HEADS
    q2 = jnp.transpose(q_s.reshape(db, t_len, MOBA_HEADS, HEAD_DIM), (0, 2, 1, 3))
    o2 = _moba_sample(page_flat, q2.reshape(db, rows_s, HEAD_DIM), k_s.reshape(db, rows_s, HEAD_DIM),
                      v_s.reshape(db, rows_s, HEAD_DIM), ck, cv, n_pages)
    o_moba_s = jnp.transpose(o2.reshape(db, MOBA_HEADS, t_len, HEAD_DIM), (0, 2, 1, 3))
    y_s = _merge(o_gdn_s.reshape(db * t_len, GDN_WIDTH), za_s, o_moba_s.reshape(db * t_len, MOBA_WIDTH),
                 zb_s, x_sample.reshape(db * t_len, d), gw, w_o, nf)

    ctx = GDN_CONV_W - 1
    heads = lambda a, n, t: a.reshape(1, n, t, MOBA_HEADS, HEAD_DIM)
    return (y_p.reshape(b, s, d), y_s.reshape(db, t_len, d),
            heads(k_p, b, s), heads(v_p, b, s),
            s_gdn_p.reshape(1, b, GDN_HEADS, HEAD_DIM, HEAD_DIM),
            qkv_p3[:, s - ctx:, :][None],
            heads(k_s, db, t_len), heads(v_s, db, t_len),
            s_gdn_s[None],
            qkv_s3[:, t_len - ctx:, :][None])
```

```python
import functools
import math

import jax
import jax.numpy as jnp
from jax import lax
from jax.experimental import pallas as pl
from jax.experimental.pallas import tpu as pltpu

F32 = jnp.float32
BF16 = jnp.bfloat16

HEAD_DIM = 128
GDN_HEADS = 4
MOBA_HEADS = 4
GDN_QK = GDN_HEADS * HEAD_DIM
GDN_WIDTH = GDN_HEADS * HEAD_DIM
GDN_CONV_W = 4
GDN_CONV_CH = 2 * GDN_QK + GDN_WIDTH
GDN_CHUNK = 64
MOBA_WIDTH = MOBA_HEADS * HEAD_DIM
MOBA_BLOCK = 256
MOBA_TOPK = 3
PAGE_SIZE = 128
ROPE_THETA = 10000.0
NORM_EPS = 1e-6
LANES = 128
SUBLANES = 8
MASK_BIAS = -1e30

C_QKV = 0
C_ZA = GDN_CONV_CH
C_QB = C_ZA + GDN_WIDTH
C_KB = C_QB + MOBA_WIDTH
C_VB = C_KB + MOBA_WIDTH
C_ZB = C_VB + MOBA_WIDTH
C_BA = C_ZB + MOBA_WIDTH
IN_COLS_PAD = C_BA + LANES

ROW_TILE = 256
MERGE_TILE = 1024
VMEM_LIMIT = 56 * 1024 * 1024


def _dot(a, b):
    return jnp.dot(a, b, preferred_element_type=F32)


def _dot_nt(a, b):
    return lax.dot_general(a, b, (((1,), (1,)), ((), ())), preferred_element_type=F32)


def _dot_tn(a, b):
    return lax.dot_general(a, b, (((0,), (0,)), ((), ())), preferred_element_type=F32)


def _dot_nt_f32(a, b):
    return lax.dot_general(a, b, (((1,), (1,)), ((), ())), preferred_element_type=F32,
                           precision=lax.Precision.HIGHEST)


def _dot_f32(a, b):
    return jnp.dot(a, b, preferred_element_type=F32, precision=lax.Precision.HIGHEST)


def _split_bf16(a):
    hi = a.astype(BF16)
    lo = (a - hi.astype(F32)).astype(BF16)
    return hi, lo


def _dot3(a, b):
    ah, al = _split_bf16(a)
    bh, bl = _split_bf16(b)
    return _dot(ah, bh) + (_dot(ah, bl) + _dot(al, bh))


def _dot3_nt(a, b):
    ah, al = _split_bf16(a)
    bh, bl = _split_bf16(b)
    return _dot_nt(ah, bh) + (_dot_nt(ah, bl) + _dot_nt(al, bh))


def _silu(x):
    return x * jax.nn.sigmoid(x)


def _softplus(x):
    return jnp.maximum(x, 0.0) + jnp.log1p(jnp.exp(-jnp.abs(x)))


def _top3_members(gate, valid, axis=1):
    cand = lax.broadcasted_iota(jnp.int32, gate.shape, axis).astype(F32)
    g = jnp.where(valid, gate, -jnp.inf)
    member = jnp.zeros(gate.shape, dtype=jnp.bool_)
    for _ in range(MOBA_TOPK):
        m = jnp.max(g, axis=axis, keepdims=True)
        idx = jnp.min(jnp.where(g == m, cand, float(LANES)), axis=axis, keepdims=True)
        pick = cand == idx
        member = jnp.logical_or(member, pick)
        g = jnp.where(pick, -jnp.inf, g)
    return jnp.logical_and(member, valid)


def _inproj_kernel(x_ref, nw_ref, w_ref, cos_ref, sin_ref,
                   qkv_ref, za_ref, q_ref, k_ref, v_ref, zb_ref, ba_ref, *prompt_refs):
    x = x_ref[...]
    ms = jnp.mean(x * x, axis=-1, keepdims=True)
    hb = (x * lax.rsqrt(ms + NORM_EPS) * nw_ref[...]).astype(BF16)

    def proj(c0, n):
        return _dot(hb, w_ref[:, c0:c0 + n])

    def rope(t):
        parts = []
        for h in range(MOBA_HEADS):
            th = t[:, h * HEAD_DIM:(h + 1) * HEAD_DIM]
            parts.append(th * cos_ref[...] + pltpu.roll(th, HEAD_DIM // 2, 1) * sin_ref[...])
        return jnp.concatenate(parts, axis=1)

    for j in range(GDN_CONV_CH // GDN_QK):
        qkv_ref[:, j * GDN_QK:(j + 1) * GDN_QK] = proj(C_QKV + j * GDN_QK, GDN_QK)
    za_ref[...] = proj(C_ZA, GDN_WIDTH).astype(za_ref.dtype)
    q_ref[...] = rope(proj(C_QB, MOBA_WIDTH))
    k = rope(proj(C_KB, MOBA_WIDTH))
    k_ref[...] = k
    v = proj(C_VB, MOBA_WIDTH)
    v_ref[...] = v
    zb_ref[...] = proj(C_ZB, MOBA_WIDTH).astype(zb_ref.dtype)
    ba_ref[...] = proj(C_BA, LANES)
    if prompt_refs:
        kbf_ref, vbf_ref, kmean_ref = prompt_refs
        kbf_ref[...] = k.astype(BF16)
        vbf_ref[...] = v.astype(BF16)
        kmean_ref[0] = jnp.mean(k, axis=0, keepdims=True)


def _inproj(x2d, norm_w, w_p, cosf, sinf, *, prompt):
    m, d = x2d.shape
    tm = ROW_TILE
    assert m % tm == 0 and cosf.shape[0] % tm == 0
    n_pos = cosf.shape[0] // tm
    row = lambda i: (i, 0)
    full = lambda i: (0, 0)
    pos = (lambda i: (i % n_pos, 0)) if n_pos > 1 else full
    out_shape = [jax.ShapeDtypeStruct((m, GDN_CONV_CH), F32)]
    out_specs = [pl.BlockSpec((tm, GDN_CONV_CH), row)]
    for dt in (BF16, F32, F32, F32, BF16):
        out_shape.append(jax.ShapeDtypeStruct((m, MOBA_WIDTH), dt))
        out_specs.append(pl.BlockSpec((tm, MOBA_WIDTH), row))
    out_shape.append(jax.ShapeDtypeStruct((m, LANES), F32))
    out_specs.append(pl.BlockSpec((tm, LANES), row))
    if prompt:
        assert tm == MOBA_BLOCK
        for _ in range(2):
            out_shape.append(jax.ShapeDtypeStruct((m, MOBA_WIDTH), BF16))
            out_specs.append(pl.BlockSpec((tm, MOBA_WIDTH), row))
        out_shape.append(jax.ShapeDtypeStruct((m // tm, 1, MOBA_WIDTH), F32))
        out_specs.append(pl.BlockSpec((1, 1, MOBA_WIDTH), lambda i: (i, 0, 0)))
    return pl.pallas_call(
        _inproj_kernel,
        grid=(m // tm,),
        in_specs=[pl.BlockSpec((tm, d), row), pl.BlockSpec((1, d), full),
                  pl.BlockSpec((d, IN_COLS_PAD), full),
                  pl.BlockSpec((tm, HEAD_DIM), pos), pl.BlockSpec((tm, HEAD_DIM), pos)],
        out_specs=out_specs,
        out_shape=out_shape,
        compiler_params=pltpu.CompilerParams(dimension_semantics=("parallel",),
                                             vmem_limit_bytes=VMEM_LIMIT),
        name="inproj_prompt" if prompt else "inproj_sample",
    )(x2d, norm_w, w_p, cosf, sinf)


def _conv_silu_norm(c):
    c = _silu(c)
    qs, ks = [], []
    for h in range(GDN_HEADS):
        qh = c[:, h * HEAD_DIM:(h + 1) * HEAD_DIM]
        kh = c[:, GDN_QK + h * HEAD_DIM:GDN_QK + (h + 1) * HEAD_DIM]
        qs.append(qh * (lax.rsqrt(jnp.sum(qh * qh, axis=-1, keepdims=True) + NORM_EPS)
                        * (HEAD_DIM ** -0.5)))
        ks.append(kh * lax.rsqrt(jnp.sum(kh * kh, axis=-1, keepdims=True) + NORM_EPS))
    return qs, ks, c[:, 2 * GDN_QK:]


def _gdn_prep_kernel(x_ref, halo_ref, ba_ref, bat_ref, cw_ref, arow_ref, acol_ref,
                     n_ref, intra_ref, rhs_ref, qdec_ref, kdec_ref, glast_ref, buf_ref):
    i = pl.program_id(1)
    tt = x_ref.shape[1]
    c = GDN_CHUNK
    halo = jnp.where(i > 0, halo_ref[0], 0.0)
    buf_ref[0:SUBLANES, :] = halo
    buf_ref[SUBLANES:SUBLANES + tt, :] = x_ref[0]
    full = buf_ref[...]
    conv = full[SUBLANES:, :] * cw_ref[GDN_CONV_W - 1:GDN_CONV_W, :]
    for back in range(1, GDN_CONV_W):
        w = GDN_CONV_W - 1 - back
        conv = conv + pltpu.roll(full, back, 0)[SUBLANES:, :] * cw_ref[w:w + 1, :]
    qs, ks, v = _conv_silu_norm(conv)

    ba = ba_ref[0]
    beta_full = jax.nn.sigmoid(ba)
    g_full = -jnp.exp(arow_ref[0:1, :]) * _softplus(ba + arow_ref[1:2, :])
    bat = bat_ref[0]
    gt_full = -jnp.exp(acol_ref[0]) * _softplus(bat + acol_ref[1])

    ri = lax.broadcasted_iota(jnp.int32, (c, c), 0)
    ci = lax.broadcasted_iota(jnp.int32, (c, c), 1)
    tril = ri >= ci
    strict = ri > ci
    ltri = jnp.where(tril, 1.0, 0.0).astype(F32)
    utri = jnp.where(ri <= ci, 1.0, 0.0).astype(F32)

    for cc in range(tt // c):
        rows = slice(cc * c, (cc + 1) * c)
        gcol_all = _dot_f32(ltri, g_full[rows, :])
        grow_all = _dot_f32(gt_full[:, rows], utri)
        for h in range(GDN_HEADS):
            gc_c = gcol_all[:, GDN_HEADS + h:GDN_HEADS + h + 1]
            gc_r = grow_all[GDN_HEADS + h:GDN_HEADS + h + 1, :]
            gamma = jnp.where(tril, jnp.exp(jnp.where(tril, gc_c - gc_r, 0.0)), 0.0)
            beta = beta_full[rows, h:h + 1]
            qh, kh = qs[h][rows, :], ks[h][rows, :]
            vh = v[rows, h * HEAD_DIM:(h + 1) * HEAD_DIM]
            kb = kh * beta
            k16 = kh.astype(BF16)
            n_ref[h, cc] = jnp.where(strict, _dot_nt(kb.astype(BF16), k16) * gamma, 0.0)
            intra_ref[h, cc] = jnp.where(tril, _dot_nt(qh.astype(BF16), k16) * gamma, 0.0)
            egc = jnp.exp(gc_c)
            rhs_ref[h, cc, :, 0:HEAD_DIM] = vh * beta
            rhs_ref[h, cc, :, HEAD_DIM:2 * HEAD_DIM] = kb * egc
            qdec_ref[h, cc] = qh * egc
            gl = gc_c[c - 1:c, :]
            kdec_ref[h, cc] = kh * jnp.exp(gl - gc_c)
            glast_ref[h, cc] = jnp.broadcast_to(jnp.exp(gl), (SUBLANES, LANES))


def _gdn_prep(qkv, ba, bat, conv_w, arow, acol):
    b, s, _ = qkv.shape
    tt = ROW_TILE
    c = GDN_CHUNK
    cpt = tt // c
    nc = s // c
    hb = tt // SUBLANES
    bh = b * GDN_HEADS

    def o(x):
        return (jax.ShapeDtypeStruct((bh, nc, c, x), F32),
                pl.BlockSpec((GDN_HEADS, cpt, c, x), lambda bi, i: (bi, i, 0, 0)))

    outs = [o(c), o(c), o(2 * HEAD_DIM), o(HEAD_DIM), o(HEAD_DIM)]
    outs.append((jax.ShapeDtypeStruct((bh, nc, SUBLANES, LANES), F32),
                 pl.BlockSpec((GDN_HEADS, cpt, SUBLANES, LANES), lambda bi, i: (bi, i, 0, 0))))
    return pl.pallas_call(
        _gdn_prep_kernel,
        grid=(b, s // tt),
        in_specs=[
            pl.BlockSpec((1, tt, GDN_CONV_CH), lambda bi, i: (bi, i, 0)),
            pl.BlockSpec((1, SUBLANES, GDN_CONV_CH), lambda bi, i: (bi, jnp.maximum(i * hb - 1, 0), 0)),
            pl.BlockSpec((1, tt, LANES), lambda bi, i: (bi, i, 0)),
            pl.BlockSpec((1, SUBLANES, tt), lambda bi, i: (bi, 0, i)),
            pl.BlockSpec((GDN_CONV_W, GDN_CONV_CH), lambda bi, i: (0, 0)),
            pl.BlockSpec((2, LANES), lambda bi, i: (0, 0)),
            pl.BlockSpec((2, SUBLANES, 1), lambda bi, i: (0, 0, 0)),
        ],
        out_specs=[x[1] for x in outs],
        out_shape=[x[0] for x in outs],
        scratch_shapes=[pltpu.VMEM((SUBLANES + tt, GDN_CONV_CH), F32)],
        compiler_params=pltpu.CompilerParams(dimension_semantics=("parallel", "parallel"),
                                             vmem_limit_bytes=VMEM_LIMIT),
        name="gdn_prep",
    )(qkv, qkv, ba, bat, conv_w, arow, acol)


def _tri_inv_kernel(n_ref, t_ref):
    c = n_ref.shape[0]
    t_ref[...] = jnp.zeros(t_ref.shape, F32)

    for ig in range(c // SUBLANES):
        width = (ig + 1) * SUBLANES
        col = lax.broadcasted_iota(jnp.int32, (width, LANES), 0)

        def outer(i, carry, width=width, col=col):
            def inner(jg, acc):
                j0 = pl.multiple_of(jg * SUBLANES, SUBLANES)
                n_rows = n_ref[i, pl.ds(j0, SUBLANES), :]
                for r in range(SUBLANES):
                    acc = acc - n_rows[r:r + 1, :] * t_ref[j0 + r, 0:width, :]
                return acc
            groups = lax.shift_right_logical(i + (SUBLANES - 1), SUBLANES.bit_length() - 1)
            acc = lax.fori_loop(0, groups, inner, jnp.where(col == i, 1.0, 0.0).astype(F32))
            t_ref[i, 0:width, :] = acc
            return carry

        lax.fori_loop(ig * SUBLANES, (ig + 1) * SUBLANES, outer, 0)


def _tri_inv(nt):
    c, _, nmat = nt.shape
    assert nmat % LANES == 0
    spec = pl.BlockSpec((c, c, LANES), lambda g: (0, 0, g))
    return pl.pallas_call(
        _tri_inv_kernel,
        grid=(nmat // LANES,),
        in_specs=[spec],
        out_specs=spec,
        out_shape=jax.ShapeDtypeStruct(nt.shape, F32),
        compiler_params=pltpu.CompilerParams(dimension_semantics=("parallel",)),
        name="gdn_tri_inv",
    )(nt)


def _gdn_scan_kernel(t_ref, rhs_ref, qdec_ref, kdec_ref, intra_ref, glast_ref, o_ref, s_ref):
    i = pl.program_id(0)
    bh, ct, c, _ = t_ref.shape

    @pl.when(i == 0)
    def _():
        s_ref[...] = jnp.zeros(s_ref.shape, F32)

    heads = range(bh)
    for cc in range(ct):
        sol = [_dot3(t_ref[n, cc], rhs_ref[n, cc]) for n in heads]
        wqs = [_dot(jnp.concatenate([sol[n][:, HEAD_DIM:], qdec_ref[n, cc]], axis=0).astype(BF16),
                    s_ref[n].astype(BF16)) for n in heads]
        vn16 = [(sol[n][:, :HEAD_DIM] - wqs[n][:c]).astype(BF16) for n in heads]
        for n in heads:
            s_ref[n] = (s_ref[n] * glast_ref[n, cc, 0:1, :]
                        + _dot_tn(kdec_ref[n, cc].astype(BF16), vn16[n]))
        for n in heads:
            b, h = n // GDN_HEADS, n % GDN_HEADS
            o_ref[b, cc * c:(cc + 1) * c, h * HEAD_DIM:(h + 1) * HEAD_DIM] = (
                wqs[n][c:] + _dot(intra_ref[n, cc].astype(BF16), vn16[n])).astype(o_ref.dtype)


def _gdn_scan(t, rhs, qdec, kdec, intra, glast, b):
    bh, nc, c, _ = t.shape
    ct = 8
    assert nc % ct == 0

    def spec(x):
        return pl.BlockSpec((bh, ct, x.shape[2], x.shape[3]), lambda i: (0, i, 0, 0))

    return pl.pallas_call(
        _gdn_scan_kernel,
        grid=(nc // ct,),
        in_specs=[spec(t), spec(rhs), spec(qdec), spec(kdec), spec(intra), spec(glast)],
        out_specs=[pl.BlockSpec((b, ct * c, GDN_WIDTH), lambda i: (0, i, 0)),
                   pl.BlockSpec((bh, HEAD_DIM, HEAD_DIM), lambda i: (0, 0, 0))],
        out_shape=[jax.ShapeDtypeStruct((b, nc * c, GDN_WIDTH), BF16),
                   jax.ShapeDtypeStruct((bh, HEAD_DIM, HEAD_DIM), F32)],
        compiler_params=pltpu.CompilerParams(dimension_semantics=("arbitrary",),
                                             vmem_limit_bytes=VMEM_LIMIT),
        name="gdn_scan",
    )(t, rhs, qdec, kdec, intra, glast)


MOBA_GROUP = 8
MOBA_SPLIT = 2
MOBA_QTILE = MOBA_BLOCK


def _moba_prompt_kernel(q_ref, k_ref, v_ref, kmean_ref, o_ref):
    i = pl.program_id(2)
    tq = q_ref.shape[1]
    blk = MOBA_BLOCK
    scale = HEAD_DIM ** -0.5
    q = q_ref[0]
    q16 = q.astype(BF16)
    kmean = jnp.concatenate(
        [kmean_ref[0], jnp.zeros((LANES - kmean_ref.shape[1], HEAD_DIM), F32)], axis=0)
    bpt = tq // blk
    blk_shift = blk.bit_length() - 1
    own_blk = i * bpt + lax.shift_right_logical(
        lax.broadcasted_iota(jnp.int32, (LANES, tq), 1), blk_shift)
    past = lax.broadcasted_iota(jnp.int32, (LANES, tq), 0) < own_blk
    member = _top3_members(_dot3_nt(kmean, q), past, axis=0)
    bias = jnp.transpose(jnp.where(member, 0.0, MASK_BIAS))
    q_aug = jnp.concatenate([q16, bias.astype(BF16)], axis=1)
    grp = MOBA_GROUP * blk
    c_exp = scale * math.log2(math.e)
    tile_rows = 2 * SUBLANES
    tile_lane = lax.broadcasted_iota(jnp.int32, (tile_rows, LANES), 1)

    rq = lax.broadcasted_iota(jnp.int32, (blk, blk), 0)
    ck = lax.broadcasted_iota(jnp.int32, (blk, blk), 1)
    ms, ls, accs = [], [], []
    for r in range(bpt):
        start = pl.multiple_of((i * bpt + r) * blk, blk)
        s = _dot_nt(q16[r * blk:(r + 1) * blk, :], k_ref[0, pl.ds(start, blk), :])
        s = jnp.where(ck <= rq, s, -jnp.inf)
        m_r = jnp.max(s, axis=1, keepdims=True)
        p = jnp.exp2((s - m_r) * c_exp)
        ms.append(m_r)
        ls.append(jnp.sum(p, axis=1, keepdims=True))
        accs.append(_dot(p.astype(BF16), v_ref[0, pl.ds(start, blk), :]))
    m, l, acc = (jnp.concatenate(x, axis=0) for x in (ms, ls, accs))

    sub = grp // MOBA_SPLIT

    def body(j, carry):
        m, l, acc = carry
        st = pl.multiple_of(j * grp, grp)
        onehot = []
        for n in range(MOBA_GROUP):
            tile = jnp.where(tile_lane == j * MOBA_GROUP + n, 1.0, 0.0).astype(BF16)
            onehot += [tile] * (blk // tile_rows)
        k_aug = jnp.concatenate([k_ref[0, pl.ds(st, grp), :], jnp.concatenate(onehot, axis=0)], axis=1)
        raw = [_dot_nt(q_aug, k_aug[u * sub:(u + 1) * sub, :]) for u in range(MOBA_SPLIT)]
        for u in range(MOBA_SPLIT):
            m_new = jnp.maximum(m, jnp.max(raw[u], axis=1, keepdims=True))
            alpha = jnp.exp2((m - m_new) * c_exp)
            p = jnp.exp2((raw[u] - m_new) * c_exp)
            l = l * alpha + jnp.sum(p, axis=1, keepdims=True)
            acc = acc * alpha + _dot(p.astype(BF16), v_ref[0, pl.ds(st + u * sub, sub), :])
            m = m_new
        return m, l, acc

    n_groups = lax.shift_right_logical(i * bpt + (bpt - 1) + (MOBA_GROUP - 1),
                                       MOBA_GROUP.bit_length() - 1)
    m, l, acc = lax.fori_loop(0, n_groups, body, (m, l, acc))
    o_ref[0] = (acc / l).astype(o_ref.dtype)


def _moba_prompt(q, kbf, vbf, kmean):
    b, s, _ = q.shape
    tq = MOBA_QTILE
    nb = s // MOBA_BLOCK
    assert s % tq == 0 and tq % MOBA_BLOCK == 0 and nb <= LANES and nb % MOBA_GROUP == 0
    return pl.pallas_call(
        _moba_prompt_kernel,
        grid=(b, MOBA_HEADS, s // tq),
        in_specs=[pl.BlockSpec((1, tq, HEAD_DIM), lambda bi, h, i: (bi, i, h)),
                  pl.BlockSpec((1, s, HEAD_DIM), lambda bi, h, i: (bi, 0, h)),
                  pl.BlockSpec((1, s, HEAD_DIM), lambda bi, h, i: (bi, 0, h)),
                  pl.BlockSpec((1, nb, HEAD_DIM), lambda bi, h, i: (bi, 0, h))],
        out_specs=pl.BlockSpec((1, tq, HEAD_DIM), lambda bi, h, i: (bi, i, h)),
        out_shape=jax.ShapeDtypeStruct((b, s, MOBA_WIDTH), BF16),
        compiler_params=pltpu.CompilerParams(
            dimension_semantics=("parallel", "parallel", "arbitrary"),
            vmem_limit_bytes=VMEM_LIMIT),
        name="moba_prompt",
    )(q, kbf, vbf, kmean)


def _gdn_sample_prep_kernel(x_ref, st_ref, ba_ref, cw_ref, arow_ref,
                            q_ref, k_ref, v_ref, beta_ref, dec_ref):
    t_len = x_ref.shape[0]
    ctx = GDN_CONV_W - 1

    def src(tt):
        return st_ref[tt + ctx] if tt < 0 else x_ref[tt]

    for t in range(t_len):
        conv = src(t - ctx) * cw_ref[0:1, :]
        for w in range(1, GDN_CONV_W):
            conv = conv + src(t - ctx + w) * cw_ref[w:w + 1, :]
        qs, ks, v = _conv_silu_norm(conv)
        q_ref[t] = jnp.concatenate(qs, axis=1)
        k_ref[t] = jnp.concatenate(ks, axis=1)
        v_ref[t] = v
        ba = ba_ref[t]
        beta_full = jax.nn.sigmoid(ba)
        dec_full = jnp.exp(-jnp.exp(arow_ref[0:1, :]) * _softplus(ba + arow_ref[1:2, :]))
        rows = ba.shape[0]
        beta_ref[t] = jnp.concatenate(
            [jnp.broadcast_to(beta_full[:, h:h + 1], (rows, HEAD_DIM)) for h in range(GDN_HEADS)], axis=1)
        dec_ref[t] = jnp.concatenate(
            [jnp.broadcast_to(dec_full[:, GDN_HEADS + h:GDN_HEADS + h + 1], (rows, HEAD_DIM))
             for h in range(GDN_HEADS)], axis=1)


def _gdn_sample_prep(x_t, st_t, ba_t, conv_w, arow):
    t_len, db, _ = x_t.shape
    bt = min(db, 32)
    assert db % bt == 0

    def spec(t, w):
        return pl.BlockSpec((t, bt, w), lambda i: (0, i, 0))

    out = jax.ShapeDtypeStruct((t_len, db, GDN_WIDTH), F32)
    return pl.pallas_call(
        _gdn_sample_prep_kernel,
        grid=(db // bt,),
        in_specs=[spec(t_len, GDN_CONV_CH), spec(GDN_CONV_W - 1, GDN_CONV_CH), spec(t_len, LANES),
                  pl.BlockSpec((GDN_CONV_W, GDN_CONV_CH), lambda i: (0, 0)),
                  pl.BlockSpec((2, LANES), lambda i: (0, 0))],
        out_specs=[spec(t_len, GDN_WIDTH)] * 5,
        out_shape=[out] * 5,
        compiler_params=pltpu.CompilerParams(dimension_semantics=("parallel",),
                                             vmem_limit_bytes=VMEM_LIMIT),
        name="gdn_sample_prep",
    )(x_t, st_t, ba_t, conv_w, arow)


def _gdn_sample_rec_kernel(s0_ref, qt_ref, kt_ref, v_ref, beta_ref, dec_ref, o_ref, s_ref):
    bt = s0_ref.shape[0]
    t_len = v_ref.shape[1]

    def per_batch(bi, carry):
        v, beta, dec = v_ref[bi], beta_ref[bi], dec_ref[bi]
        outs = []
        for h in range(GDN_HEADS):
            cols = slice(h * HEAD_DIM, (h + 1) * HEAD_DIM)
            s = s0_ref[bi, h]
            kt = kt_ref[bi, h]
            qt = qt_ref[bi, h]
            rows = []
            for t in range(t_len):
                s = s * dec[t:t + 1, cols]
                kcol = kt[:, t:t + 1]
                kv = jnp.sum(kcol * s, axis=0, keepdims=True)
                upd = (v[t:t + 1, cols] - kv) * beta[t:t + 1, cols]
                s = s + kcol * upd
                rows.append(jnp.sum(qt[:, t:t + 1] * s, axis=0, keepdims=True))
            s_ref[bi, h] = s
            outs.append(jnp.concatenate(rows, axis=0))
        o_ref[bi] = jnp.concatenate(outs, axis=1)
        return carry

    lax.fori_loop(0, bt, per_batch, 0)


def _gdn_sample_rec(s0, qt, kt, v, beta, dec):
    db, t_len, _ = v.shape
    bt = min(db, 8)
    assert db % bt == 0
    st_spec = pl.BlockSpec((bt, GDN_HEADS, HEAD_DIM, HEAD_DIM), lambda i: (i, 0, 0, 0))
    tr_spec = pl.BlockSpec((bt, GDN_HEADS, HEAD_DIM, t_len), lambda i: (i, 0, 0, 0))
    tok_spec = pl.BlockSpec((bt, t_len, GDN_WIDTH), lambda i: (i, 0, 0))
    return pl.pallas_call(
        _gdn_sample_rec_kernel,
        grid=(db // bt,),
        in_specs=[st_spec, tr_spec, tr_spec, tok_spec, tok_spec, tok_spec],
        out_specs=[tok_spec, st_spec],
        out_shape=[jax.ShapeDtypeStruct((db, t_len, GDN_WIDTH), F32),
                   jax.ShapeDtypeStruct(s0.shape, F32)],
        compiler_params=pltpu.CompilerParams(dimension_semantics=("parallel",),
                                             vmem_limit_bytes=VMEM_LIMIT),
        name="gdn_sample_rec",
    )(s0, qt, kt, v, beta, dec)


PAGES_PER_STEP = 16
PAGE_ROWS = PAGE_SIZE * MOBA_HEADS
PAGES_PER_BLOCK = MOBA_BLOCK // PAGE_SIZE
BLOCKS_PER_STEP = PAGES_PER_STEP // PAGES_PER_BLOCK
STEP_ROWS = PAGES_PER_STEP * PAGE_ROWS
HEAD_SHIFT = MOBA_HEADS.bit_length() - 1
assert 1 << HEAD_SHIFT == MOBA_HEADS and SUBLANES == 2 * MOBA_HEADS and PAGES_PER_BLOCK == 2


PAGE_RING = 3


def _moba_sample_kernel(pt_ref, q_ref, kn_ref, vn_ref, ck_ref, cv_ref, o_ref,
                        kwin_ref, vwin_ref, ksem, vsem,
                        kbuf_ref, vbuf_ref, kmean_ref, mem_ref, m_ref, l_ref, acc_ref):
    bp, g = pl.program_id(0), pl.program_id(1)
    n_batch = pl.num_programs(0) - 1
    n_steps = pl.num_programs(1)
    n_pages = n_steps * PAGES_PER_STEP
    lin = bp * n_steps + g

    def page_copies(step, ring_slot, stage):
        sb, sg = lax.div(step, n_steps), lax.rem(step, n_steps)
        if stage == 1:
            needed, row, cache, win, sem = sb < n_batch, sb, ck_ref, kwin_ref, ksem
        else:
            needed, row, cache, win, sem = sb > 0, sb - 1, cv_ref, vwin_ref, vsem
        base = jnp.where(needed, row, 0) * n_pages + sg * PAGES_PER_STEP
        return needed, [pltpu.make_async_copy(cache.at[pt_ref[base + pg]], win.at[ring_slot, pg],
                                              sem.at[ring_slot]) for pg in range(PAGES_PER_STEP)]

    def request(step, ring_slot):
        for stage in (1, 2):
            needed, copies = page_copies(step, ring_slot, stage)

            @pl.when(needed)
            def _():
                for n, c in enumerate(copies):
                    c.start(priority=n % 2)

    ring_slot = lax.rem(lin, PAGE_RING)

    @pl.when(lin == 0)
    def _():
        for step in range(PAGE_RING - 1):
            request(jnp.int32(step), step)

    ahead = lin + (PAGE_RING - 1)

    @pl.when(ahead < (n_batch + 1) * n_steps)
    def _():
        request(ahead, lax.rem(ahead, PAGE_RING))

    for stage in (1, 2):
        needed, copies = page_copies(lin, ring_slot, stage)

        @pl.when(needed)
        def _():
            for c in copies:
                c.wait()

    rows = q_ref.shape[1]
    t_len = rows // MOBA_HEADS
    scale = HEAD_DIM ** -0.5
    t_shift = t_len.bit_length() - 1
    assert 1 << t_shift == t_len and kn_ref.shape[1] == rows
    step_lanes = BLOCKS_PER_STEP * MOBA_HEADS
    par = jnp.bitwise_and(bp, 1)
    row_i = lax.broadcasted_iota(jnp.int32, (rows, LANES), 0)
    lane_i = lax.broadcasted_iota(jnp.int32, (rows, LANES), 1)
    row_head = lax.shift_right_logical(row_i, t_shift)
    lane_head = jnp.bitwise_and(lane_i, MOBA_HEADS - 1)

    @pl.when(jnp.logical_and(bp == 0, g == 0))
    def _():
        kmean_ref[...] = jnp.zeros(kmean_ref.shape, F32)

    @pl.when(jnp.logical_and(bp > 0, g == 0))
    def _():
        q = q_ref[0]
        q16 = q.astype(BF16)
        n_cand = n_steps * step_lanes
        valid = jnp.logical_and(lane_head == row_head, lane_i < n_cand)
        member = _top3_members(_dot_nt_f32(q, kmean_ref[...]), valid)
        mem_ref[...] = jnp.where(member, 1.0, 0.0)
        pad = jnp.zeros((LANES - rows, HEAD_DIM), F32)
        kn = jnp.concatenate([kn_ref[0], pad], axis=0).astype(BF16)
        vn = jnp.concatenate([vn_ref[0], pad], axis=0).astype(BF16)
        s = _dot_nt(q16, kn) * scale
        ok = jnp.logical_and(lane_head == row_head, lane_i < rows)
        ok = jnp.logical_and(ok, lax.shift_right_logical(lane_i, HEAD_SHIFT)
                             <= jnp.bitwise_and(row_i, t_len - 1))
        s = jnp.where(ok, s, -jnp.inf)
        m = jnp.max(s, axis=1, keepdims=True)
        p = jnp.exp(s - m)
        m_ref[...] = jnp.broadcast_to(m, m_ref.shape)
        l_ref[...] = jnp.broadcast_to(jnp.sum(p, axis=1, keepdims=True), l_ref.shape)
        acc_ref[...] = _dot(p.astype(BF16), vn)

    @pl.when(bp > 0)
    def _():
        for pg in range(PAGES_PER_STEP):
            vbuf_ref[pg * PAGE_ROWS:(pg + 1) * PAGE_ROWS, :] = vwin_ref[ring_slot, pg].astype(BF16)
        member = mem_ref[...]
        blk_rows = PAGES_PER_BLOCK * PAGE_ROWS
        col_head = jnp.bitwise_and(lax.broadcasted_iota(jnp.int32, (rows, blk_rows), 1), MOBA_HEADS - 1)
        own_head = col_head == lax.shift_right_logical(
            lax.broadcasted_iota(jnp.int32, (rows, blk_rows), 0), t_shift)
        cand_blk = lax.shift_right_logical(lane_i, HEAD_SHIFT)
        bias = []
        for n in range(BLOCKS_PER_STEP):
            picked = jnp.sum(jnp.where(cand_blk == g * BLOCKS_PER_STEP + n, member, 0.0),
                             axis=1, keepdims=True) > 0.0
            bias.append(jnp.where(jnp.logical_and(own_head, picked), 0.0, -jnp.inf))
        s = _dot_nt(q_ref[0].astype(BF16), kbuf_ref[1 - par, g]) * scale + jnp.concatenate(bias, axis=1)
        m = m_ref[:, 0:1]
        m_new = jnp.maximum(m, jnp.max(s, axis=1, keepdims=True))
        alpha = jnp.exp(m - m_new)
        p = jnp.exp(s - m_new)
        l = l_ref[:, 0:1] * alpha + jnp.sum(p, axis=1, keepdims=True)
        acc = acc_ref[...] * alpha + _dot(p.astype(BF16), vbuf_ref[...])
        m_ref[...] = jnp.broadcast_to(m_new, m_ref.shape)
        l_ref[...] = jnp.broadcast_to(l, l_ref.shape)
        acc_ref[...] = acc

        @pl.when(g == n_steps - 1)
        def _():
            o_ref[0] = acc / l

    @pl.when(bp < n_batch)
    def _():
        low = lax.broadcasted_iota(jnp.int32, (SUBLANES, HEAD_DIM), 0) < MOBA_HEADS

        def block_mean(n):
            tot = None
            for r in range(PAGES_PER_BLOCK):
                pg = n * PAGES_PER_BLOCK + r
                page = kwin_ref[ring_slot, pg]
                kbuf_ref[par, g, pg * PAGE_ROWS:(pg + 1) * PAGE_ROWS, :] = page.astype(BF16)
                part = jnp.sum(page.reshape(PAGE_ROWS // SUBLANES, SUBLANES, HEAD_DIM), axis=0)
                tot = part if tot is None else tot + part
            return (tot + pltpu.roll(tot, MOBA_HEADS, 0)) * (1.0 / MOBA_BLOCK)

        tiles = [jnp.where(low, block_mean(2 * n), block_mean(2 * n + 1))
                 for n in range(BLOCKS_PER_STEP // 2)]
        kmean_ref[pl.ds(pl.multiple_of(g * step_lanes, step_lanes), step_lanes), :] = (
            jnp.concatenate(tiles, axis=0))


def _moba_sample(page_flat, q2, kn2, vn2, cache_k, cache_v, n_pages):
    db, rows, _ = q2.shape
    assert n_pages % PAGES_PER_STEP == 0
    n_steps = n_pages // PAGES_PER_STEP
    n_cand = n_steps * BLOCKS_PER_STEP * MOBA_HEADS
    assert rows <= LANES and n_cand <= LANES
    tok = pl.BlockSpec((1, rows, HEAD_DIM), lambda bp, g, pt: (jnp.maximum(bp - 1, 0), 0, 0))
    window = pltpu.VMEM((PAGE_RING, PAGES_PER_STEP, PAGE_ROWS, HEAD_DIM), F32)
    return pl.pallas_call(
        _moba_sample_kernel,
        grid_spec=pltpu.PrefetchScalarGridSpec(
            num_scalar_prefetch=1,
            grid=(db + 1, n_steps),
            in_specs=[tok, tok, tok, pl.BlockSpec(memory_space=pl.ANY),
                      pl.BlockSpec(memory_space=pl.ANY)],
            out_specs=tok,
            scratch_shapes=[window, window,
                            pltpu.SemaphoreType.DMA((PAGE_RING,)), pltpu.SemaphoreType.DMA((PAGE_RING,)),
                            pltpu.VMEM((2, n_steps, STEP_ROWS, HEAD_DIM), BF16),
                            pltpu.VMEM((STEP_ROWS, HEAD_DIM), BF16),
                            pltpu.VMEM((LANES, HEAD_DIM), F32),
                            pltpu.VMEM((rows, LANES), F32),
                            pltpu.VMEM((rows, LANES), F32), pltpu.VMEM((rows, LANES), F32),
                            pltpu.VMEM((rows, HEAD_DIM), F32)],
        ),
        out_shape=jax.ShapeDtypeStruct((db, rows, HEAD_DIM), F32),
        compiler_params=pltpu.CompilerParams(dimension_semantics=("arbitrary", "arbitrary"),
                                             vmem_limit_bytes=VMEM_LIMIT),
        name="moba_sample",
    )(page_flat, q2, kn2, vn2, cache_k, cache_v)


def _merge_kernel(og_ref, za_ref, om_ref, zb_ref, x_ref, gw_ref, wo_ref, nf_ref, y_ref):
    og = og_ref[...].astype(F32)
    parts = []
    for h in range(GDN_HEADS):
        oh = og[:, h * HEAD_DIM:(h + 1) * HEAD_DIM]
        parts.append(oh * lax.rsqrt(jnp.mean(oh * oh, axis=-1, keepdims=True) + NORM_EPS) * gw_ref[...])
    ga = jnp.concatenate(parts, axis=1) * _silu(za_ref[...].astype(F32))
    gb = om_ref[...].astype(F32) * _silu(zb_ref[...].astype(F32))
    cat = jnp.concatenate([ga, gb], axis=1).astype(BF16)
    hp = x_ref[...] + _dot(cat, wo_ref[...])
    y_ref[...] = hp * lax.rsqrt(jnp.mean(hp * hp, axis=-1, keepdims=True) + NORM_EPS) * nf_ref[...]


def _merge(og, za, om, zb, x2d, gdn_norm_w, w_out16, norm_f_w):
    m, d = x2d.shape
    tm = MERGE_TILE if m % MERGE_TILE == 0 else ROW_TILE
    assert m % tm == 0
    row = lambda i: (i, 0)
    full = lambda i: (0, 0)
    half = pl.BlockSpec((tm, GDN_WIDTH), row)
    return pl.pallas_call(
        _merge_kernel,
        grid=(m // tm,),
        in_specs=[half, half, half, half, pl.BlockSpec((tm, d), row),
                  pl.BlockSpec((1, HEAD_DIM), full), pl.BlockSpec(w_out16.shape, full),
                  pl.BlockSpec((1, d), full)],
        out_specs=pl.BlockSpec((tm, d), row),
        out_shape=jax.ShapeDtypeStruct((m, d), F32),
        compiler_params=pltpu.CompilerParams(dimension_semantics=("parallel",),
                                             vmem_limit_bytes=VMEM_LIMIT),
        name="merge_out",
    )(og, za, om, zb, x2d, gdn_norm_w, w_out16, norm_f_w)


def _rope_tables(pos):
    half = HEAD_DIM // 2
    inv_freq = ROPE_THETA ** (-jnp.arange(half, dtype=F32) / half)
    ang = pos.astype(F32)[:, None] * inv_freq[None, :]
    cos, sin = jnp.cos(ang), jnp.sin(ang)
    return jnp.concatenate([cos, cos], axis=1), jnp.concatenate([-sin, sin], axis=1)


def _relayout_w_in(w):
    o = 0
    parts = {}
    for name, n in (("qkv", GDN_CONV_CH), ("za", GDN_WIDTH), ("b", GDN_HEADS), ("a", GDN_HEADS),
                    ("qb", MOBA_WIDTH), ("kb", MOBA_WIDTH), ("vb", MOBA_WIDTH), ("zb", MOBA_WIDTH)):
        parts[name] = w[:, o:o + n]
        o += n
    pad = jnp.zeros((w.shape[0], LANES - 2 * GDN_HEADS), w.dtype)
    return jnp.concatenate([parts["qkv"], parts["za"], parts["qb"], parts["kb"], parts["vb"],
                            parts["zb"], parts["b"], parts["a"], pad], axis=1).astype(BF16)


def kernel(x_prompt, x_sample, cache_k, cache_v, state_gdn, state_conv, page_table, norm_in_w, w_in,
           conv_w, a_log, dt_bias, gdn_norm_w, w_out, norm_f_w):
    b, s, d = x_prompt.shape
    db, t_len, _ = x_sample.shape
    depth = w_in.shape[0]
    assert depth == 1
    n_pages = page_table.shape[1]
    past_len = n_pages * PAGE_SIZE
    assert past_len % MOBA_BLOCK == 0
    assert t_len <= PAGE_SIZE and t_len >= GDN_CONV_W - 1 and (db * t_len) % ROW_TILE == 0
    assert ROW_TILE % t_len == 0

    w_p = _relayout_w_in(w_in[0])
    w_o = w_out[0].astype(BF16)
    nw = norm_in_w[0][None, :]
    cw = conv_w[0]
    zeros4 = jnp.zeros((GDN_HEADS,), F32)
    lane_pad = jnp.zeros((LANES - 2 * GDN_HEADS,), F32)
    arow = jnp.stack([jnp.concatenate([zeros4, a_log[0], lane_pad]),
                      jnp.concatenate([zeros4, dt_bias[0], lane_pad])])
    acol = jnp.stack([jnp.concatenate([zeros4, a_log[0]]),
                      jnp.concatenate([zeros4, dt_bias[0]])])[:, :, None]
    gw = gdn_norm_w[0][None, :]
    nf = norm_f_w[None, :]

    cos_p, sin_p = _rope_tables(jnp.arange(s, dtype=jnp.int32))
    (qkv_p, za_p, q_p, k_p, v_p, zb_p, ba_p, kbf_p, vbf_p, kmean_p) = _inproj(
        x_prompt.reshape(b * s, d), nw, w_p, cos_p, sin_p, prompt=True)
    qkv_p3 = qkv_p.reshape(b, s, GDN_CONV_CH)
    ba_p3 = ba_p.reshape(b, s, LANES)
    bat_p = jnp.transpose(ba_p3[:, :, :SUBLANES], (0, 2, 1))
    n_m, intra, rhs, qdec, kdec, glast = _gdn_prep(qkv_p3, ba_p3, bat_p, cw, arow, acol)
    bh, nc, c, _ = n_m.shape
    t_m = _tri_inv(jnp.transpose(n_m.reshape(bh * nc, c, c), (1, 2, 0)))
    t_m = jnp.transpose(t_m, (2, 0, 1)).reshape(bh, nc, c, c)
    o_gdn_p, s_gdn_p = _gdn_scan(t_m, rhs, qdec, kdec, intra, glast, b)
    o_moba_p = _moba_prompt(q_p.reshape(b, s, MOBA_WIDTH), kbf_p.reshape(b, s, MOBA_WIDTH),
                            vbf_p.reshape(b, s, MOBA_WIDTH),
                            kmean_p.reshape(b, s // MOBA_BLOCK, MOBA_WIDTH))
    y_p = _merge(o_gdn_p.reshape(b * s, GDN_WIDTH), za_p, o_moba_p.reshape(b * s, MOBA_WIDTH), zb_p,
                 x_prompt.reshape(b * s, d), gw, w_o, nf)

    cos_s, sin_s = _rope_tables(past_len + jnp.arange(t_len, dtype=jnp.int32))
    reps = ROW_TILE // t_len
    (qkv_s, za_s, q_s, k_s, v_s, zb_s, ba_s) = _inproj(
        x_sample.reshape(db * t_len, d), nw, w_p, jnp.tile(cos_s, (reps, 1)), jnp.tile(sin_s, (reps, 1)),
        prompt=False)
    qkv_s3 = qkv_s.reshape(db, t_len, GDN_CONV_CH)
    tb = lambda a: jnp.transpose(a, (1, 0, 2))
    qg, kg, vg, beta_t, dec_t = _gdn_sample_prep(
        tb(qkv_s3), tb(state_conv[0]), tb(ba_s.reshape(db, t_len, LANES)), cw, arow)
    to_cols = lambda a: jnp.transpose(a.reshape(t_len, db, GDN_HEADS, HEAD_DIM), (1, 2, 3, 0))
    o_gdn_s, s_gdn_s = _gdn_sample_rec(state_gdn[0], to_cols(qg), to_cols(kg), tb(vg), tb(beta_t),
                                       tb(dec_t))

    page_flat = page_table.reshape(-1).astype(jnp.int32)
    ck = cache_k.reshape(cache_k.shape[1], PAGE_ROWS, HEAD_DIM)
    cv = cache_v.reshape(cache_v.shape[1], PAGE_ROWS, HEAD_DIM)
    rows_s = t_len * MOBA_HEADS
    q2 = jnp.transpose(q_s.reshape(db, t_len, MOBA_HEADS, HEAD_DIM), (0, 2, 1, 3))
    o2 = _moba_sample(page_flat, q2.reshape(db, rows_s, HEAD_DIM), k_s.reshape(db, rows_s, HEAD_DIM),
                      v_s.reshape(db, rows_s, HEAD_DIM), ck, cv, n_pages)
    o_moba_s = jnp.transpose(o2.reshape(db, MOBA_HEADS, t_len, HEAD_DIM), (0, 2, 1, 3))
    y_s = _merge(o_gdn_s.reshape(db * t_len, GDN_WIDTH), za_s, o_moba_s.reshape(db * t_len, MOBA_WIDTH),
                 zb_s, x_sample.reshape(db * t_len, d), gw, w_o, nf)

    ctx = GDN_CONV_W - 1
    heads = lambda a, n, t: a.reshape(1, n, t, MOBA_HEADS, HEAD_DIM)
    return (y_p.reshape(b, s, d), y_s.reshape(db, t_len, d),
            heads(k_p, b, s), heads(v_p, b, s),
            s_gdn_p.reshape(1, b, GDN_HEADS, HEAD_DIM, HEAD_DIM),
            qkv_p3[:, s - ctx:, :][None],
            heads(k_s, db, t_len), heads(v_s, db, t_len),
            s_gdn_s[None],
            qkv_s3[:, t_len - ctx:, :][None])
```
